```python
import math
import jax, jax.numpy as jnp
from jax import lax
import numpy as np

D_MODEL = 2048
BATCH = 2
SEQ = 16384
DEPTH = 1
DEC_BATCH = 32
DEC_SEQ = 16
PAST_LEN = 2048

CHUNK = 64
Q_BLOCK = 128
N_A = 8
HD_A = 128
H_IDX = 16
D_IDX = 64
TOPK_MAX = 256
N_B = 4
HD_B = 256
D_FF = 5504
NUM_BUCKETS = 32
MAX_DISTANCE = 128
EPS = 1e-6
NEG = -1e30
COLS = (N_A * HD_A, N_A * HD_A, N_A * HD_A, H_IDX * D_IDX, D_IDX, H_IDX,
        N_B * HD_B, N_B * HD_B, N_B * HD_B, N_B * HD_B, N_B, N_B)
D_IN = sum(COLS)
D_MIX = N_A * HD_A + N_B * HD_B

kernel_name = 'hybrid_dsa_mlstm_macaron_step'


def rms_norm(x, g):
    xf = x.astype(jnp.float32)
    y = xf * lax.rsqrt(jnp.mean(xf * xf, axis=-1, keepdims=True) + EPS)
    return (y * g.astype(jnp.float32)).astype(x.dtype)


def swiglu(x, w1, w3, w2):
    return (jax.nn.silu(x @ w1) * (x @ w3)) @ w2


def split_cols(p):
    out, start = [], 0
    for c in COLS:
        out.append(p[..., start:start + c])
        start += c
    return out


def t5_bucket(rel):
    nb = NUM_BUCKETS // 2
    max_exact = nb // 2
    ret = (rel > 0).astype(jnp.int32) * nb
    n = jnp.abs(rel)
    large = max_exact + (jnp.log(jnp.maximum(n, 1).astype(jnp.float32) / max_exact)
                         / math.log(MAX_DISTANCE / max_exact) * (nb - max_exact)).astype(jnp.int32)
    large = jnp.minimum(large, nb - 1)
    return ret + jnp.where(n < max_exact, n, large)


def mixer_inputs(x, g_ffn1, w1_ffn1, w3_ffn1, w2_ffn1, g_mix, w_in, g_q, g_k, g_kidx, b_i, b_f):
    B, S = x.shape[0], x.shape[1]
    x1 = x + 0.5 * swiglu(rms_norm(x, g_ffn1), w1_ffn1, w3_ffn1, w2_ffn1)
    qa, ka, va, qi, ki, wi, qb, kb, vb, ob, ib, fb = split_cols(rms_norm(x1, g_mix) @ w_in)
    attn_in = (rms_norm(qa.reshape(B, S, N_A, HD_A), g_q),
               rms_norm(ka.reshape(B, S, N_A, HD_A), g_k),
               va.reshape(B, S, N_A, HD_A),
               qi.reshape(B, S, H_IDX, D_IDX),
               rms_norm(ki, g_kidx),
               wi)
    mlstm_in = (qb.reshape(B, S, N_B, HD_B),
                kb.reshape(B, S, N_B, HD_B) * HD_B ** -0.5,
                vb.reshape(B, S, N_B, HD_B),
                (ib + b_i).astype(jnp.float32),
                jax.nn.log_sigmoid((fb + b_f).astype(jnp.float32)))
    gate_o = jax.nn.sigmoid(ob).reshape(B, S, N_B, HD_B)
    return x1, attn_in, mlstm_in, gate_o


def mixer_output(x1, attn, h_b, gate_o, g_h, w_out, g_ffn2, w1_ffn2, w3_ffn2, w2_ffn2):
    B, S = x1.shape[0], x1.shape[1]
    hb = (rms_norm(h_b, g_h) * gate_o.astype(jnp.float32)).astype(x1.dtype)
    mix = jnp.concatenate([attn.reshape(B, S, N_A * HD_A).astype(x1.dtype),
                           hb.reshape(B, S, N_B * HD_B)], axis=-1)
    x2 = x1 + mix @ w_out
    return x2 + 0.5 * swiglu(rms_norm(x2, g_ffn2), w1_ffn2, w3_ffn2, w2_ffn2)


def dsa_attend(q, k_all, v_all, qi, ki_all, wi, q_pos, top, rel_bias):
    L = k_all.shape[1]
    k_pos = jnp.arange(L, dtype=jnp.int32)
    score = jax.nn.relu(jnp.einsum('bqjd,bld->bqjl', qi, ki_all, preferred_element_type=jnp.float32))
    score = jnp.einsum('bqjl,bqj->bql', score, wi.astype(jnp.float32))
    q_chunk = q_pos // CHUNK
    admissible = (k_pos // CHUNK)[None, :] <= q_chunk[:, None]
    score = jnp.where(admissible[None], score, NEG)
    _, idx = lax.top_k(score, top)
    valid = (idx // CHUNK) <= q_chunk[None, :, None]
    gather = jax.vmap(lambda a, i: a[i])
    k_sel = gather(k_all, idx)
    v_sel = gather(v_all, idx)
    bias = rel_bias[t5_bucket(idx - q_pos[None, :, None])].astype(jnp.float32)
    logits = (jnp.einsum('bqhd,bqkhd->bqhk', q, k_sel, preferred_element_type=jnp.float32) * HD_A ** -0.5
              + jnp.moveaxis(bias, -1, 2))
    logits = jnp.where(valid[:, :, None, :], logits, NEG)
    p = jax.nn.softmax(logits, axis=-1)
    return jnp.einsum('bqhk,bqkhd->bqhd', p.astype(v_all.dtype), v_sel)


def dsa_prompt(q, k, v, qi, ki, wi, rel_bias):
    B, S = q.shape[0], q.shape[1]
    top = min(TOPK_MAX, S // 4)
    nb = S // Q_BLOCK
    to_blocks = lambda a: a.reshape((B, nb, Q_BLOCK) + a.shape[2:]).swapaxes(0, 1)
    pos = jnp.arange(S, dtype=jnp.int32).reshape(nb, Q_BLOCK)

    def block(args):
        qb, qib, wib, pb = args
        return dsa_attend(qb, k, v, qib, ki, wib, pb, top, rel_bias)

    out = lax.map(block, (to_blocks(q), to_blocks(qi), to_blocks(wi), pos))
    return out.swapaxes(0, 1).reshape(B, S, N_A, HD_A)


def mlstm_chunk(state, q, k, v, ig, lf):
    C, n, m = state
    q, k, v = q.astype(jnp.float32), k.astype(jnp.float32), v.astype(jnp.float32)
    L = q.shape[1]
    b = jnp.cumsum(lf, axis=1)
    causal = jnp.tril(jnp.ones((L, L), dtype=bool))[None, :, :, None]
    log_d = jnp.where(causal, b[:, :, None, :] - b[:, None, :, :] + ig[:, None, :, :], NEG)
    log_inter = b + m[:, None, :]
    m_t = jnp.maximum(log_inter, jnp.max(log_d, axis=2))
    d = jnp.exp(log_d - m_t[:, :, None, :])
    inter = jnp.exp(log_inter - m_t)
    s = jnp.einsum('bthd,bshd->btsh', q, k) * d
    num = jnp.einsum('btsh,bshe->bthe', s, v) + inter[..., None] * jnp.einsum('bthd,bhde->bthe', q, C)
    den = jnp.sum(s, axis=2) + inter * jnp.einsum('bthd,bhd->bth', q, n)
    h = num / jnp.maximum(jnp.abs(den), jnp.exp(-m_t))[..., None]
    m_new = m_t[:, -1]
    decay = jnp.exp(b[:, -1] + m - m_new)
    w_s = jnp.exp(b[:, -1:, :] - b + ig - m_new[:, None, :])
    C_new = decay[..., None, None] * C + jnp.einsum('bsh,bshd,bshe->bhde', w_s, k, v)
    n_new = decay[..., None] * n + jnp.einsum('bsh,bshd->bhd', w_s, k)
    return (C_new, n_new, m_new), h


def mlstm_scan(q, k, v, ig, lf):
    B, S = q.shape[0], q.shape[1]
    nc = S // CHUNK
    to_chunks = lambda a: a.reshape((B, nc, CHUNK) + a.shape[2:]).swapaxes(0, 1)
    init = (jnp.zeros((B, N_B, HD_B, HD_B), jnp.float32),
            jnp.zeros((B, N_B, HD_B), jnp.float32),
            jnp.zeros((B, N_B), jnp.float32))
    state, h = lax.scan(lambda st, xs: mlstm_chunk(st, *xs), init,
                        (to_chunks(q), to_chunks(k), to_chunks(v), to_chunks(ig), to_chunks(lf)))
    return h.swapaxes(0, 1).reshape(B, S, N_B, HD_B), state


def setup_inputs(seed: int = 0) -> dict:
    key = jax.random.key(seed)
    ks = jax.random.split(key, 32)
    f32 = jnp.float32
    nrm = lambda k, shape, s: jax.random.normal(k, shape, f32) * s
    gain = lambda k, d: 1.0 + 0.02 * jax.random.normal(k, (d,), f32)
    return {
        'x_prompt': nrm(ks[0], (BATCH, SEQ, D_MODEL), 1.0),
        'x_sample': nrm(ks[1], (DEC_BATCH, DEC_SEQ, D_MODEL), 1.0),
        'cache_k': nrm(ks[2], (DEC_BATCH, PAST_LEN, N_A, HD_A), 1.0),
        'cache_v': nrm(ks[3], (DEC_BATCH, PAST_LEN, N_A, HD_A), 1.0),
        'cache_k_idx': nrm(ks[4], (DEC_BATCH, PAST_LEN, D_IDX), 1.0),
        'state_C': nrm(ks[5], (DEC_BATCH, N_B, HD_B, HD_B), 0.1),
        'state_n': nrm(ks[6], (DEC_BATCH, N_B, HD_B), 0.1),
        'state_m': nrm(ks[7], (DEC_BATCH, N_B), 1.0),
        'g_ffn1': gain(ks[8], D_MODEL),
        'w1_ffn1': nrm(ks[9], (D_MODEL, D_FF), D_MODEL ** -0.5),
        'w3_ffn1': nrm(ks[10], (D_MODEL, D_FF), D_MODEL ** -0.5),
        'w2_ffn1': nrm(ks[11], (D_FF, D_MODEL), D_FF ** -0.5),
        'g_mix': gain(ks[12], D_MODEL),
        'w_in': nrm(ks[13], (D_MODEL, D_IN), D_MODEL ** -0.5),
        'g_q': gain(ks[14], HD_A),
        'g_k': gain(ks[15], HD_A),
        'g_kidx': gain(ks[16], D_IDX),
        'rel_bias': nrm(ks[17], (NUM_BUCKETS, N_A), 0.5),
        'b_i': nrm(ks[18], (N_B,), 0.1),
        'b_f': 3.0 + nrm(ks[19], (N_B,), 0.1),
        'g_h': gain(ks[20], HD_B),
        'w_out': nrm(ks[21], (D_MIX, D_MODEL), D_MIX ** -0.5),
        'g_ffn2': gain(ks[22], D_MODEL),
        'w1_ffn2': nrm(ks[23], (D_MODEL, D_FF), D_MODEL ** -0.5),
        'w3_ffn2': nrm(ks[24], (D_MODEL, D_FF), D_MODEL ** -0.5),
        'w2_ffn2': nrm(ks[25], (D_FF, D_MODEL), D_FF ** -0.5),
    }


def reference(x_prompt, x_sample, cache_k, cache_v, cache_k_idx, state_C, state_n, state_m,
              g_ffn1, w1_ffn1, w3_ffn1, w2_ffn1, g_mix, w_in, g_q, g_k, g_kidx, rel_bias,
              b_i, b_f, g_h, w_out, g_ffn2, w1_ffn2, w3_ffn2, w2_ffn2):
    x1p, (qa, ka_p, va_p, qi, ki_p, wi), (qb, kb, vb, ig, lf), og_p = mixer_inputs(
        x_prompt, g_ffn1, w1_ffn1, w3_ffn1, w2_ffn1, g_mix, w_in, g_q, g_k, g_kidx, b_i, b_f)
    attn_p = dsa_prompt(qa, ka_p, va_p, qi, ki_p, wi, rel_bias)
    hb_p, (C_p, n_p, m_p) = mlstm_scan(qb, kb, vb, ig, lf)
    y_prompt = mixer_output(x1p, attn_p, hb_p, og_p, g_h, w_out, g_ffn2, w1_ffn2, w3_ffn2, w2_ffn2)

    x1s, (qa_s, ka_s, va_s, qi_s, ki_s, wi_s), (qb_s, kb_s, vb_s, ig_s, lf_s), og_s = mixer_inputs(
        x_sample, g_ffn1, w1_ffn1, w3_ffn1, w2_ffn1, g_mix, w_in, g_q, g_k, g_kidx, b_i, b_f)
    P, T = cache_k.shape[1], x_sample.shape[1]
    k_all = jnp.concatenate([cache_k.astype(ka_s.dtype), ka_s], axis=1)
    v_all = jnp.concatenate([cache_v.astype(va_s.dtype), va_s], axis=1)
    ki_all = jnp.concatenate([cache_k_idx.astype(ki_s.dtype), ki_s], axis=1)
    q_pos = P + jnp.arange(T, dtype=jnp.int32)
    attn_s = dsa_attend(qa_s, k_all, v_all, qi_s, ki_all, wi_s, q_pos, min(TOPK_MAX, (P + T) // 4), rel_bias)
    st0 = (state_C.astype(jnp.float32), state_n.astype(jnp.float32), state_m.astype(jnp.float32))
    (C_s, n_s, m_s), hb_s = mlstm_chunk(st0, qb_s, kb_s, vb_s, ig_s, lf_s)
    y_sample = mixer_output(x1s, attn_s, hb_s, og_s, g_h, w_out, g_ffn2, w1_ffn2, w3_ffn2, w2_ffn2)

    sd = state_C.dtype
    return (y_prompt, y_sample,
            ka_p, va_p, ki_p, C_p.astype(sd), n_p.astype(sd), m_p.astype(sd),
            ka_s, va_s, ki_s, C_s.astype(sd), n_s.astype(sd), m_s.astype(sd))
```

```python
import functools
import math

import numpy as np
import jax
import jax.numpy as jnp
from jax import lax
from jax.experimental import pallas as pl
from jax.experimental.pallas import tpu as pltpu

F32 = jnp.float32
BF16 = jnp.bfloat16
I32 = jnp.int32

D_MODEL = 2048
CHUNK = 64
N_A, HD_A = 8, 128
H_IDX, D_IDX = 16, 64
TOPK_MAX = 256
N_B, HD_B = 4, 256
D_FF = 5504
NUM_BUCKETS, MAX_DISTANCE = 32, 128
EPS = 1e-6
NEG = -1e30
INT_MIN = -(2 ** 31)

LANES = 128
D_A = N_A * HD_A
D_B = N_B * HD_B
D_QI = H_IDX * D_IDX
FF_TILE = 512
D_FF_PAD = -(-D_FF // FF_TILE) * FF_TILE
TOK_TILE = 512
SUB = 256
VMEM_LIMIT = 56 * 1024 * 1024

SM_KI, SM_WI, SM_IG, SM_LF = 0, D_IDX, D_IDX + H_IDX, D_IDX + H_IDX + N_B
SM_END = SM_LF + N_B


def _cparams(sem):
    return pltpu.CompilerParams(dimension_semantics=sem, vmem_limit_bytes=VMEM_LIMIT)


def _rms(x, g):
    ms = jnp.mean(x * x, axis=-1, keepdims=True)
    return x * lax.rsqrt(ms + EPS) * g


def _ffn_body(x_ref, g_ref, w1_ref, w3_ref, w2_ref, o_ref, hn_ref):
    j = pl.program_id(1)

    @pl.when(j == 0)
    def _():
        hn_ref[...] = _rms(x_ref[...], g_ref[...]).astype(BF16)

    h = hn_ref[...]
    a = jnp.dot(h, w1_ref[...], preferred_element_type=F32)
    b = jnp.dot(h, w3_ref[...], preferred_element_type=F32)
    u = (a * jax.nn.sigmoid(a) * b).astype(BF16)
    part = jnp.dot(u, w2_ref[...], preferred_element_type=F32)

    @pl.when(j == 0)
    def _():
        o_ref[...] = part

    @pl.when(j > 0)
    def _():
        o_ref[...] += part

    @pl.when(j == pl.num_programs(1) - 1)
    def _():
        o_ref[...] = x_ref[...] + 0.5 * o_ref[...]


def _ffn(x, g, w1, w3, w2):
    n = x.shape[0]
    tm = min(TOK_TILE, n)
    grid = (n // tm, D_FF_PAD // FF_TILE)
    return pl.pallas_call(
        _ffn_body,
        grid=grid,
        in_specs=[
            pl.BlockSpec((tm, D_MODEL), lambda i, j: (i, 0)),
            pl.BlockSpec((1, D_MODEL), lambda i, j: (0, 0)),
            pl.BlockSpec((D_MODEL, FF_TILE), lambda i, j: (0, j)),
            pl.BlockSpec((D_MODEL, FF_TILE), lambda i, j: (0, j)),
            pl.BlockSpec((FF_TILE, D_MODEL), lambda i, j: (j, 0)),
        ],
        out_specs=pl.BlockSpec((tm, D_MODEL), lambda i, j: (i, 0)),
        out_shape=jax.ShapeDtypeStruct((n, D_MODEL), F32),
        scratch_shapes=[pltpu.VMEM((tm, D_MODEL), BF16)],
        compiler_params=_cparams(("parallel", "arbitrary")),
    )(x, g, w1, w3, w2)


def _prep_ffn(g, w1, w3, w2):
    pad = D_FF_PAD - D_FF
    return (g.reshape(1, D_MODEL),
            jnp.pad(w1.astype(BF16), ((0, 0), (0, pad))),
            jnp.pad(w3.astype(BF16), ((0, 0), (0, pad))),
            jnp.pad(w2.astype(BF16), ((0, pad), (0, 0))))


N_MAIN_GROUPS = 8


def _head_norm(p, g, heads, hd):
    outs = []
    for h in range(heads):
        outs.append(_rms(p[:, h * hd:(h + 1) * hd], g))
    return outs


def _inproj_body(x_ref, g_ref, wm_ref, wsh_ref, wsl_ref, gq_ref, gk_ref, sg_ref, sb_ref,
                 q_ref, kf_ref, k16_ref, vf_ref, v16_ref, qi_ref, qb_ref, kb_ref, vb_ref, og_ref, sm_ref,
                 hn_ref):
    j = pl.program_id(1)

    @pl.when(j == 0)
    def _():
        h32 = _rms(x_ref[...], g_ref[...])
        h_hi = h32.astype(BF16)
        hn_ref[...] = h_hi
        h_lo = (h32 - h_hi.astype(F32)).astype(BF16)
        ps = (jnp.dot(h_hi, wsh_ref[...], preferred_element_type=F32)
              + jnp.dot(h_lo, wsh_ref[...], preferred_element_type=F32)
              + jnp.dot(h_hi, wsl_ref[...], preferred_element_type=F32))
        lane = lax.broadcasted_iota(I32, ps.shape, 1)
        ms = jnp.sum(jnp.where(lane < SM_WI, ps * ps, 0.0), axis=-1, keepdims=True) * (1.0 / D_IDX)
        kin = ps * lax.rsqrt(ms + EPS) * sg_ref[...]
        z = ps + sb_ref[...]
        ls = jnp.minimum(z, 0.0) - jnp.log1p(jnp.exp(-jnp.abs(z)))
        sm_ref[...] = jnp.where(lane < SM_WI, kin,
                                jnp.where(lane < SM_IG, ps,
                                          jnp.where(lane < SM_LF, z,
                                                    jnp.where(lane < SM_END, ls, 0.0))))

    p = jnp.dot(hn_ref[...], wm_ref[...], preferred_element_type=F32)

    @pl.when(j == 0)
    def _():
        for h, qh in enumerate(_head_norm(p, gq_ref[...], N_A, HD_A)):
            q_ref[:, h * HD_A:(h + 1) * HD_A] = (qh * HD_A ** -0.5).astype(BF16)

    @pl.when(j == 1)
    def _():
        for h, kh in enumerate(_head_norm(p, gk_ref[...], N_A, HD_A)):
            kf_ref[:, h * HD_A:(h + 1) * HD_A] = kh
            k16_ref[:, h * HD_A:(h + 1) * HD_A] = kh.astype(BF16)

    @pl.when(j == 2)
    def _():
        vf_ref[...] = p
        v16_ref[...] = p.astype(BF16)

    @pl.when(j == 3)
    def _():
        qi_ref[...] = p.astype(BF16)

    @pl.when(j == 4)
    def _():
        qb_ref[...] = p.astype(BF16)

    @pl.when(j == 5)
    def _():
        kb_ref[...] = (p * HD_B ** -0.5).astype(BF16)

    @pl.when(j == 6)
    def _():
        vb_ref[...] = p.astype(BF16)

    @pl.when(j == 7)
    def _():
        og_ref[...] = jax.nn.sigmoid(p).astype(BF16)


def _inproj(x, g, wm, wsh, wsl, gq, gk, sg, sb):
    n = x.shape[0]
    tm = min(TOK_TILE, n)
    grid = (n // tm, N_MAIN_GROUPS)
    tok = lambda w: pl.BlockSpec((tm, w), lambda i, j: (i, 0))
    const = lambda a: pl.BlockSpec(a.shape, lambda i, j: (0,) * a.ndim)
    wide = lambda dt: jax.ShapeDtypeStruct((n, D_A), dt)
    return pl.pallas_call(
        _inproj_body,
        grid=grid,
        in_specs=[tok(D_MODEL), const(g),
                  pl.BlockSpec((D_MODEL, D_A), lambda i, j: (0, j)),
                  const(wsh), const(wsl), const(gq), const(gk), const(sg), const(sb)],
        out_specs=[tok(D_A)] * 10 + [tok(LANES)],
        out_shape=[wide(BF16), wide(F32), wide(BF16), wide(F32), wide(BF16), wide(BF16),
                   wide(BF16), wide(BF16), wide(BF16), wide(BF16),
                   jax.ShapeDtypeStruct((n, LANES), F32)],
        scratch_shapes=[pltpu.VMEM((tm, D_MODEL), BF16)],
        compiler_params=_cparams(("parallel", "arbitrary")),
    )(x, g, wm, wsh, wsl, gq, gk, sg, sb)


def _prep_inproj(g_mix, w_in, g_q, g_k, g_kidx, b_i, b_f):
    o_ki = 4 * D_A
    o_qb = o_ki + D_IDX + H_IDX
    o_ib = o_qb + 4 * D_B
    wm = jnp.concatenate([w_in[:, :o_ki], w_in[:, o_qb:o_ib]], axis=1).astype(BF16)
    ws = jnp.concatenate([w_in[:, o_ki:o_qb], w_in[:, o_ib:],
                          jnp.zeros((D_MODEL, LANES - SM_END), F32)], axis=1)
    wsh = ws.astype(BF16)
    wsl = (ws - wsh.astype(F32)).astype(BF16)
    sg = jnp.concatenate([g_kidx, jnp.ones((LANES - D_IDX,), F32)]).reshape(1, LANES)
    sb = jnp.concatenate([jnp.zeros((SM_IG,), F32), b_i, b_f,
                          jnp.zeros((LANES - SM_END,), F32)]).reshape(1, LANES)
    return (g_mix.reshape(1, D_MODEL), wm, wsh, wsl,
            g_q.reshape(1, HD_A), g_k.reshape(1, HD_A), sg, sb)


def _outproj_body(x_ref, a_ref, h_ref, wa_ref, wh_ref, o_ref):
    o_ref[...] = (x_ref[...]
                  + jnp.dot(a_ref[...], wa_ref[...], preferred_element_type=F32)
                  + jnp.dot(h_ref[...], wh_ref[...], preferred_element_type=F32))


def _outproj(x, a, h, wa, wh):
    n = x.shape[0]
    tm = min(TOK_TILE, n)
    return pl.pallas_call(
        _outproj_body,
        grid=(n // tm,),
        in_specs=[pl.BlockSpec((tm, D_MODEL), lambda i: (i, 0)),
                  pl.BlockSpec((tm, D_A), lambda i: (i, 0)),
                  pl.BlockSpec((tm, D_B), lambda i: (i, 0)),
                  pl.BlockSpec((D_A, D_MODEL), lambda i: (0, 0)),
                  pl.BlockSpec((D_B, D_MODEL), lambda i: (0, 0))],
        out_specs=pl.BlockSpec((tm, D_MODEL), lambda i: (i, 0)),
        out_shape=jax.ShapeDtypeStruct((n, D_MODEL), F32),
        compiler_params=_cparams(("parallel",)),
    )(x, a, h, wa, wh)


def _bucket_thresholds():
    nb = NUM_BUCKETS // 2
    max_exact = nb // 2
    span = nb - max_exact
    ratio = MAX_DISTANCE // max_exact
    out = []
    for m in range(1, span):
        n = max_exact
        while n ** span < max_exact ** span * ratio ** m:
            n += 1
        out.append(n)
    return tuple(out)


def _bias_body(tbl_ref, o_ref):
    o = pl.program_id(0)
    h = pl.program_id(1)
    nb = NUM_BUCKETS // 2
    max_exact = nb // 2
    row = lax.broadcasted_iota(I32, (SUB, SUB), 0)
    col = lax.broadcasted_iota(I32, (SUB, SUB), 1)
    rel = col - o * SUB - row
    n = jnp.abs(rel)
    large = jnp.full((SUB, SUB), max_exact, I32)
    for t in _bucket_thresholds():
        large = large + (n >= t).astype(I32)
    bucket = jnp.where(rel > 0, nb, 0) + jnp.where(n < max_exact, n, large)
    val = jnp.zeros((SUB, SUB), F32)
    for bk in range(NUM_BUCKETS):
        val = jnp.where(bucket == bk, tbl_ref[bk, h], val)
    o_ref[0, 0] = val - tbl_ref[nb - 1, h]


def _bias_tiles(rel_bias):
    return pl.pallas_call(
        _bias_body,
        grid=(2, N_A),
        in_specs=[pl.BlockSpec(memory_space=pltpu.SMEM)],
        out_specs=pl.BlockSpec((1, 1, SUB, SUB), lambda o, h: (o, h, 0, 0)),
        out_shape=jax.ShapeDtypeStruct((2, N_A, SUB, SUB), F32),
    )(rel_bias)


def _dsa_body(ti_ref, tk_ref, tn_ref, tl_ref,
              q_ref, qi_ref, wi_ref, kit_ref, k_ref, v_ref, bias_ref, o_ref,
              key_ref, thr_ref, qs_ref, wb_ref, m_ref, l_ref, acc_ref,
              *, tq, tk, q_off, l_valid, topk):
    s = pl.program_id(1)
    i = ti_ref[s]
    kt = tk_ref[s]
    nsub = tn_ref[s]
    subs = tk // SUB
    q_pos0 = q_off + i * tq
    i_sub = q_off // SUB + (i * tq) // SUB
    rc = min(tq, 128)

    @pl.when(kt == 0)
    def _():
        m_ref[...] = jnp.full(m_ref.shape, NEG, F32)
        l_ref[...] = jnp.zeros(l_ref.shape, F32)
        acc_ref[...] = jnp.zeros(acc_ref.shape, F32)

        wi = wi_ref[0]
        for j in range(H_IDX):
            qs_ref[j * tq:(j + 1) * tq, :] = qi_ref[0, :, j * D_IDX:(j + 1) * D_IDX]
            wb_ref[j] = jnp.broadcast_to(wi[:, j:j + 1], (tq, LANES))

        qchunk = (q_pos0 + lax.broadcasted_iota(I32, (tq, SUB), 0)) >> 6
        lane = lax.broadcasted_iota(I32, (tq, SUB), 1)

        def score_tile(t, c):
            off = pl.multiple_of(t * SUB, SUB)
            s_all = jnp.dot(qs_ref[...], kit_ref[0, :, pl.ds(off, SUB)], preferred_element_type=F32)
            sc = jnp.zeros((tq, SUB), F32)
            for j in range(H_IDX):
                w = wb_ref[j]
                w = jnp.concatenate([w] * (SUB // LANES), axis=1)
                sc = sc + jnp.maximum(s_all[j * tq:(j + 1) * tq, :], 0.0) * w
            kpos = off + lane
            adm = ((kpos >> 6) <= qchunk) & (kpos < l_valid)
            bits = lax.bitcast_convert_type(sc, I32)
            okey = bits ^ ((bits >> 31) & 0x7FFFFFFF)
            key_ref[:, pl.ds(off, SUB)] = jnp.where(adm, okey, INT_MIN)
            return c

        lax.fori_loop(0, nsub, score_tile, 0)

        for r0 in range(0, tq, rc):
            def bit_step(it, thr):
                cand = thr + jnp.left_shift(jnp.int32(1), 31 - it)
                cand2 = jnp.concatenate([cand] * (SUB // LANES), axis=1)

                def count_tile(t, cnt):
                    off = pl.multiple_of(t * SUB, SUB)
                    ge = jnp.where(key_ref[r0:r0 + rc, pl.ds(off, SUB)] >= cand2, 1.0, 0.0)
                    for c in range(SUB // LANES):
                        cnt = cnt + ge[:, c * LANES:(c + 1) * LANES]
                    return cnt

                cnt = lax.fori_loop(0, nsub, count_tile, jnp.zeros((rc, LANES), F32))
                tot = jnp.sum(cnt, axis=1, keepdims=True)
                return jnp.where(tot >= float(topk), cand, thr)

            thr = lax.fori_loop(0, 32, bit_step, jnp.full((rc, LANES), INT_MIN, I32))
            thr_ref[r0:r0 + rc, :] = jnp.maximum(thr, INT_MIN + 1)

    n_here = jnp.clip(nsub - kt * subs, 0, subs)

    def sub_tile(u, c):
        g = kt * subs + u
        dsub = g - i_sub
        koff = pl.multiple_of(g * SUB, SUB)
        uoff = pl.multiple_of(u * SUB, SUB)
        thr = thr_ref[...]
        mask = key_ref[:, pl.ds(koff, SUB)] >= jnp.concatenate([thr] * (SUB // LANES), axis=1)

        def heads(near):
            for h in range(N_A):
                hs = slice(h * HD_A, (h + 1) * HD_A)
                qh = q_ref[0, :, hs]
                kh = k_ref[0, pl.ds(uoff, SUB), hs]
                vh = v_ref[0, pl.ds(uoff, SUB), hs]
                lg = lax.dot_general(qh, kh, (((1,), (1,)), ((), ())), preferred_element_type=F32)
                if near:
                    lg = lg + bias_ref[-dsub, h, 0:tq, :]
                lg = jnp.where(mask, lg, 2.0 * NEG)
                m_old = m_ref[h]
                m_new = jnp.maximum(m_old, jnp.max(lg, axis=1, keepdims=True))
                p = jnp.exp(lg - jnp.concatenate([m_new] * (SUB // LANES), axis=1))
                alpha = jnp.exp(m_old - m_new)
                l_ref[h] = alpha * l_ref[h] + jnp.sum(p, axis=1, keepdims=True)
                acc_ref[:, hs] = alpha * acc_ref[:, hs] + jnp.dot(p.astype(BF16), vh,
                                                                 preferred_element_type=F32)
                m_ref[h] = m_new

        @pl.when(dsub >= -1)
        def _():
            heads(True)

        @pl.when(dsub < -1)
        def _():
            heads(False)

        return c

    lax.fori_loop(0, n_here, sub_tile, 0)

    @pl.when(tl_ref[s] == 1)
    def _():
        for h in range(N_A):
            hs = slice(h * HD_A, (h + 1) * HD_A)
            o_ref[0, :, hs] = (acc_ref[:, hs] / l_ref[h]).astype(BF16)


def _dsa(q, qi, wi, kit, k, v, bias, *, tq, tk, q_off, l_valid, topk):
    bsz, sq = q.shape[0], q.shape[1]
    sk = k.shape[1]
    assert sq % tq == 0 and sk % tk == 0 and tk % SUB == 0 and q_off % SUB == 0
    assert tq % SUB == 0 or sq == tq
    subs = tk // SUB
    ti, tkk, tn, tl = [], [], [], []
    for i in range(sq // tq):
        q_last = q_off + (i + 1) * tq - 1
        lim = min(l_valid, (q_last // CHUNK + 1) * CHUNK)
        nsub = -(-lim // SUB)
        nkt = -(-nsub // subs)
        for t in range(nkt):
            ti.append(i), tkk.append(t), tn.append(nsub), tl.append(int(t == nkt - 1))
    tabs = [jnp.asarray(np.asarray(a, np.int32)) for a in (ti, tkk, tn, tl)]
    body = functools.partial(_dsa_body, tq=tq, tk=tk, q_off=q_off, l_valid=l_valid, topk=topk)
    qmap = lambda b, s, ti, tk_, tn, tl: (b, ti[s], 0)
    kmap = lambda b, s, ti, tk_, tn, tl: (b, tk_[s], 0)
    grid_spec = pltpu.PrefetchScalarGridSpec(
        num_scalar_prefetch=4,
        grid=(bsz, len(ti)),
        in_specs=[pl.BlockSpec((1, tq, D_A), qmap),
                  pl.BlockSpec((1, tq, D_QI), qmap),
                  pl.BlockSpec((1, tq, H_IDX), qmap),
                  pl.BlockSpec((1, D_IDX, sk), lambda b, s, *_: (b, 0, 0)),
                  pl.BlockSpec((1, tk, D_A), kmap),
                  pl.BlockSpec((1, tk, D_A), kmap),
                  pl.BlockSpec((2, N_A, SUB, SUB), lambda b, s, *_: (0, 0, 0, 0))],
        out_specs=pl.BlockSpec((1, tq, D_A), qmap),
        scratch_shapes=[pltpu.VMEM((tq, sk), I32),
                        pltpu.VMEM((tq, LANES), I32),
                        pltpu.VMEM((H_IDX * tq, D_IDX), BF16),
                        pltpu.VMEM((H_IDX, tq, LANES), F32),
                        pltpu.VMEM((N_A, tq, LANES), F32),
                        pltpu.VMEM((N_A, tq, LANES), F32),
                        pltpu.VMEM((tq, D_A), F32)])
    return pl.pallas_call(
        body,
        grid_spec=grid_spec,
        out_shape=jax.ShapeDtypeStruct((bsz, sq, D_A), BF16),
        compiler_params=_cparams(("parallel", "arbitrary")),
    )(*tabs, q, qi, wi, kit, k, v, bias)


def _mlstm_body(q_ref, k_ref, kt_ref, v_ref, og_ref, sm_ref, gr_ref, gh_ref, c0_ref, n0_ref, m0_ref,
                hb_ref, co_ref, no_ref, mo_ref, c_s, n_s, m_s, *, lc):
    c = pl.program_id(1)

    @pl.when(c == 0)
    def _():
        c_s[...] = c0_ref[0]
        for h in range(N_B):
            n_s[h] = jnp.broadcast_to(n0_ref[0, h:h + 1, :], (8, HD_B))
            m_s[h] = jnp.broadcast_to(m0_ref[0, h:h + 1, :], (8, LANES))

    row = lax.broadcasted_iota(I32, (lc, lc), 0)
    col = lax.broadcasted_iota(I32, (lc, lc), 1)
    causal = col <= row
    sm = sm_ref[0]
    gr = gr_ref[0]
    hp = lax.Precision.HIGHEST
    b_cols = jnp.dot(causal.astype(F32), sm, precision=hp, preferred_element_type=F32)
    b_rows = jnp.dot(gr, (row <= col).astype(F32), precision=hp, preferred_element_type=F32)

    for h in range(N_B):
        hs = slice(h * HD_B, (h + 1) * HD_B)
        m_prev = m_s[h][0:1, 0:1]
        bc = b_cols[:, SM_LF + h:SM_LF + h + 1]
        ig_c = sm[:, SM_IG + h:SM_IG + h + 1]
        br = b_rows[N_B + h:N_B + h + 1, :]
        ig_r = gr[h:h + 1, :]
        log_d = jnp.where(causal, bc - br + ig_r, NEG)
        log_inter = bc + m_prev
        m_t = jnp.maximum(log_inter, jnp.max(log_d, axis=1, keepdims=True))
        d = jnp.exp(log_d - m_t)
        inter = jnp.exp(log_inter - m_t)
        qh = q_ref[0, :, hs]
        kth = kt_ref[0, hs, :]
        vh = v_ref[0, :, hs]
        sc = jnp.dot(qh, kth, preferred_element_type=F32) * d
        ch = c_s[h]
        nrow = n_s[h][0:1, :]
        num = (jnp.dot(sc.astype(BF16), vh, preferred_element_type=F32)
               + inter * jnp.dot(qh, ch.astype(BF16), preferred_element_type=F32))
        den = (jnp.sum(sc, axis=1, keepdims=True)
               + inter * jnp.sum(qh.astype(F32) * nrow, axis=1, keepdims=True))
        hh = num / jnp.maximum(jnp.abs(den), jnp.exp(-m_t))
        hb = _rms(hh, gh_ref[...]) * og_ref[0, :, hs].astype(F32)
        hb_ref[0, :, hs] = hb.astype(BF16)

        m_new = m_t[lc - 1:lc, :]
        b_last = bc[lc - 1:lc, :]
        decay = jnp.exp(b_last + m_prev - m_new)
        w_r = jnp.exp(b_last - br + ig_r - m_new)
        w_c = jnp.exp(b_last - bc + ig_c - m_new)
        kw = (kth.astype(F32) * w_r).astype(BF16)
        c_s[h] = decay * ch + jnp.dot(kw, vh, preferred_element_type=F32)
        n_new = decay * nrow + jnp.sum(k_ref[0, :, hs].astype(F32) * w_c, axis=0, keepdims=True)
        n_s[h] = jnp.broadcast_to(n_new, (8, HD_B))
        m_s[h] = jnp.broadcast_to(m_new, (8, LANES))

    @pl.when(c == pl.num_programs(1) - 1)
    def _():
        co_ref[0] = c_s[...]
        for h in range(N_B):
            no_ref[0, h:h + 1, :] = n_s[h][0:1, :]
            mo_ref[0, h:h + 1, :] = m_s[h][0:1, :]


def _mlstm(q, k, kt, v, og, sm, gr, g_h, c0, n0, m0, *, lc):
    bsz, s = q.shape[0], q.shape[1]
    tokb = lambda w: pl.BlockSpec((1, lc, w), lambda b, c: (b, c, 0))
    st3 = lambda w: pl.BlockSpec((1, N_B, w), lambda b, c: (b, 0, 0))
    st4 = pl.BlockSpec((1, N_B, HD_B, HD_B), lambda b, c: (b, 0, 0, 0))
    return pl.pallas_call(
        functools.partial(_mlstm_body, lc=lc),
        grid=(bsz, s // lc),
        in_specs=[tokb(D_B), tokb(D_B),
                  pl.BlockSpec((1, D_B, lc), lambda b, c: (b, 0, c)),
                  tokb(D_B), tokb(D_B), tokb(LANES),
                  pl.BlockSpec((1, 2 * N_B, lc), lambda b, c: (b, 0, c)),
                  pl.BlockSpec((1, HD_B), lambda b, c: (0, 0)),
                  st4, st3(HD_B), st3(LANES)],
        out_specs=[tokb(D_B), st4, st3(HD_B), st3(LANES)],
        out_shape=[jax.ShapeDtypeStruct((bsz, s, D_B), BF16),
                   jax.ShapeDtypeStruct((bsz, N_B, HD_B, HD_B), F32),
                   jax.ShapeDtypeStruct((bsz, N_B, HD_B), F32),
                   jax.ShapeDtypeStruct((bsz, N_B, LANES), F32)],
        scratch_shapes=[pltpu.VMEM((N_B, HD_B, HD_B), F32),
                        pltpu.VMEM((N_B, 8, HD_B), F32),
                        pltpu.VMEM((N_B, 8, LANES), F32)],
        compiler_params=_cparams(("parallel", "arbitrary")),
    )(q, k, kt, v, og, sm, gr, g_h, c0, n0, m0)


def _mixer_front(x, ffn1, inp):
    x1 = _ffn(x, *ffn1)
    return (x1,) + tuple(_inproj(x1, *inp))


def kernel(x_prompt, x_sample, cache_k, cache_v, cache_k_idx, state_C, state_n, state_m, g_ffn1, w1_ffn1, w3_ffn1, w2_ffn1, g_mix, w_in, g_q, g_k, g_kidx, rel_bias, b_i, b_f, g_h, w_out, g_ffn2, w1_ffn2, w3_ffn2, w2_ffn2):
    bsz, seq = x_prompt.shape[0], x_prompt.shape[1]
    dbs, dseq = x_sample.shape[0], x_sample.shape[1]
    past = cache_k.shape[1]

    ffn1 = _prep_ffn(g_ffn1, w1_ffn1, w3_ffn1, w2_ffn1)
    ffn2 = _prep_ffn(g_ffn2, w1_ffn2, w3_ffn2, w2_ffn2)
    inp = _prep_inproj(g_mix, w_in, g_q, g_k, g_kidx, b_i, b_f)
    wa = w_out[:D_A].astype(BF16)
    wh = w_out[D_A:].astype(BF16)
    gh = g_h.reshape(1, HD_B)
    bias = _bias_tiles(rel_bias)

    n_p = bsz * seq
    x1, q, kf, k16, vf, v16, qi, qb, kb, vb, og, sm = _mixer_front(x_prompt.reshape(n_p, D_MODEL), ffn1, inp)
    r3 = lambda a, b_, s_: a.reshape(b_, s_, a.shape[-1])
    sm3 = r3(sm, bsz, seq)
    kit = jnp.swapaxes(sm3[:, :, :D_IDX].astype(BF16), 1, 2)
    tq_p = min(SUB, seq)
    attn = _dsa(r3(q, bsz, seq), r3(qi, bsz, seq), sm3[:, :, SM_WI:SM_IG], kit,
                r3(k16, bsz, seq), r3(v16, bsz, seq), bias,
                tq=tq_p, tk=min(1024, seq), q_off=0, l_valid=seq, topk=min(TOPK_MAX, seq // 4))
    gr = jnp.swapaxes(sm3[:, :, SM_IG:SM_END], 1, 2)
    kb3 = r3(kb, bsz, seq)
    hb, c_p, n_p_, m_p = _mlstm(r3(qb, bsz, seq), kb3, jnp.swapaxes(kb3, 1, 2), r3(vb, bsz, seq),
                                r3(og, bsz, seq), sm3, gr, gh,
                                jnp.zeros((bsz, N_B, HD_B, HD_B), F32), jnp.zeros((bsz, N_B, HD_B), F32),
                                jnp.zeros((bsz, N_B, LANES), F32), lc=min(SUB, seq))
    x2 = _outproj(x1, attn.reshape(n_p, D_A), hb.reshape(n_p, D_B), wa, wh)
    y_prompt = _ffn(x2, *ffn2).reshape(bsz, seq, D_MODEL)

    n_s = dbs * dseq
    x1s, qs, kfs, k16s, vfs, v16s, qis, qbs, kbs, vbs, ogs, sms = _mixer_front(
        x_sample.reshape(n_s, D_MODEL), ffn1, inp)
    sms3 = r3(sms, dbs, dseq)
    l_all = past + dseq
    tk_s = 3 * SUB
    sk = -(-l_all // tk_s) * tk_s
    padk = lambda a: jnp.pad(a, ((0, 0), (0, sk - l_all), (0, 0)))
    k_all = padk(jnp.concatenate([cache_k.reshape(dbs, past, D_A).astype(BF16), r3(k16s, dbs, dseq)], axis=1))
    v_all = padk(jnp.concatenate([cache_v.reshape(dbs, past, D_A).astype(BF16), r3(v16s, dbs, dseq)], axis=1))
    ki_all = padk(jnp.concatenate([cache_k_idx.astype(BF16), sms3[:, :, :D_IDX].astype(BF16)], axis=1))
    attn_s = _dsa(r3(qs, dbs, dseq), r3(qis, dbs, dseq), sms3[:, :, SM_WI:SM_IG], jnp.swapaxes(ki_all, 1, 2),
                  k_all, v_all, bias,
                  tq=dseq, tk=tk_s, q_off=past, l_valid=l_all, topk=min(TOPK_MAX, l_all // 4))
    lc_s = LANES
    padt = lambda a: jnp.pad(a, ((0, 0), (0, lc_s - dseq), (0, 0)))
    lane = jnp.arange(LANES)
    sm_pad = jnp.where((lane >= SM_IG) & (lane < SM_LF), NEG, 0.0).astype(F32)
    sms_p = jnp.concatenate([sms3, jnp.broadcast_to(sm_pad, (dbs, lc_s - dseq, LANES))], axis=1)
    kbs3 = padt(r3(kbs, dbs, dseq))
    hbs, c_s, n_s_, m_s = _mlstm(padt(r3(qbs, dbs, dseq)), kbs3, jnp.swapaxes(kbs3, 1, 2), padt(r3(vbs, dbs, dseq)),
                                 padt(r3(ogs, dbs, dseq)), sms_p, jnp.swapaxes(sms_p[:, :, SM_IG:SM_END], 1, 2), gh,
                                 state_C.astype(F32), state_n.astype(F32),
                                 jnp.broadcast_to(state_m.astype(F32)[:, :, None], (dbs, N_B, LANES)), lc=lc_s)
    x2s = _outproj(x1s, attn_s.reshape(n_s, D_A), hbs[:, :dseq].reshape(n_s, D_B), wa, wh)
    y_sample = _ffn(x2s, *ffn2).reshape(dbs, dseq, D_MODEL)

    sd = state_C.dtype
    return (y_prompt, y_sample,
            kf.reshape(bsz, seq, N_A, HD_A), vf.reshape(bsz, seq, N_A, HD_A), sm3[:, :, :D_IDX],
            c_p.astype(sd), n_p_.astype(sd), m_p[:, :, 0].astype(sd),
            kfs.reshape(dbs, dseq, N_A, HD_A), vfs.reshape(dbs, dseq, N_A, HD_A), sms3[:, :, :D_IDX],
            c_s.astype(sd), n_s_.astype(sd), m_s[:, :, 0].astype(sd))
```

```python
import functools
import math

import numpy as np
import jax
import jax.numpy as jnp
from jax import lax
from jax.experimental import pallas as pl
from jax.experimental.pallas import tpu as pltpu

F32 = jnp.float32
BF16 = jnp.bfloat16
I32 = jnp.int32

D_MODEL = 2048
CHUNK = 64
N_A, HD_A = 8, 128
H_IDX, D_IDX = 16, 64
TOPK_MAX = 256
N_B, HD_B = 4, 256
D_FF = 5504
NUM_BUCKETS, MAX_DISTANCE = 32, 128
EPS = 1e-6
NEG = -1e30
INT_MIN = -(2 ** 31)
LOG2E = math.log2(math.e)

LANES = 128
D_A = N_A * HD_A
D_B = N_B * HD_B
D_QI = H_IDX * D_IDX
FF_TILE = 512
D_FF_PAD = -(-D_FF // FF_TILE) * FF_TILE
TOK_TILE = 512
SUB = 256
VMEM_LIMIT = 56 * 1024 * 1024

SM_KI, SM_WI, SM_IG, SM_LF = 0, D_IDX, D_IDX + H_IDX, D_IDX + H_IDX + N_B
SM_END = SM_LF + N_B


def _cparams(sem):
    return pltpu.CompilerParams(dimension_semantics=sem, vmem_limit_bytes=VMEM_LIMIT)


def _rms(x, g):
    ms = jnp.mean(x * x, axis=-1, keepdims=True)
    return x * lax.rsqrt(ms + EPS) * g


def _ffn_body(x_ref, g_ref, w1_ref, w3_ref, w2_ref, o_ref, hn_ref):
    j = pl.program_id(1)

    @pl.when(j == 0)
    def _():
        hn_ref[...] = _rms(x_ref[...], g_ref[...]).astype(BF16)

    h = hn_ref[...]
    a = jnp.dot(h, w1_ref[...], preferred_element_type=F32)
    b = jnp.dot(h, w3_ref[...], preferred_element_type=F32)
    u = (a * jax.nn.sigmoid(a) * b).astype(BF16)
    part = jnp.dot(u, w2_ref[...], preferred_element_type=F32)

    @pl.when(j == 0)
    def _():
        o_ref[...] = part

    @pl.when(j > 0)
    def _():
        o_ref[...] += part

    @pl.when(j == pl.num_programs(1) - 1)
    def _():
        o_ref[...] = x_ref[...] + 0.5 * o_ref[...]


def _ffn(x, g, w1, w3, w2):
    n = x.shape[0]
    tm = min(TOK_TILE, n)
    grid = (n // tm, D_FF_PAD // FF_TILE)
    return pl.pallas_call(
        _ffn_body,
        grid=grid,
        in_specs=[
            pl.BlockSpec((tm, D_MODEL), lambda i, j: (i, 0)),
            pl.BlockSpec((1, D_MODEL), lambda i, j: (0, 0)),
            pl.BlockSpec((D_MODEL, FF_TILE), lambda i, j: (0, j)),
            pl.BlockSpec((D_MODEL, FF_TILE), lambda i, j: (0, j)),
            pl.BlockSpec((FF_TILE, D_MODEL), lambda i, j: (j, 0)),
        ],
        out_specs=pl.BlockSpec((tm, D_MODEL), lambda i, j: (i, 0)),
        out_shape=jax.ShapeDtypeStruct((n, D_MODEL), F32),
        scratch_shapes=[pltpu.VMEM((tm, D_MODEL), BF16)],
        compiler_params=_cparams(("parallel", "arbitrary")),
    )(x, g, w1, w3, w2)


def _prep_ffn(g, w1, w3, w2):
    pad = D_FF_PAD - D_FF
    return (g.reshape(1, D_MODEL),
            jnp.pad(w1.astype(BF16), ((0, 0), (0, pad))),
            jnp.pad(w3.astype(BF16), ((0, 0), (0, pad))),
            jnp.pad(w2.astype(BF16), ((0, pad), (0, 0))))


N_MAIN_GROUPS = 8


def _head_norm(p, g, heads, hd):
    outs = []
    for h in range(heads):
        outs.append(_rms(p[:, h * hd:(h + 1) * hd], g))
    return outs


def _inproj_body(x_ref, g_ref, wm_ref, wsh_ref, wsl_ref, gq_ref, gk_ref, sg_ref, sb_ref,
                 q_ref, kf_ref, k16_ref, vf_ref, v16_ref, qi_ref, qb_ref, kb_ref, vb_ref, og_ref, sm_ref,
                 hn_ref):
    j = pl.program_id(1)

    @pl.when(j == 0)
    def _():
        h32 = _rms(x_ref[...], g_ref[...])
        h_hi = h32.astype(BF16)
        hn_ref[...] = h_hi
        h_lo = (h32 - h_hi.astype(F32)).astype(BF16)
        ps = (jnp.dot(h_hi, wsh_ref[...], preferred_element_type=F32)
              + jnp.dot(h_lo, wsh_ref[...], preferred_element_type=F32)
              + jnp.dot(h_hi, wsl_ref[...], preferred_element_type=F32))
        lane = lax.broadcasted_iota(I32, ps.shape, 1)
        ms = jnp.sum(jnp.where(lane < SM_WI, ps * ps, 0.0), axis=-1, keepdims=True) * (1.0 / D_IDX)
        kin = ps * lax.rsqrt(ms + EPS) * sg_ref[...]
        z = ps + sb_ref[...]
        ls = jnp.minimum(z, 0.0) - jnp.log1p(jnp.exp(-jnp.abs(z)))
        sm_ref[...] = jnp.where(lane < SM_WI, kin,
                                jnp.where(lane < SM_IG, ps,
                                          jnp.where(lane < SM_LF, z,
                                                    jnp.where(lane < SM_END, ls, 0.0))))

    p = jnp.dot(hn_ref[...], wm_ref[...], preferred_element_type=F32)

    @pl.when(j == 0)
    def _():
        for h, qh in enumerate(_head_norm(p, gq_ref[...], N_A, HD_A)):
            q_ref[:, h * HD_A:(h + 1) * HD_A] = (qh * (HD_A ** -0.5 * LOG2E)).astype(BF16)

    @pl.when(j == 1)
    def _():
        for h, kh in enumerate(_head_norm(p, gk_ref[...], N_A, HD_A)):
            kf_ref[:, h * HD_A:(h + 1) * HD_A] = kh
            k16_ref[:, h * HD_A:(h + 1) * HD_A] = kh.astype(BF16)

    @pl.when(j == 2)
    def _():
        vf_ref[...] = p
        v16_ref[...] = p.astype(BF16)

    @pl.when(j == 3)
    def _():
        qi_ref[...] = p.astype(BF16)

    @pl.when(j == 4)
    def _():
        qb_ref[...] = p.astype(BF16)

    @pl.when(j == 5)
    def _():
        kb_ref[...] = (p * HD_B ** -0.5).astype(BF16)

    @pl.when(j == 6)
    def _():
        vb_ref[...] = p.astype(BF16)

    @pl.when(j == 7)
    def _():
        og_ref[...] = jax.nn.sigmoid(p).astype(BF16)


def _inproj(x, g, wm, wsh, wsl, gq, gk, sg, sb):
    n = x.shape[0]
    tm = min(TOK_TILE, n)
    grid = (n // tm, N_MAIN_GROUPS)
    tok = lambda w: pl.BlockSpec((tm, w), lambda i, j: (i, 0))
    const = lambda a: pl.BlockSpec(a.shape, lambda i, j: (0,) * a.ndim)
    wide = lambda dt: jax.ShapeDtypeStruct((n, D_A), dt)
    return pl.pallas_call(
        _inproj_body,
        grid=grid,
        in_specs=[tok(D_MODEL), const(g),
                  pl.BlockSpec((D_MODEL, D_A), lambda i, j: (0, j)),
                  const(wsh), const(wsl), const(gq), const(gk), const(sg), const(sb)],
        out_specs=[tok(D_A)] * 10 + [tok(LANES)],
        out_shape=[wide(BF16), wide(F32), wide(BF16), wide(F32), wide(BF16), wide(BF16),
                   wide(BF16), wide(BF16), wide(BF16), wide(BF16),
                   jax.ShapeDtypeStruct((n, LANES), F32)],
        scratch_shapes=[pltpu.VMEM((tm, D_MODEL), BF16)],
        compiler_params=_cparams(("parallel", "arbitrary")),
    )(x, g, wm, wsh, wsl, gq, gk, sg, sb)


def _prep_inproj(g_mix, w_in, g_q, g_k, g_kidx, b_i, b_f):
    o_ki = 4 * D_A
    o_qb = o_ki + D_IDX + H_IDX
    o_ib = o_qb + 4 * D_B
    wm = jnp.concatenate([w_in[:, :o_ki], w_in[:, o_qb:o_ib]], axis=1).astype(BF16)
    ws = jnp.concatenate([w_in[:, o_ki:o_qb], w_in[:, o_ib:],
                          jnp.zeros((D_MODEL, LANES - SM_END), F32)], axis=1)
    wsh = ws.astype(BF16)
    wsl = (ws - wsh.astype(F32)).astype(BF16)
    sg = jnp.concatenate([g_kidx, jnp.ones((LANES - D_IDX,), F32)]).reshape(1, LANES)
    sb = jnp.concatenate([jnp.zeros((SM_IG,), F32), b_i, b_f,
                          jnp.zeros((LANES - SM_END,), F32)]).reshape(1, LANES)
    return (g_mix.reshape(1, D_MODEL), wm, wsh, wsl,
            g_q.reshape(1, HD_A), g_k.reshape(1, HD_A), sg, sb)


def _outproj_body(x_ref, a_ref, h_ref, wa_ref, wh_ref, o_ref):
    o_ref[...] = (x_ref[...]
                  + jnp.dot(a_ref[...], wa_ref[...], preferred_element_type=F32)
                  + jnp.dot(h_ref[...], wh_ref[...], preferred_element_type=F32))


def _outproj(x, a, h, wa, wh):
    n = x.shape[0]
    tm = min(TOK_TILE, n)
    return pl.pallas_call(
        _outproj_body,
        grid=(n // tm,),
        in_specs=[pl.BlockSpec((tm, D_MODEL), lambda i: (i, 0)),
                  pl.BlockSpec((tm, D_A), lambda i: (i, 0)),
                  pl.BlockSpec((tm, D_B), lambda i: (i, 0)),
                  pl.BlockSpec((D_A, D_MODEL), lambda i: (0, 0)),
                  pl.BlockSpec((D_B, D_MODEL), lambda i: (0, 0))],
        out_specs=pl.BlockSpec((tm, D_MODEL), lambda i: (i, 0)),
        out_shape=jax.ShapeDtypeStruct((n, D_MODEL), F32),
        compiler_params=_cparams(("parallel",)),
    )(x, a, h, wa, wh)


def _bucket_thresholds():
    nb = NUM_BUCKETS // 2
    max_exact = nb // 2
    span = nb - max_exact
    ratio = MAX_DISTANCE // max_exact
    out = []
    for m in range(1, span):
        n = max_exact
        while n ** span < max_exact ** span * ratio ** m:
            n += 1
        out.append(n)
    return tuple(out)


def _bias_body(tbl_ref, o_ref):
    o = pl.program_id(0)
    h = pl.program_id(1)
    nb = NUM_BUCKETS // 2
    max_exact = nb // 2
    row = lax.broadcasted_iota(I32, (SUB, SUB), 0)
    col = lax.broadcasted_iota(I32, (SUB, SUB), 1)
    rel = row - o * SUB - col
    n = jnp.abs(rel)
    large = jnp.full((SUB, SUB), max_exact, I32)
    for t in _bucket_thresholds():
        large = large + (n >= t).astype(I32)
    bucket = jnp.where(rel > 0, nb, 0) + jnp.where(n < max_exact, n, large)
    val = jnp.zeros((SUB, SUB), F32)
    for bk in range(NUM_BUCKETS):
        val = jnp.where(bucket == bk, tbl_ref[bk, h], val)
    o_ref[0, 0] = (val - tbl_ref[nb - 1, h]) * LOG2E


def _bias_tiles(rel_bias):
    return pl.pallas_call(
        _bias_body,
        grid=(2, N_A),
        in_specs=[pl.BlockSpec(memory_space=pltpu.SMEM)],
        out_specs=pl.BlockSpec((1, 1, SUB, SUB), lambda o, h: (o, h, 0, 0)),
        out_shape=jax.ShapeDtypeStruct((2, N_A, SUB, SUB), F32),
    )(rel_bias)


def _dsa_body(ti_ref, tk_ref, tn_ref, tl_ref,
              qt_ref, qit_ref, wit_ref, ki_ref, k_ref, vt_ref, bias_ref, o_ref,
              key_ref, gmax_ref, thr_ref, m_ref, l_ref, acc_ref, lg_ref, al_ref,
              *, tq, tk, q_off, l_valid, topk, q_valid):
    s = pl.program_id(1)
    i = ti_ref[s]
    kt = tk_ref[s]
    nsub = tn_ref[s]
    subs = tk // SUB
    q_pos0 = q_off + i * tq
    i_sub = q_off // SUB + (i * tq) // SUB

    def fold(x, op, rows=8):
        parts = [x[r:r + rows] for r in range(0, x.shape[0], rows)]
        while len(parts) > 1:
            parts = [op(parts[a], parts[a + 1]) for a in range(0, len(parts), 2)]
        return parts[0]

    @pl.when(kt == 0)
    def _():
        m_ref[...] = jnp.full(m_ref.shape, NEG, F32)
        l_ref[...] = jnp.zeros(l_ref.shape, F32)
        acc_ref[...] = jnp.zeros(acc_ref.shape, F32)
        gmax_ref[...] = jnp.full(gmax_ref.shape, INT_MIN, I32)

        qchunk = (q_pos0 + lax.broadcasted_iota(I32, (SUB, tq), 1)) >> 6
        krow = lax.broadcasted_iota(I32, (SUB, tq), 0)

        def score_tile(t, c):
            off = pl.multiple_of(t * SUB, SUB)
            kit = ki_ref[0, pl.ds(off, SUB), :]
            sc = jnp.zeros((SUB, tq), F32)
            for j in range(H_IDX):
                sj = jnp.dot(kit, qit_ref[0, j * D_IDX:(j + 1) * D_IDX, :], preferred_element_type=F32)
                sc = sc + jnp.maximum(sj, 0.0) * wit_ref[0, j:j + 1, :]
            kpos = off + krow
            adm = ((kpos >> 6) <= qchunk) & (kpos < l_valid)
            bits = lax.bitcast_convert_type(sc, I32)
            okey = bits ^ ((bits >> 31) & 0x7FFFFFFF)
            okey = jnp.where(adm, okey, INT_MIN)
            key_ref[pl.ds(off, SUB), :] = okey
            gmax_ref[...] = jnp.maximum(gmax_ref[...], okey)
            return c

        lax.fori_loop(0, nsub, score_tile, 0)

        gm = gmax_ref[...]
        lo = jnp.min(fold(gm, jnp.minimum), axis=0, keepdims=True)
        hi = jnp.max(fold(gm, jnp.maximum), axis=0, keepdims=True)
        diff = lo ^ hi
        nbits = jnp.where(diff < 0, 32,
                          (lax.bitcast_convert_type(diff.astype(F32), I32) >> 23) - 126)
        top = jnp.clip(jnp.max(nbits), 1, 32)
        keep = jnp.where(top >= 32, 0, -jnp.left_shift(jnp.int32(1), jnp.minimum(top, 31)))
        thr0 = jnp.where(top >= 32, INT_MIN, hi & keep)
        lane = lax.broadcasted_iota(I32, (1, tq), 1)
        cnt0 = jnp.where(lane < q_valid, -1.0, float(topk))

        def bit_cond(st):
            b, _, cnt_at = st
            return (b >= 0) & (jnp.min(jnp.where(cnt_at == float(topk), 1, 0)) == 0)

        def bit_step(st):
            b, thr, cnt_at = st
            cand = thr + jnp.left_shift(jnp.int32(1), b)

            def count_tile(t, cnt):
                off = pl.multiple_of(t * SUB, SUB)
                ge = jnp.where(key_ref[pl.ds(off, SUB), :] >= cand, 1.0, 0.0)
                return cnt + fold(ge, jnp.add, rows=32)

            cnt = lax.fori_loop(0, nsub, count_tile, jnp.zeros((32, tq), F32))
            tot = jnp.sum(cnt, axis=0, keepdims=True)
            take = tot >= float(topk)
            return b - 1, jnp.where(take, cand, thr), jnp.where(take, tot, cnt_at)

        _, thr, _ = lax.while_loop(bit_cond, bit_step, (top - 1, thr0, cnt0))
        thr_ref[...] = jnp.broadcast_to(jnp.maximum(thr, INT_MIN + 1), (8, tq))

    n_here = jnp.clip(nsub - kt * subs, 0, subs)

    def sub_tile(u, c):
        g = kt * subs + u
        dsub = g - i_sub
        koff = pl.multiple_of(g * SUB, SUB)
        uoff = pl.multiple_of(u * SUB, SUB)
        mask = key_ref[pl.ds(koff, SUB), :] >= thr_ref[0:1, :]

        def heads(near):
            for h in range(N_A):
                hs = slice(h * HD_A, (h + 1) * HD_A)
                kh = k_ref[0, pl.ds(uoff, SUB), hs]
                lg = jnp.dot(kh, qt_ref[0, hs, :], preferred_element_type=F32)
                if near:
                    lg = lg + bias_ref[-dsub, h, :, 0:tq]
                lg = jnp.where(mask, lg, 2.0 * NEG)
                lg_ref[h] = lg
                m_old = m_ref[h]
                m_new = jnp.maximum(m_old, jnp.max(fold(lg, jnp.maximum), axis=0, keepdims=True))
                al_ref[h] = jnp.exp2(m_old - m_new)
                m_ref[h] = m_new
            for h in range(N_A):
                hs = slice(h * HD_A, (h + 1) * HD_A)
                vth = vt_ref[0, hs, pl.ds(uoff, SUB)]
                alpha = al_ref[h]
                p = jnp.exp2(lg_ref[h] - m_ref[h][0:1, :])
                l_ref[h] = alpha * l_ref[h] + jnp.sum(fold(p, jnp.add), axis=0, keepdims=True)
                acc_ref[hs, :] = alpha[0:1, :] * acc_ref[hs, :] + jnp.dot(vth, p.astype(BF16),
                                                                          preferred_element_type=F32)

        @pl.when(dsub >= -1)
        def _():
            heads(True)

        @pl.when(dsub < -1)
        def _():
            heads(False)

        return c

    lax.fori_loop(0, n_here, sub_tile, 0)

    @pl.when(tl_ref[s] == 1)
    def _():
        for h in range(N_A):
            hs = slice(h * HD_A, (h + 1) * HD_A)
            o_ref[0, hs, :] = (acc_ref[hs, :] / l_ref[h][0:1, :]).astype(BF16)


def _dsa(qt, qit, wit, ki, k, vt, bias, *, tq, tk, q_off, l_valid, topk, q_valid):
    bsz, sq = qt.shape[0], qt.shape[2]
    sk = k.shape[1]
    assert sq % tq == 0 and sk % tk == 0 and tk % SUB == 0 and q_off % SUB == 0
    assert (tq % SUB == 0 or sq == tq) and topk <= SUB and tq % LANES == 0
    subs = tk // SUB
    ti, tkk, tn, tl = [], [], [], []
    for i in range(sq // tq):
        q_last = q_off + (i + 1) * tq - 1
        lim = min(l_valid, (q_last // CHUNK + 1) * CHUNK)
        nsub = -(-lim // SUB)
        nkt = -(-nsub // subs)
        for t in range(nkt):
            ti.append(i), tkk.append(t), tn.append(nsub), tl.append(int(t == nkt - 1))
    tabs = [jnp.asarray(np.asarray(a, np.int32)) for a in (ti, tkk, tn, tl)]
    body = functools.partial(_dsa_body, tq=tq, tk=tk, q_off=q_off, l_valid=l_valid, topk=topk, q_valid=q_valid)
    qmap = lambda b, s, ti, tk_, tn, tl: (b, 0, ti[s])
    grid_spec = pltpu.PrefetchScalarGridSpec(
        num_scalar_prefetch=4,
        grid=(bsz, len(ti)),
        in_specs=[pl.BlockSpec((1, D_A, tq), qmap),
                  pl.BlockSpec((1, D_QI, tq), qmap),
                  pl.BlockSpec((1, H_IDX, tq), qmap),
                  pl.BlockSpec((1, sk, D_IDX), lambda b, s, *_: (b, 0, 0)),
                  pl.BlockSpec((1, tk, D_A), lambda b, s, ti, tk_, tn, tl: (b, tk_[s], 0)),
                  pl.BlockSpec((1, D_A, tk), lambda b, s, ti, tk_, tn, tl: (b, 0, tk_[s])),
                  pl.BlockSpec((2, N_A, SUB, SUB), lambda b, s, *_: (0, 0, 0, 0))],
        out_specs=pl.BlockSpec((1, D_A, tq), qmap),
        scratch_shapes=[pltpu.VMEM((sk, tq), I32),
                        pltpu.VMEM((SUB, tq), I32),
                        pltpu.VMEM((8, tq), I32),
                        pltpu.VMEM((N_A, 8, tq), F32),
                        pltpu.VMEM((N_A, 8, tq), F32),
                        pltpu.VMEM((D_A, tq), F32),
                        pltpu.VMEM((N_A, SUB, tq), F32),
                        pltpu.VMEM((N_A, 8, tq), F32)])
    return pl.pallas_call(
        body,
        grid_spec=grid_spec,
        out_shape=jax.ShapeDtypeStruct((bsz, D_A, sq), BF16),
        compiler_params=_cparams(("parallel", "arbitrary")),
    )(*tabs, qt, qit, wit, ki, k, vt, bias)


def _mlstm_body(q_ref, k_ref, kt_ref, v_ref, og_ref, sm_ref, gr_ref, gh_ref, c0_ref, n0_ref, m0_ref,
                hb_ref, co_ref, no_ref, mo_ref, c_s, n_s, m_s, *, lc):
    c = pl.program_id(1)

    @pl.when(c == 0)
    def _():
        c_s[...] = c0_ref[0]
        for h in range(N_B):
            n_s[h] = jnp.broadcast_to(n0_ref[0, h:h + 1, :], (8, HD_B))
            m_s[h] = jnp.broadcast_to(m0_ref[0, h:h + 1, :], (8, LANES))

    row = lax.broadcasted_iota(I32, (lc, lc), 0)
    col = lax.broadcasted_iota(I32, (lc, lc), 1)
    causal = col <= row
    sm = sm_ref[0]
    gr = gr_ref[0]
    hp = lax.Precision.HIGHEST
    b_cols = jnp.dot(causal.astype(F32), sm, precision=hp, preferred_element_type=F32)
    b_rows = jnp.dot(gr, (row <= col).astype(F32), precision=hp, preferred_element_type=F32)

    for h in range(N_B):
        hs = slice(h * HD_B, (h + 1) * HD_B)
        m_prev = m_s[h][0:1, 0:1]
        bc = b_cols[:, SM_LF + h:SM_LF + h + 1]
        ig_c = sm[:, SM_IG + h:SM_IG + h + 1]
        br = b_rows[N_B + h:N_B + h + 1, :]
        ig_r = gr[h:h + 1, :]
        log_d = jnp.where(causal, bc - br + ig_r, NEG)
        log_inter = bc + m_prev
        m_t = jnp.maximum(log_inter, jnp.max(log_d, axis=1, keepdims=True))
        d = jnp.exp(log_d - m_t)
        inter = jnp.exp(log_inter - m_t)
        qh = q_ref[0, :, hs]
        kth = kt_ref[0, hs, :]
        vh = v_ref[0, :, hs]
        sc = jnp.dot(qh, kth, preferred_element_type=F32) * d
        ch = c_s[h]
        nrow = n_s[h][0:1, :]
        num = (jnp.dot(sc.astype(BF16), vh, preferred_element_type=F32)
               + inter * jnp.dot(qh, ch.astype(BF16), preferred_element_type=F32))
        den = (jnp.sum(sc, axis=1, keepdims=True)
               + inter * jnp.sum(qh.astype(F32) * nrow, axis=1, keepdims=True))
        hh = num / jnp.maximum(jnp.abs(den), jnp.exp(-m_t))
        hb = _rms(hh, gh_ref[...]) * og_ref[0, :, hs].astype(F32)
        hb_ref[0, :, hs] = hb.astype(BF16)

        m_new = m_t[lc - 1:lc, :]
        b_last = bc[lc - 1:lc, :]
        decay = jnp.exp(b_last + m_prev - m_new)
        w_r = jnp.exp(b_last - br + ig_r - m_new)
        w_c = jnp.exp(b_last - bc + ig_c - m_new)
        kw = (kth.astype(F32) * w_r).astype(BF16)
        c_s[h] = decay * ch + jnp.dot(kw, vh, preferred_element_type=F32)
        n_new = decay * nrow + jnp.sum(k_ref[0, :, hs].astype(F32) * w_c, axis=0, keepdims=True)
        n_s[h] = jnp.broadcast_to(n_new, (8, HD_B))
        m_s[h] = jnp.broadcast_to(m_new, (8, LANES))

    @pl.when(c == pl.num_programs(1) - 1)
    def _():
        co_ref[0] = c_s[...]
        for h in range(N_B):
            no_ref[0, h:h + 1, :] = n_s[h][0:1, :]
            mo_ref[0, h:h + 1, :] = m_s[h][0:1, :]


def _mlstm(q, k, kt, v, og, sm, gr, g_h, c0, n0, m0, *, lc):
    bsz, s = q.shape[0], q.shape[1]
    tokb = lambda w: pl.BlockSpec((1, lc, w), lambda b, c: (b, c, 0))
    st3 = lambda w: pl.BlockSpec((1, N_B, w), lambda b, c: (b, 0, 0))
    st4 = pl.BlockSpec((1, N_B, HD_B, HD_B), lambda b, c: (b, 0, 0, 0))
    return pl.pallas_call(
        functools.partial(_mlstm_body, lc=lc),
        grid=(bsz, s // lc),
        in_specs=[tokb(D_B), tokb(D_B),
                  pl.BlockSpec((1, D_B, lc), lambda b, c: (b, 0, c)),
                  tokb(D_B), tokb(D_B), tokb(LANES),
                  pl.BlockSpec((1, 2 * N_B, lc), lambda b, c: (b, 0, c)),
                  pl.BlockSpec((1, HD_B), lambda b, c: (0, 0)),
                  st4, st3(HD_B), st3(LANES)],
        out_specs=[tokb(D_B), st4, st3(HD_B), st3(LANES)],
        out_shape=[jax.ShapeDtypeStruct((bsz, s, D_B), BF16),
                   jax.ShapeDtypeStruct((bsz, N_B, HD_B, HD_B), F32),
                   jax.ShapeDtypeStruct((bsz, N_B, HD_B), F32),
                   jax.ShapeDtypeStruct((bsz, N_B, LANES), F32)],
        scratch_shapes=[pltpu.VMEM((N_B, HD_B, HD_B), F32),
                        pltpu.VMEM((N_B, 8, HD_B), F32),
                        pltpu.VMEM((N_B, 8, LANES), F32)],
        compiler_params=_cparams(("parallel", "arbitrary")),
    )(q, k, kt, v, og, sm, gr, g_h, c0, n0, m0)


def _mixer_front(x, ffn1, inp):
    x1 = _ffn(x, *ffn1)
    return (x1,) + tuple(_inproj(x1, *inp))


def kernel(x_prompt, x_sample, cache_k, cache_v, cache_k_idx, state_C, state_n, state_m, g_ffn1, w1_ffn1, w3_ffn1, w2_ffn1, g_mix, w_in, g_q, g_k, g_kidx, rel_bias, b_i, b_f, g_h, w_out, g_ffn2, w1_ffn2, w3_ffn2, w2_ffn2):
    bsz, seq = x_prompt.shape[0], x_prompt.shape[1]
    dbs, dseq = x_sample.shape[0], x_sample.shape[1]
    past = cache_k.shape[1]

    ffn1 = _prep_ffn(g_ffn1, w1_ffn1, w3_ffn1, w2_ffn1)
    ffn2 = _prep_ffn(g_ffn2, w1_ffn2, w3_ffn2, w2_ffn2)
    inp = _prep_inproj(g_mix, w_in, g_q, g_k, g_kidx, b_i, b_f)
    wa = w_out[:D_A].astype(BF16)
    wh = w_out[D_A:].astype(BF16)
    gh = g_h.reshape(1, HD_B)
    bias = _bias_tiles(rel_bias)

    n_p = bsz * seq
    x1, q, kf, k16, vf, v16, qi, qb, kb, vb, og, sm = _mixer_front(x_prompt.reshape(n_p, D_MODEL), ffn1, inp)
    r3 = lambda a, b_, s_: a.reshape(b_, s_, a.shape[-1])
    sm3 = r3(sm, bsz, seq)
    tr = lambda a: jnp.swapaxes(a, 1, 2)
    tq_p = min(SUB, seq)
    attn = tr(_dsa(tr(r3(q, bsz, seq)), tr(r3(qi, bsz, seq)), tr(sm3[:, :, SM_WI:SM_IG]),
                   sm3[:, :, :D_IDX].astype(BF16), r3(k16, bsz, seq), tr(r3(v16, bsz, seq)), bias,
                   tq=tq_p, tk=min(1024, seq), q_off=0, l_valid=seq, topk=min(TOPK_MAX, seq // 4),
                   q_valid=tq_p))
    gr = jnp.swapaxes(sm3[:, :, SM_IG:SM_END], 1, 2)
    kb3 = r3(kb, bsz, seq)
    hb, c_p, n_p_, m_p = _mlstm(r3(qb, bsz, seq), kb3, jnp.swapaxes(kb3, 1, 2), r3(vb, bsz, seq),
                                r3(og, bsz, seq), sm3, gr, gh,
                                jnp.zeros((bsz, N_B, HD_B, HD_B), F32), jnp.zeros((bsz, N_B, HD_B), F32),
                                jnp.zeros((bsz, N_B, LANES), F32), lc=min(SUB, seq))
    x2 = _outproj(x1, attn.reshape(n_p, D_A), hb.reshape(n_p, D_B), wa, wh)
    y_prompt = _ffn(x2, *ffn2).reshape(bsz, seq, D_MODEL)

    n_s = dbs * dseq
    x1s, qs, kfs, k16s, vfs, v16s, qis, qbs, kbs, vbs, ogs, sms = _mixer_front(
        x_sample.reshape(n_s, D_MODEL), ffn1, inp)
    sms3 = r3(sms, dbs, dseq)
    l_all = past + dseq
    tk_s = 3 * SUB
    sk = -(-l_all // tk_s) * tk_s
    padk = lambda a: jnp.pad(a, ((0, 0), (0, sk - l_all), (0, 0)))
    k_all = padk(jnp.concatenate([cache_k.reshape(dbs, past, D_A).astype(BF16), r3(k16s, dbs, dseq)], axis=1))
    v_all = padk(jnp.concatenate([cache_v.reshape(dbs, past, D_A).astype(BF16), r3(v16s, dbs, dseq)], axis=1))
    ki_all = padk(jnp.concatenate([cache_k_idx.astype(BF16), sms3[:, :, :D_IDX].astype(BF16)], axis=1))
    assert dseq <= LANES
    padq = lambda a: jnp.pad(tr(a), ((0, 0), (0, 0), (0, LANES - dseq)))
    attn_s = tr(_dsa(padq(r3(qs, dbs, dseq)), padq(r3(qis, dbs, dseq)), padq(sms3[:, :, SM_WI:SM_IG]),
                     ki_all, k_all, tr(v_all), bias,
                     tq=LANES, tk=tk_s, q_off=past, l_valid=l_all, topk=min(TOPK_MAX, l_all // 4),
                     q_valid=dseq)[:, :, :dseq])
    lc_s = LANES
    padt = lambda a: jnp.pad(a, ((0, 0), (0, lc_s - dseq), (0, 0)))
    lane = jnp.arange(LANES)
    sm_pad = jnp.where((lane >= SM_IG) & (lane < SM_LF), NEG, 0.0).astype(F32)
    sms_p = jnp.concatenate([sms3, jnp.broadcast_to(sm_pad, (dbs, lc_s - dseq, LANES))], axis=1)
    kbs3 = padt(r3(kbs, dbs, dseq))
    hbs, c_s, n_s_, m_s = _mlstm(padt(r3(qbs, dbs, dseq)), kbs3, jnp.swapaxes(kbs3, 1, 2), padt(r3(vbs, dbs, dseq)),
                                 padt(r3(ogs, dbs, dseq)), sms_p, jnp.swapaxes(sms_p[:, :, SM_IG:SM_END], 1, 2), gh,
                                 state_C.astype(F32), state_n.astype(F32),
                                 jnp.broadcast_to(state_m.astype(F32)[:, :, None], (dbs, N_B, LANES)), lc=lc_s)
    x2s = _outproj(x1s, attn_s.reshape(n_s, D_A), hbs[:, :dseq].reshape(n_s, D_B), wa, wh)
    y_sample = _ffn(x2s, *ffn2).reshape(dbs, dseq, D_MODEL)

    sd = state_C.dtype
    return (y_prompt, y_sample,
            kf.reshape(bsz, seq, N_A, HD_A), vf.reshape(bsz, seq, N_A, HD_A), sm3[:, :, :D_IDX],
            c_p.astype(sd), n_p_.astype(sd), m_p[:, :, 0].astype(sd),
            kfs.reshape(dbs, dseq, N_A, HD_A), vfs.reshape(dbs, dseq, N_A, HD_A), sms3[:, :, :D_IDX],
            c_s.astype(sd), n_s_.astype(sd), m_s[:, :, 0].astype(sd))
```

```python
import functools
import math

import numpy as np
import jax
import jax.numpy as jnp
from jax import lax
from jax.experimental import pallas as pl
from jax.experimental.pallas import tpu as pltpu

F32 = jnp.float32
BF16 = jnp.bfloat16
I32 = jnp.int32

D_MODEL = 2048
CHUNK = 64
N_A, HD_A = 8, 128
H_IDX, D_IDX = 16, 64
TOPK_MAX = 256
N_B, HD_B = 4, 256
D_FF = 5504
NUM_BUCKETS, MAX_DISTANCE = 32, 128
EPS = 1e-6
NEG = -1e30
INT_MIN = -(2 ** 31)
LOG2E = math.log2(math.e)

LANES = 128
D_A = N_A * HD_A
D_B = N_B * HD_B
D_QI = H_IDX * D_IDX
FF_TILE = 512
D_FF_PAD = -(-D_FF // FF_TILE) * FF_TILE
FFN_TOK_TILE = 512
PROJ_TOK_TILE = 256
TOK_TILE = 512
SUB = 256
VMEM_LIMIT = 56 * 1024 * 1024

SM_KI, SM_WI, SM_IG, SM_LF = 0, D_IDX, D_IDX + H_IDX, D_IDX + H_IDX + N_B
SM_END = SM_LF + N_B


def _cparams(sem):
    return pltpu.CompilerParams(dimension_semantics=sem, vmem_limit_bytes=VMEM_LIMIT)


def _rms(x, g):
    ms = jnp.mean(x * x, axis=-1, keepdims=True)
    return x * lax.rsqrt(ms + EPS) * g


def _ffn_body(x_ref, g_ref, w1_ref, w3_ref, w2_ref, o_ref, hn_ref):
    j = pl.program_id(1)

    @pl.when(j == 0)
    def _():
        hn_ref[...] = _rms(x_ref[...], g_ref[...]).astype(BF16)
        o_ref[...] = jnp.zeros(o_ref.shape, F32)

    h = hn_ref[...]
    a = jnp.dot(h, w1_ref[...], preferred_element_type=F32)
    b = jnp.dot(h, w3_ref[...], preferred_element_type=F32)
    u = (a * jax.nn.sigmoid(a) * b).astype(BF16)
    o_ref[...] += jnp.dot(u, w2_ref[...], preferred_element_type=F32)

    @pl.when(j == pl.num_programs(1) - 1)
    def _():
        o_ref[...] = x_ref[...] + 0.5 * o_ref[...]


def _ffn(x, g, w1, w3, w2):
    n = x.shape[0]
    tm = min(FFN_TOK_TILE, n)
    grid = (n // tm, D_FF_PAD // FF_TILE)
    return pl.pallas_call(
        _ffn_body,
        grid=grid,
        in_specs=[
            pl.BlockSpec((tm, D_MODEL), lambda i, j: (i, 0)),
            pl.BlockSpec((1, D_MODEL), lambda i, j: (0, 0)),
            pl.BlockSpec((D_MODEL, FF_TILE), lambda i, j: (0, j)),
            pl.BlockSpec((D_MODEL, FF_TILE), lambda i, j: (0, j)),
            pl.BlockSpec((FF_TILE, D_MODEL), lambda i, j: (j, 0)),
        ],
        out_specs=pl.BlockSpec((tm, D_MODEL), lambda i, j: (i, 0)),
        out_shape=jax.ShapeDtypeStruct((n, D_MODEL), F32),
        scratch_shapes=[pltpu.VMEM((tm, D_MODEL), BF16)],
        compiler_params=_cparams(("parallel", "arbitrary")),
    )(x, g, w1, w3, w2)


def _prep_ffn(g, w1, w3, w2):
    pad = D_FF_PAD - D_FF
    return (g.reshape(1, D_MODEL),
            jnp.pad(w1.astype(BF16), ((0, 0), (0, pad))),
            jnp.pad(w3.astype(BF16), ((0, 0), (0, pad))),
            jnp.pad(w2.astype(BF16), ((0, pad), (0, 0))))


N_MAIN_GROUPS = 8


def _head_norm(p, g, heads, hd):
    outs = []
    for h in range(heads):
        outs.append(_rms(p[:, h * hd:(h + 1) * hd], g))
    return outs


def _inproj_a_body(x_ref, g_ref, w_ref, wsh_ref, wsl_ref, gq_ref, gk_ref, sg_ref, sb_ref,
                   q_ref, kf_ref, k16_ref, vf_ref, v16_ref, qi_ref, sm_ref):
    h32 = _rms(x_ref[...], g_ref[...])
    h = h32.astype(BF16)
    col = lambda c: jnp.dot(h, w_ref[:, c * D_A:(c + 1) * D_A], preferred_element_type=F32)

    h_lo = (h32 - h.astype(F32)).astype(BF16)
    ps = (jnp.dot(h, wsh_ref[...], preferred_element_type=F32)
          + jnp.dot(h_lo, wsh_ref[...], preferred_element_type=F32)
          + jnp.dot(h, wsl_ref[...], preferred_element_type=F32))
    lane = lax.broadcasted_iota(I32, ps.shape, 1)
    ms = jnp.sum(jnp.where(lane < SM_WI, ps * ps, 0.0), axis=-1, keepdims=True) * (1.0 / D_IDX)
    kin = ps * lax.rsqrt(ms + EPS) * sg_ref[...]
    z = ps + sb_ref[...]
    ls = jnp.minimum(z, 0.0) - jnp.log1p(jnp.exp(-jnp.abs(z)))
    sm_ref[...] = jnp.where(lane < SM_WI, kin,
                            jnp.where(lane < SM_IG, ps,
                                      jnp.where(lane < SM_LF, z,
                                                jnp.where(lane < SM_END, ls, 0.0))))

    for hd, qh in enumerate(_head_norm(col(0), gq_ref[...], N_A, HD_A)):
        q_ref[:, hd * HD_A:(hd + 1) * HD_A] = (qh * (HD_A ** -0.5 * LOG2E)).astype(BF16)
    for hd, kh in enumerate(_head_norm(col(1), gk_ref[...], N_A, HD_A)):
        kf_ref[:, hd * HD_A:(hd + 1) * HD_A] = kh
        k16_ref[:, hd * HD_A:(hd + 1) * HD_A] = kh.astype(BF16)
    p = col(2)
    vf_ref[...] = p
    v16_ref[...] = p.astype(BF16)
    qi_ref[...] = col(3).astype(BF16)


def _inproj_b_body(x_ref, g_ref, w_ref, qb_ref, kb_ref, vb_ref, og_ref):
    h = _rms(x_ref[...], g_ref[...]).astype(BF16)
    col = lambda c: jnp.dot(h, w_ref[:, c * D_B:(c + 1) * D_B], preferred_element_type=F32)
    qb_ref[...] = col(0).astype(BF16)
    kb_ref[...] = (col(1) * HD_B ** -0.5).astype(BF16)
    vb_ref[...] = col(2).astype(BF16)
    og_ref[...] = jax.nn.sigmoid(col(3)).astype(BF16)


def _inproj(x, g, wa, wb, wsh, wsl, gq, gk, sg, sb):
    n = x.shape[0]
    tm = min(PROJ_TOK_TILE, n)
    tok = lambda w: pl.BlockSpec((tm, w), lambda i: (i, 0))
    const = lambda a: pl.BlockSpec(a.shape, lambda i: (0,) * a.ndim)
    wide = lambda dt: jax.ShapeDtypeStruct((n, D_A), dt)
    outs_a = pl.pallas_call(
        _inproj_a_body,
        grid=(n // tm,),
        in_specs=[tok(D_MODEL)] + [const(a) for a in (g, wa, wsh, wsl, gq, gk, sg, sb)],
        out_specs=[tok(D_A)] * 6 + [tok(LANES)],
        out_shape=[wide(BF16), wide(F32), wide(BF16), wide(F32), wide(BF16), wide(BF16),
                   jax.ShapeDtypeStruct((n, LANES), F32)],
        compiler_params=_cparams(("parallel",)),
    )(x, g, wa, wsh, wsl, gq, gk, sg, sb)
    outs_b = pl.pallas_call(
        _inproj_b_body,
        grid=(n // tm,),
        in_specs=[tok(D_MODEL), const(g), const(wb)],
        out_specs=[tok(D_B)] * 4,
        out_shape=[wide(BF16)] * 4,
        compiler_params=_cparams(("parallel",)),
    )(x, g, wb)
    q, kf, k16, vf, v16, qi, sm = outs_a
    return (q, kf, k16, vf, v16, qi) + tuple(outs_b) + (sm,)


def _prep_inproj(g_mix, w_in, g_q, g_k, g_kidx, b_i, b_f):
    o_ki = 4 * D_A
    o_qb = o_ki + D_IDX + H_IDX
    o_ib = o_qb + 4 * D_B
    wa = w_in[:, :o_ki].astype(BF16)
    wb = w_in[:, o_qb:o_ib].astype(BF16)
    ws = jnp.concatenate([w_in[:, o_ki:o_qb], w_in[:, o_ib:],
                          jnp.zeros((D_MODEL, LANES - SM_END), F32)], axis=1)
    wsh = ws.astype(BF16)
    wsl = (ws - wsh.astype(F32)).astype(BF16)
    sg = jnp.concatenate([g_kidx, jnp.ones((LANES - D_IDX,), F32)]).reshape(1, LANES)
    sb = jnp.concatenate([jnp.zeros((SM_IG,), F32), b_i, b_f,
                          jnp.zeros((LANES - SM_END,), F32)]).reshape(1, LANES)
    return (g_mix.reshape(1, D_MODEL), wa, wb, wsh, wsl,
            g_q.reshape(1, HD_A), g_k.reshape(1, HD_A), sg, sb)


def _outproj_body(x_ref, a_ref, h_ref, wa_ref, wh_ref, o_ref):
    o_ref[...] = (x_ref[...]
                  + jnp.dot(a_ref[...], wa_ref[...], preferred_element_type=F32)
                  + jnp.dot(h_ref[...], wh_ref[...], preferred_element_type=F32))


def _outproj(x, a, h, wa, wh):
    n = x.shape[0]
    tm = min(TOK_TILE, n)
    return pl.pallas_call(
        _outproj_body,
        grid=(n // tm,),
        in_specs=[pl.BlockSpec((tm, D_MODEL), lambda i: (i, 0)),
                  pl.BlockSpec((tm, D_A), lambda i: (i, 0)),
                  pl.BlockSpec((tm, D_B), lambda i: (i, 0)),
                  pl.BlockSpec((D_A, D_MODEL), lambda i: (0, 0)),
                  pl.BlockSpec((D_B, D_MODEL), lambda i: (0, 0))],
        out_specs=pl.BlockSpec((tm, D_MODEL), lambda i: (i, 0)),
        out_shape=jax.ShapeDtypeStruct((n, D_MODEL), F32),
        compiler_params=_cparams(("parallel",)),
    )(x, a, h, wa, wh)


def _bucket_thresholds():
    nb = NUM_BUCKETS // 2
    max_exact = nb // 2
    span = nb - max_exact
    ratio = MAX_DISTANCE // max_exact
    out = []
    for m in range(1, span):
        n = max_exact
        while n ** span < max_exact ** span * ratio ** m:
            n += 1
        out.append(n)
    return tuple(out)


def _bias_body(tbl_ref, o_ref):
    o = pl.program_id(0)
    h = pl.program_id(1)
    nb = NUM_BUCKETS // 2
    max_exact = nb // 2
    row = lax.broadcasted_iota(I32, (SUB, SUB), 0)
    col = lax.broadcasted_iota(I32, (SUB, SUB), 1)
    rel = row - o * SUB - col
    n = jnp.abs(rel)
    large = jnp.full((SUB, SUB), max_exact, I32)
    for t in _bucket_thresholds():
        large = large + (n >= t).astype(I32)
    bucket = jnp.where(rel > 0, nb, 0) + jnp.where(n < max_exact, n, large)
    val = jnp.zeros((SUB, SUB), F32)
    for bk in range(NUM_BUCKETS):
        val = jnp.where(bucket == bk, tbl_ref[bk, h], val)
    o_ref[0, 0] = (val - tbl_ref[nb - 1, h]) * LOG2E


def _bias_tiles(rel_bias):
    return pl.pallas_call(
        _bias_body,
        grid=(2, N_A),
        in_specs=[pl.BlockSpec(memory_space=pltpu.SMEM)],
        out_specs=pl.BlockSpec((1, 1, SUB, SUB), lambda o, h: (o, h, 0, 0)),
        out_shape=jax.ShapeDtypeStruct((2, N_A, SUB, SUB), F32),
    )(rel_bias)


WCH = 4


def _bit(k):
    return INT_MIN if k == 31 else 1 << k


def _bit_transpose32(a):
    a = list(a)
    j, m = 16, 0x0000FFFF
    while j:
        for k in range(32):
            if not k & j:
                t = (lax.shift_right_logical(a[k], jnp.int32(j)) ^ a[k + j]) & m
                a[k + j] = a[k + j] ^ t
                a[k] = a[k] ^ jnp.left_shift(t, jnp.int32(j))
        j >>= 1
        m ^= m << j
    return a


def _dsa_body(ti_ref, tk_ref, tn_ref, tl_ref,
              qt_ref, qit_ref, wit_ref, ki_ref, k_ref, vt_ref, bias_ref, o_ref,
              plane_ref, adm_ref, cand_ref, sel_ref, mb_ref, m_ref, l_ref, acc_ref, lg_ref, al_ref,
              *, tq, tk, q_off, l_valid, topk):
    s = pl.program_id(1)
    i = ti_ref[s]
    kt = tk_ref[s]
    nsub = tn_ref[s]
    subs = tk // SUB
    q_pos0 = q_off + i * tq
    i_sub = q_off // SUB + (i * tq) // SUB

    def fold(x, op, rows=8):
        parts = [x[r:r + rows] for r in range(0, x.shape[0], rows)]
        while len(parts) > 1:
            parts = [op(parts[a], parts[a + 1]) for a in range(0, len(parts), 2)]
        return parts[0]

    @pl.when(kt == 0)
    def _():
        m_ref[...] = jnp.full(m_ref.shape, NEG, F32)
        l_ref[...] = jnp.zeros(l_ref.shape, F32)
        acc_ref[...] = jnp.zeros(acc_ref.shape, F32)

        qchunk = (q_pos0 + lax.broadcasted_iota(I32, (SUB, tq), 1)) >> 6
        krow = lax.broadcasted_iota(I32, (SUB, tq), 0)

        def score_tile(t, c):
            off = pl.multiple_of(t * SUB, SUB)
            kit = ki_ref[0, pl.ds(off, SUB), :]
            sc = jnp.zeros((SUB, tq), F32)
            for j in range(H_IDX):
                sj = jnp.dot(kit, qit_ref[0, j * D_IDX:(j + 1) * D_IDX, :], preferred_element_type=F32)
                sc = sc + jnp.maximum(sj, 0.0) * wit_ref[0, j:j + 1, :]
            kpos = off + krow
            adm = ((kpos >> 6) <= qchunk) & (kpos < l_valid)
            bits = lax.bitcast_convert_type(sc, I32)
            code = bits ^ ((bits >> 31) | INT_MIN)
            planes = _bit_transpose32([code[8 * k:8 * k + 8] for k in range(32)])
            row = pl.multiple_of(t * 8, 8)
            aw = jnp.zeros((8, tq), I32)
            for k in range(32):
                plane_ref[k, pl.ds(row, 8), :] = planes[k]
                aw = aw | jnp.where(adm[8 * k:8 * k + 8], _bit(k), 0)
            adm_ref[pl.ds(row, 8), :] = aw
            return c

        lax.fori_loop(0, nsub, score_tile, 0)

        nchunk = (nsub + WCH - 1) // WCH

        def clear_tile(t, c):
            row = pl.multiple_of(t * 8, 8)
            plane_ref[:, pl.ds(row, 8), :] = jnp.zeros((32, 8, tq), I32)
            adm_ref[pl.ds(row, 8), :] = jnp.zeros((8, tq), I32)
            return c

        lax.fori_loop(nsub, nchunk * WCH, clear_tile, 0)

        def chunk_rows(cn):
            return pl.ds(pl.multiple_of(cn * (8 * WCH), 8 * WCH), 8 * WCH)

        def init_chunk(cn, c):
            cand_ref[chunk_rows(cn), :] = adm_ref[chunk_rows(cn), :]
            sel_ref[chunk_rows(cn), :] = jnp.zeros((8 * WCH, tq), I32)
            return c

        lax.fori_loop(0, nchunk, init_chunk, 0)

        def bit_step(it, n_sel):
            b = 31 - it

            def count_chunk(cn, cnt):
                ones = cand_ref[chunk_rows(cn), :] & plane_ref[b, chunk_rows(cn), :]
                return cnt + lax.population_count(ones)

            cnt = lax.fori_loop(0, nchunk, count_chunk, jnp.zeros((8 * WCH, tq), I32))
            n_one = jnp.sum(fold(cnt.astype(F32), jnp.add), axis=0, keepdims=True).astype(I32)
            take = (n_sel + n_one) >= topk

            def update_chunk(cn, c):
                cw = cand_ref[chunk_rows(cn), :]
                pw = plane_ref[b, chunk_rows(cn), :]
                ones = cw & pw
                cand_ref[chunk_rows(cn), :] = jnp.where(take, ones, cw & ~pw)
                sel_ref[chunk_rows(cn), :] = sel_ref[chunk_rows(cn), :] | jnp.where(take, 0, ones)
                return c

            lax.fori_loop(0, nchunk, update_chunk, 0)
            return jnp.where(take, n_sel, n_sel + n_one)

        lax.fori_loop(0, 32, bit_step, jnp.zeros((1, tq), I32))

        def final_chunk(cn, c):
            sel_ref[chunk_rows(cn), :] = sel_ref[chunk_rows(cn), :] | cand_ref[chunk_rows(cn), :]
            return c

        lax.fori_loop(0, nchunk, final_chunk, 0)

    n_here = jnp.clip(nsub - kt * subs, 0, subs)

    def sub_tile(u, c):
        g = kt * subs + u
        dsub = g - i_sub
        uoff = pl.multiple_of(u * SUB, SUB)
        sw = sel_ref[pl.ds(pl.multiple_of(g * 8, 8), 8), :]
        for k in range(32):
            mb_ref[8 * k:8 * k + 8, :] = jnp.where((sw & _bit(k)) != 0, 0.0, 2.0 * NEG)

        def heads(near):
            for h in range(N_A):
                hs = slice(h * HD_A, (h + 1) * HD_A)
                kh = k_ref[0, pl.ds(uoff, SUB), hs]
                lg = jnp.dot(kh, qt_ref[0, hs, :], preferred_element_type=F32)
                if near:
                    lg = lg + bias_ref[-dsub, h, :, 0:tq]
                lg = lg + mb_ref[...]
                lg_ref[h] = lg
                m_old = m_ref[h]
                m_new = jnp.maximum(m_old, jnp.max(fold(lg, jnp.maximum), axis=0, keepdims=True))
                al_ref[h] = jnp.exp2(m_old - m_new)
                m_ref[h] = m_new
            for h in range(N_A):
                hs = slice(h * HD_A, (h + 1) * HD_A)
                vth = vt_ref[0, hs, pl.ds(uoff, SUB)]
                alpha = al_ref[h]
                p = jnp.exp2(lg_ref[h] - m_ref[h][0:1, :])
                l_ref[h] = alpha * l_ref[h] + jnp.sum(fold(p, jnp.add), axis=0, keepdims=True)
                acc_ref[hs, :] = alpha[0:1, :] * acc_ref[hs, :] + jnp.dot(vth, p.astype(BF16),
                                                                          preferred_element_type=F32)

        @pl.when(dsub >= -1)
        def _():
            heads(True)

        @pl.when(dsub < -1)
        def _():
            heads(False)

        return c

    lax.fori_loop(0, n_here, sub_tile, 0)

    @pl.when(tl_ref[s] == 1)
    def _():
        for h in range(N_A):
            hs = slice(h * HD_A, (h + 1) * HD_A)
            o_ref[0, hs, :] = (acc_ref[hs, :] / l_ref[h][0:1, :]).astype(BF16)


def _dsa(qt, qit, wit, ki, k, vt, bias, *, tq, tk, q_off, l_valid, topk):
    bsz, sq = qt.shape[0], qt.shape[2]
    sk = k.shape[1]
    assert sq % tq == 0 and sk % tk == 0 and tk % SUB == 0 and q_off % SUB == 0
    assert (tq % SUB == 0 or sq == tq) and tq % LANES == 0
    subs = tk // SUB
    wrows = -(-(sk // SUB) // WCH) * WCH * 8
    ti, tkk, tn, tl = [], [], [], []
    for i in range(sq // tq):
        q_last = q_off + (i + 1) * tq - 1
        lim = min(l_valid, (q_last // CHUNK + 1) * CHUNK)
        nsub = -(-lim // SUB)
        nkt = -(-nsub // subs)
        for t in range(nkt):
            ti.append(i), tkk.append(t), tn.append(nsub), tl.append(int(t == nkt - 1))
    tabs = [jnp.asarray(np.asarray(a, np.int32)) for a in (ti, tkk, tn, tl)]
    body = functools.partial(_dsa_body, tq=tq, tk=tk, q_off=q_off, l_valid=l_valid, topk=topk)
    qmap = lambda b, s, ti, tk_, tn, tl: (b, 0, ti[s])
    grid_spec = pltpu.PrefetchScalarGridSpec(
        num_scalar_prefetch=4,
        grid=(bsz, len(ti)),
        in_specs=[pl.BlockSpec((1, D_A, tq), qmap),
                  pl.BlockSpec((1, D_QI, tq), qmap),
                  pl.BlockSpec((1, H_IDX, tq), qmap),
                  pl.BlockSpec((1, sk, D_IDX), lambda b, s, *_: (b, 0, 0)),
                  pl.BlockSpec((1, tk, D_A), lambda b, s, ti, tk_, tn, tl: (b, tk_[s], 0)),
                  pl.BlockSpec((1, D_A, tk), lambda b, s, ti, tk_, tn, tl: (b, 0, tk_[s])),
                  pl.BlockSpec((2, N_A, SUB, SUB), lambda b, s, *_: (0, 0, 0, 0))],
        out_specs=pl.BlockSpec((1, D_A, tq), qmap),
        scratch_shapes=[pltpu.VMEM((32, wrows, tq), I32),
                        pltpu.VMEM((wrows, tq), I32),
                        pltpu.VMEM((wrows, tq), I32),
                        pltpu.VMEM((wrows, tq), I32),
                        pltpu.VMEM((SUB, tq), F32),
                        pltpu.VMEM((N_A, 8, tq), F32),
                        pltpu.VMEM((N_A, 8, tq), F32),
                        pltpu.VMEM((D_A, tq), F32),
                        pltpu.VMEM((N_A, SUB, tq), F32),
                        pltpu.VMEM((N_A, 8, tq), F32)])
    return pl.pallas_call(
        body,
        grid_spec=grid_spec,
        out_shape=jax.ShapeDtypeStruct((bsz, D_A, sq), BF16),
        compiler_params=_cparams(("parallel", "arbitrary")),
    )(*tabs, qt, qit, wit, ki, k, vt, bias)


def _mlstm_body(q_ref, k_ref, kt_ref, v_ref, og_ref, sm_ref, gr_ref, gh_ref, c0_ref, n0_ref, m0_ref,
                hb_ref, co_ref, no_ref, mo_ref, c_s, n_s, m_s, *, lc):
    c = pl.program_id(1)

    @pl.when(c == 0)
    def _():
        c_s[...] = c0_ref[0]
        for h in range(N_B):
            n_s[h] = jnp.broadcast_to(n0_ref[0, h:h + 1, :], (8, HD_B))
            m_s[h] = jnp.broadcast_to(m0_ref[0, h:h + 1, :], (8, LANES))

    row = lax.broadcasted_iota(I32, (lc, lc), 0)
    col = lax.broadcasted_iota(I32, (lc, lc), 1)
    causal = col <= row
    sm = sm_ref[0]
    gr = gr_ref[0]
    hp = lax.Precision.HIGHEST
    b_cols = jnp.dot(causal.astype(F32), sm, precision=hp, preferred_element_type=F32)
    b_rows = jnp.dot(gr, (row <= col).astype(F32), precision=hp, preferred_element_type=F32)

    for h in range(N_B):
        hs = slice(h * HD_B, (h + 1) * HD_B)
        m_prev = m_s[h][0:1, 0:1]
        bc = b_cols[:, SM_LF + h:SM_LF + h + 1]
        ig_c = sm[:, SM_IG + h:SM_IG + h + 1]
        br = b_rows[N_B + h:N_B + h + 1, :]
        ig_r = gr[h:h + 1, :]
        log_d = jnp.where(causal, bc - br + ig_r, NEG)
        log_inter = bc + m_prev
        m_t = jnp.maximum(log_inter, jnp.max(log_d, axis=1, keepdims=True))
        d = jnp.exp(log_d - m_t)
        inter = jnp.exp(log_inter - m_t)
        qh = q_ref[0, :, hs]
        kth = kt_ref[0, hs, :]
        vh = v_ref[0, :, hs]
        sc = jnp.dot(qh, kth, preferred_element_type=F32) * d
        ch = c_s[h]
        nrow = n_s[h][0:1, :]
        num = (jnp.dot(sc.astype(BF16), vh, preferred_element_type=F32)
               + inter * jnp.dot(qh, ch.astype(BF16), preferred_element_type=F32))
        den = (jnp.sum(sc, axis=1, keepdims=True)
               + inter * jnp.sum(qh.astype(F32) * nrow, axis=1, keepdims=True))
        hh = num / jnp.maximum(jnp.abs(den), jnp.exp(-m_t))
        hb = _rms(hh, gh_ref[...]) * og_ref[0, :, hs].astype(F32)
        hb_ref[0, :, hs] = hb.astype(BF16)

        m_new = m_t[lc - 1:lc, :]
        b_last = bc[lc - 1:lc, :]
        decay = jnp.exp(b_last + m_prev - m_new)
        w_r = jnp.exp(b_last - br + ig_r - m_new)
        w_c = jnp.exp(b_last - bc + ig_c - m_new)
        kw = (kth.astype(F32) * w_r).astype(BF16)
        c_s[h] = decay * ch + jnp.dot(kw, vh, preferred_element_type=F32)
        n_new = decay * nrow + jnp.sum(k_ref[0, :, hs].astype(F32) * w_c, axis=0, keepdims=True)
        n_s[h] = jnp.broadcast_to(n_new, (8, HD_B))
        m_s[h] = jnp.broadcast_to(m_new, (8, LANES))

    @pl.when(c == pl.num_programs(1) - 1)
    def _():
        co_ref[0] = c_s[...]
        for h in range(N_B):
            no_ref[0, h:h + 1, :] = n_s[h][0:1, :]
            mo_ref[0, h:h + 1, :] = m_s[h][0:1, :]


def _mlstm(q, k, kt, v, og, sm, gr, g_h, c0, n0, m0, *, lc):
    bsz, s = q.shape[0], q.shape[1]
    tokb = lambda w: pl.BlockSpec((1, lc, w), lambda b, c: (b, c, 0))
    st3 = lambda w: pl.BlockSpec((1, N_B, w), lambda b, c: (b, 0, 0))
    st4 = pl.BlockSpec((1, N_B, HD_B, HD_B), lambda b, c: (b, 0, 0, 0))
    return pl.pallas_call(
        functools.partial(_mlstm_body, lc=lc),
        grid=(bsz, s // lc),
        in_specs=[tokb(D_B), tokb(D_B),
                  pl.BlockSpec((1, D_B, lc), lambda b, c: (b, 0, c)),
                  tokb(D_B), tokb(D_B), tokb(LANES),
                  pl.BlockSpec((1, 2 * N_B, lc), lambda b, c: (b, 0, c)),
                  pl.BlockSpec((1, HD_B), lambda b, c: (0, 0)),
                  st4, st3(HD_B), st3(LANES)],
        out_specs=[tokb(D_B), st4, st3(HD_B), st3(LANES)],
        out_shape=[jax.ShapeDtypeStruct((bsz, s, D_B), BF16),
                   jax.ShapeDtypeStruct((bsz, N_B, HD_B, HD_B), F32),
                   jax.ShapeDtypeStruct((bsz, N_B, HD_B), F32),
                   jax.ShapeDtypeStruct((bsz, N_B, LANES), F32)],
        scratch_shapes=[pltpu.VMEM((N_B, HD_B, HD_B), F32),
                        pltpu.VMEM((N_B, 8, HD_B), F32),
                        pltpu.VMEM((N_B, 8, LANES), F32)],
        compiler_params=_cparams(("parallel", "arbitrary")),
    )(q, k, kt, v, og, sm, gr, g_h, c0, n0, m0)


def _mixer_front(x, ffn1, inp):
    x1 = _ffn(x, *ffn1)
    return (x1,) + tuple(_inproj(x1, *inp))


def kernel(x_prompt, x_sample, cache_k, cache_v, cache_k_idx, state_C, state_n, state_m, g_ffn1, w1_ffn1, w3_ffn1, w2_ffn1, g_mix, w_in, g_q, g_k, g_kidx, rel_bias, b_i, b_f, g_h, w_out, g_ffn2, w1_ffn2, w3_ffn2, w2_ffn2):
    bsz, seq = x_prompt.shape[0], x_prompt.shape[1]
    dbs, dseq = x_sample.shape[0], x_sample.shape[1]
    past = cache_k.shape[1]

    ffn1 = _prep_ffn(g_ffn1, w1_ffn1, w3_ffn1, w2_ffn1)
    ffn2 = _prep_ffn(g_ffn2, w1_ffn2, w3_ffn2, w2_ffn2)
    inp = _prep_inproj(g_mix, w_in, g_q, g_k, g_kidx, b_i, b_f)
    wa = w_out[:D_A].astype(BF16)
    wh = w_out[D_A:].astype(BF16)
    gh = g_h.reshape(1, HD_B)
    bias = _bias_tiles(rel_bias)

    n_p = bsz * seq
    x1, q, kf, k16, vf, v16, qi, qb, kb, vb, og, sm = _mixer_front(x_prompt.reshape(n_p, D_MODEL), ffn1, inp)
    r3 = lambda a, b_, s_: a.reshape(b_, s_, a.shape[-1])
    sm3 = r3(sm, bsz, seq)
    tr = lambda a: jnp.swapaxes(a, 1, 2)
    tq_p = min(SUB, seq)
    attn = tr(_dsa(tr(r3(q, bsz, seq)), tr(r3(qi, bsz, seq)), tr(sm3[:, :, SM_WI:SM_IG]),
                   sm3[:, :, :D_IDX].astype(BF16), r3(k16, bsz, seq), tr(r3(v16, bsz, seq)), bias,
                   tq=tq_p, tk=min(1024, seq), q_off=0, l_valid=seq, topk=min(TOPK_MAX, seq // 4)))
    gr = jnp.swapaxes(sm3[:, :, SM_IG:SM_END], 1, 2)
    kb3 = r3(kb, bsz, seq)
    hb, c_p, n_p_, m_p = _mlstm(r3(qb, bsz, seq), kb3, jnp.swapaxes(kb3, 1, 2), r3(vb, bsz, seq),
                                r3(og, bsz, seq), sm3, gr, gh,
                                jnp.zeros((bsz, N_B, HD_B, HD_B), F32), jnp.zeros((bsz, N_B, HD_B), F32),
                                jnp.zeros((bsz, N_B, LANES), F32), lc=min(SUB, seq))
    x2 = _outproj(x1, attn.reshape(n_p, D_A), hb.reshape(n_p, D_B), wa, wh)
    y_prompt = _ffn(x2, *ffn2).reshape(bsz, seq, D_MODEL)

    n_s = dbs * dseq
    x1s, qs, kfs, k16s, vfs, v16s, qis, qbs, kbs, vbs, ogs, sms = _mixer_front(
        x_sample.reshape(n_s, D_MODEL), ffn1, inp)
    sms3 = r3(sms, dbs, dseq)
    l_all = past + dseq
    tk_s = 3 * SUB
    sk = -(-l_all // tk_s) * tk_s
    padk = lambda a: jnp.pad(a, ((0, 0), (0, sk - l_all), (0, 0)))
    k_all = padk(jnp.concatenate([cache_k.reshape(dbs, past, D_A).astype(BF16), r3(k16s, dbs, dseq)], axis=1))
    v_all = padk(jnp.concatenate([cache_v.reshape(dbs, past, D_A).astype(BF16), r3(v16s, dbs, dseq)], axis=1))
    ki_all = padk(jnp.concatenate([cache_k_idx.astype(BF16), sms3[:, :, :D_IDX].astype(BF16)], axis=1))
    assert dseq <= LANES
    padq = lambda a: jnp.pad(tr(a), ((0, 0), (0, 0), (0, LANES - dseq)))
    attn_s = tr(_dsa(padq(r3(qs, dbs, dseq)), padq(r3(qis, dbs, dseq)), padq(sms3[:, :, SM_WI:SM_IG]),
                     ki_all, k_all, tr(v_all), bias,
                     tq=LANES, tk=tk_s, q_off=past, l_valid=l_all, topk=min(TOPK_MAX, l_all // 4))[:, :, :dseq])
    lc_s = LANES
    padt = lambda a: jnp.pad(a, ((0, 0), (0, lc_s - dseq), (0, 0)))
    lane = jnp.arange(LANES)
    sm_pad = jnp.where((lane >= SM_IG) & (lane < SM_LF), NEG, 0.0).astype(F32)
    sms_p = jnp.concatenate([sms3, jnp.broadcast_to(sm_pad, (dbs, lc_s - dseq, LANES))], axis=1)
    kbs3 = padt(r3(kbs, dbs, dseq))
    hbs, c_s, n_s_, m_s = _mlstm(padt(r3(qbs, dbs, dseq)), kbs3, jnp.swapaxes(kbs3, 1, 2), padt(r3(vbs, dbs, dseq)),
                                 padt(r3(ogs, dbs, dseq)), sms_p, jnp.swapaxes(sms_p[:, :, SM_IG:SM_END], 1, 2), gh,
                                 state_C.astype(F32), state_n.astype(F32),
                                 jnp.broadcast_to(state_m.astype(F32)[:, :, None], (dbs, N_B, LANES)), lc=lc_s)
    x2s = _outproj(x1s, attn_s.reshape(n_s, D_A), hbs[:, :dseq].reshape(n_s, D_B), wa, wh)
    y_sample = _ffn(x2s, *ffn2).reshape(dbs, dseq, D_MODEL)

    sd = state_C.dtype
    return (y_prompt, y_sample,
            kf.reshape(bsz, seq, N_A, HD_A), vf.reshape(bsz, seq, N_A, HD_A), sm3[:, :, :D_IDX],
            c_p.astype(sd), n_p_.astype(sd), m_p[:, :, 0].astype(sd),
            kfs.reshape(dbs, dseq, N_A, HD_A), vfs.reshape(dbs, dseq, N_A, HD_A), sms3[:, :, :D_IDX],
            c_s.astype(sd), n_s_.astype(sd), m_s[:, :, 0].astype(sd))
```

```python
import functools
import math

import numpy as np
import jax
import jax.numpy as jnp
from jax import lax
from jax.experimental import pallas as pl
from jax.experimental.pallas import tpu as pltpu

F32 = jnp.float32
BF16 = jnp.bfloat16
I32 = jnp.int32

D_MODEL = 2048
CHUNK = 64
N_A, HD_A = 8, 128
H_IDX, D_IDX = 16, 64
TOPK_MAX = 256
N_B, HD_B = 4, 256
D_FF = 5504
NUM_BUCKETS, MAX_DISTANCE = 32, 128
EPS = 1e-6
NEG = -1e30
INT_MIN = -(2 ** 31)
LOG2E = math.log2(math.e)

LANES = 128
D_A = N_A * HD_A
D_B = N_B * HD_B
D_QI = H_IDX * D_IDX
FF_TILE = 512
D_FF_PAD = -(-D_FF // FF_TILE) * FF_TILE
FFN_TOK_TILE = 512
PROJ_TOK_TILE = 256
TOK_TILE = 512
SUB = 256
VMEM_LIMIT = 56 * 1024 * 1024

SM_KI, SM_WI, SM_IG, SM_LF = 0, D_IDX, D_IDX + H_IDX, D_IDX + H_IDX + N_B
SM_END = SM_LF + N_B


def _cparams(sem):
    return pltpu.CompilerParams(dimension_semantics=sem, vmem_limit_bytes=VMEM_LIMIT)


def _rms(x, g):
    ms = jnp.mean(x * x, axis=-1, keepdims=True)
    return x * lax.rsqrt(ms + EPS) * g


def _ffn_body(x_ref, g_ref, w1_ref, w3_ref, w2_ref, o_ref, hn_ref):
    j = pl.program_id(1)

    @pl.when(j == 0)
    def _():
        hn_ref[...] = _rms(x_ref[...], g_ref[...]).astype(BF16)
        o_ref[...] = jnp.zeros(o_ref.shape, F32)

    h = hn_ref[...]
    a = jnp.dot(h, w1_ref[...], preferred_element_type=F32)
    b = jnp.dot(h, w3_ref[...], preferred_element_type=F32)
    u = (a * jax.nn.sigmoid(a) * b).astype(BF16)
    o_ref[...] += jnp.dot(u, w2_ref[...], preferred_element_type=F32)

    @pl.when(j == pl.num_programs(1) - 1)
    def _():
        o_ref[...] = x_ref[...] + 0.5 * o_ref[...]


def _ffn(x, g, w1, w3, w2):
    n = x.shape[0]
    tm = min(FFN_TOK_TILE, n)
    grid = (n // tm, D_FF_PAD // FF_TILE)
    return pl.pallas_call(
        _ffn_body,
        grid=grid,
        in_specs=[
            pl.BlockSpec((tm, D_MODEL), lambda i, j: (i, 0)),
            pl.BlockSpec((1, D_MODEL), lambda i, j: (0, 0)),
            pl.BlockSpec((D_MODEL, FF_TILE), lambda i, j: (0, j)),
            pl.BlockSpec((D_MODEL, FF_TILE), lambda i, j: (0, j)),
            pl.BlockSpec((FF_TILE, D_MODEL), lambda i, j: (j, 0)),
        ],
        out_specs=pl.BlockSpec((tm, D_MODEL), lambda i, j: (i, 0)),
        out_shape=jax.ShapeDtypeStruct((n, D_MODEL), F32),
        scratch_shapes=[pltpu.VMEM((tm, D_MODEL), BF16)],
        compiler_params=_cparams(("parallel", "arbitrary")),
    )(x, g, w1, w3, w2)


def _prep_ffn(g, w1, w3, w2):
    pad = D_FF_PAD - D_FF
    return (g.reshape(1, D_MODEL),
            jnp.pad(w1.astype(BF16), ((0, 0), (0, pad))),
            jnp.pad(w3.astype(BF16), ((0, 0), (0, pad))),
            jnp.pad(w2.astype(BF16), ((0, pad), (0, 0))))


N_MAIN_GROUPS = 8


def _head_norm(p, g, heads, hd):
    outs = []
    for h in range(heads):
        outs.append(_rms(p[:, h * hd:(h + 1) * hd], g))
    return outs


def _inproj_a_body(x_ref, g_ref, w_ref, wsh_ref, wsl_ref, gq_ref, gk_ref, sg_ref, sb_ref,
                   q_ref, kf_ref, k16_ref, vf_ref, v16_ref, qi_ref, sm_ref):
    h32 = _rms(x_ref[...], g_ref[...])
    h = h32.astype(BF16)
    col = lambda c: jnp.dot(h, w_ref[:, c * D_A:(c + 1) * D_A], preferred_element_type=F32)

    h_lo = (h32 - h.astype(F32)).astype(BF16)
    ps = (jnp.dot(h, wsh_ref[...], preferred_element_type=F32)
          + jnp.dot(h_lo, wsh_ref[...], preferred_element_type=F32)
          + jnp.dot(h, wsl_ref[...], preferred_element_type=F32))
    lane = lax.broadcasted_iota(I32, ps.shape, 1)
    ms = jnp.sum(jnp.where(lane < SM_WI, ps * ps, 0.0), axis=-1, keepdims=True) * (1.0 / D_IDX)
    kin = ps * lax.rsqrt(ms + EPS) * sg_ref[...]
    z = ps + sb_ref[...]
    ls = jnp.minimum(z, 0.0) - jnp.log1p(jnp.exp(-jnp.abs(z)))
    sm_ref[...] = jnp.where(lane < SM_WI, kin,
                            jnp.where(lane < SM_IG, ps,
                                      jnp.where(lane < SM_LF, z,
                                                jnp.where(lane < SM_END, ls, 0.0))))

    for hd, qh in enumerate(_head_norm(col(0), gq_ref[...], N_A, HD_A)):
        q_ref[:, hd * HD_A:(hd + 1) * HD_A] = (qh * (HD_A ** -0.5 * LOG2E)).astype(BF16)
    for hd, kh in enumerate(_head_norm(col(1), gk_ref[...], N_A, HD_A)):
        kf_ref[:, hd * HD_A:(hd + 1) * HD_A] = kh
        k16_ref[:, hd * HD_A:(hd + 1) * HD_A] = kh.astype(BF16)
    p = col(2)
    vf_ref[...] = p
    v16_ref[...] = p.astype(BF16)
    qi_ref[...] = col(3).astype(BF16)


def _inproj_b_body(x_ref, g_ref, w_ref, qb_ref, kb_ref, vb_ref, og_ref):
    h = _rms(x_ref[...], g_ref[...]).astype(BF16)
    col = lambda c: jnp.dot(h, w_ref[:, c * D_B:(c + 1) * D_B], preferred_element_type=F32)
    qb_ref[...] = col(0).astype(BF16)
    kb_ref[...] = (col(1) * HD_B ** -0.5).astype(BF16)
    vb_ref[...] = col(2).astype(BF16)
    og_ref[...] = jax.nn.sigmoid(col(3)).astype(BF16)


def _inproj(x, g, wa, wb, wsh, wsl, gq, gk, sg, sb):
    n = x.shape[0]
    tm = min(PROJ_TOK_TILE, n)
    tok = lambda w: pl.BlockSpec((tm, w), lambda i: (i, 0))
    const = lambda a: pl.BlockSpec(a.shape, lambda i: (0,) * a.ndim)
    wide = lambda dt: jax.ShapeDtypeStruct((n, D_A), dt)
    outs_a = pl.pallas_call(
        _inproj_a_body,
        grid=(n // tm,),
        in_specs=[tok(D_MODEL)] + [const(a) for a in (g, wa, wsh, wsl, gq, gk, sg, sb)],
        out_specs=[tok(D_A)] * 6 + [tok(LANES)],
        out_shape=[wide(BF16), wide(F32), wide(BF16), wide(F32), wide(BF16), wide(BF16),
                   jax.ShapeDtypeStruct((n, LANES), F32)],
        compiler_params=_cparams(("parallel",)),
    )(x, g, wa, wsh, wsl, gq, gk, sg, sb)
    outs_b = pl.pallas_call(
        _inproj_b_body,
        grid=(n // tm,),
        in_specs=[tok(D_MODEL), const(g), const(wb)],
        out_specs=[tok(D_B)] * 4,
        out_shape=[wide(BF16)] * 4,
        compiler_params=_cparams(("parallel",)),
    )(x, g, wb)
    q, kf, k16, vf, v16, qi, sm = outs_a
    return (q, kf, k16, vf, v16, qi) + tuple(outs_b) + (sm,)


def _prep_inproj(g_mix, w_in, g_q, g_k, g_kidx, b_i, b_f):
    o_ki = 4 * D_A
    o_qb = o_ki + D_IDX + H_IDX
    o_ib = o_qb + 4 * D_B
    wa = w_in[:, :o_ki].astype(BF16)
    wb = w_in[:, o_qb:o_ib].astype(BF16)
    ws = jnp.concatenate([w_in[:, o_ki:o_qb], w_in[:, o_ib:],
                          jnp.zeros((D_MODEL, LANES - SM_END), F32)], axis=1)
    wsh = ws.astype(BF16)
    wsl = (ws - wsh.astype(F32)).astype(BF16)
    sg = jnp.concatenate([g_kidx, jnp.ones((LANES - D_IDX,), F32)]).reshape(1, LANES)
    sb = jnp.concatenate([jnp.zeros((SM_IG,), F32), b_i, b_f,
                          jnp.zeros((LANES - SM_END,), F32)]).reshape(1, LANES)
    return (g_mix.reshape(1, D_MODEL), wa, wb, wsh, wsl,
            g_q.reshape(1, HD_A), g_k.reshape(1, HD_A), sg, sb)


def _outproj_body(x_ref, a_ref, h_ref, wa_ref, wh_ref, o_ref):
    o_ref[...] = (x_ref[...]
                  + jnp.dot(a_ref[...], wa_ref[...], preferred_element_type=F32)
                  + jnp.dot(h_ref[...], wh_ref[...], preferred_element_type=F32))


def _outproj(x, a, h, wa, wh):
    n = x.shape[0]
    tm = min(TOK_TILE, n)
    return pl.pallas_call(
        _outproj_body,
        grid=(n // tm,),
        in_specs=[pl.BlockSpec((tm, D_MODEL), lambda i: (i, 0)),
                  pl.BlockSpec((tm, D_A), lambda i: (i, 0)),
                  pl.BlockSpec((tm, D_B), lambda i: (i, 0)),
                  pl.BlockSpec((D_A, D_MODEL), lambda i: (0, 0)),
                  pl.BlockSpec((D_B, D_MODEL), lambda i: (0, 0))],
        out_specs=pl.BlockSpec((tm, D_MODEL), lambda i: (i, 0)),
        out_shape=jax.ShapeDtypeStruct((n, D_MODEL), F32),
        compiler_params=_cparams(("parallel",)),
    )(x, a, h, wa, wh)


def _bucket_thresholds():
    nb = NUM_BUCKETS // 2
    max_exact = nb // 2
    span = nb - max_exact
    ratio = MAX_DISTANCE // max_exact
    out = []
    for m in range(1, span):
        n = max_exact
        while n ** span < max_exact ** span * ratio ** m:
            n += 1
        out.append(n)
    return tuple(out)


def _bias_body(tbl_ref, o_ref):
    o = pl.program_id(0)
    h = pl.program_id(1)
    nb = NUM_BUCKETS // 2
    max_exact = nb // 2
    row = lax.broadcasted_iota(I32, (SUB, SUB), 0)
    col = lax.broadcasted_iota(I32, (SUB, SUB), 1)
    rel = row - o * SUB - col
    n = jnp.abs(rel)
    large = jnp.full((SUB, SUB), max_exact, I32)
    for t in _bucket_thresholds():
        large = large + (n >= t).astype(I32)
    bucket = jnp.where(rel > 0, nb, 0) + jnp.where(n < max_exact, n, large)
    val = jnp.zeros((SUB, SUB), F32)
    for bk in range(NUM_BUCKETS):
        val = jnp.where(bucket == bk, tbl_ref[bk, h], val)
    o_ref[0, 0] = (val - tbl_ref[nb - 1, h]) * LOG2E


def _bias_tiles(rel_bias):
    return pl.pallas_call(
        _bias_body,
        grid=(2, N_A),
        in_specs=[pl.BlockSpec(memory_space=pltpu.SMEM)],
        out_specs=pl.BlockSpec((1, 1, SUB, SUB), lambda o, h: (o, h, 0, 0)),
        out_shape=jax.ShapeDtypeStruct((2, N_A, SUB, SUB), F32),
    )(rel_bias)


WCH = 4
A_UNROLL = 4


def _bit(k):
    return INT_MIN if k == 31 else 1 << k


def _bit_transpose32(a):
    a = list(a)
    j, m = 16, 0x0000FFFF
    while j:
        for k in range(32):
            if not k & j:
                t = (lax.shift_right_logical(a[k], jnp.int32(j)) ^ a[k + j]) & m
                a[k + j] = a[k + j] ^ t
                a[k] = a[k] ^ jnp.left_shift(t, jnp.int32(j))
        j >>= 1
        m ^= m << j
    return a


def _dsa_body(ti_ref, tk_ref, tn_ref, tl_ref,
              qt_ref, qit_ref, wit_ref, ki_ref, k_ref, vt_ref, bias_ref, o_ref,
              plane_ref, adm_ref, cand_ref, sel_ref, mb_ref, m_ref, l_ref, acc_ref, lg_ref, al_ref,
              *, tq, tk, q_off, l_valid, topk):
    s = pl.program_id(1)
    i = ti_ref[s]
    kt = tk_ref[s]
    nsub = tn_ref[s]
    subs = tk // SUB
    q_pos0 = q_off + i * tq
    i_sub = q_off // SUB + (i * tq) // SUB

    def fold(x, op, rows=8):
        parts = [x[r:r + rows] for r in range(0, x.shape[0], rows)]
        while len(parts) > 1:
            parts = [op(parts[a], parts[a + 1]) for a in range(0, len(parts), 2)]
        return parts[0]

    @pl.when(kt == 0)
    def _():
        m_ref[...] = jnp.full(m_ref.shape, NEG, F32)
        l_ref[...] = jnp.zeros(l_ref.shape, F32)
        acc_ref[...] = jnp.zeros(acc_ref.shape, F32)

        qchunk = (q_pos0 + lax.broadcasted_iota(I32, (SUB, tq), 1)) >> 6
        krow = lax.broadcasted_iota(I32, (SUB, tq), 0)

        def score_codes(t):
            off = pl.multiple_of(t * SUB, SUB)
            kit = ki_ref[0, pl.ds(off, SUB), :]
            sc = jnp.zeros((SUB, tq), F32)
            for j in range(H_IDX):
                sj = jnp.dot(kit, qit_ref[0, j * D_IDX:(j + 1) * D_IDX, :], preferred_element_type=F32)
                sc = sc + jnp.maximum(sj, 0.0) * wit_ref[0, j:j + 1, :]
            bits = lax.bitcast_convert_type(sc, I32)
            return bits ^ ((bits >> 31) | INT_MIN)

        def slice_codes(t, code):
            kpos = t * SUB + krow
            adm = ((kpos >> 6) <= qchunk) & (kpos < l_valid)
            planes = _bit_transpose32([code[8 * k:8 * k + 8] for k in range(32)])
            row = pl.multiple_of(t * 8, 8)
            aw = jnp.zeros((8, tq), I32)
            for k in range(32):
                plane_ref[k, pl.ds(row, 8), :] = planes[k]
                aw = aw | jnp.where(adm[8 * k:8 * k + 8], _bit(k), 0)
            adm_ref[pl.ds(row, 8), :] = aw

        def score_tiles(gi, c):
            for u in range(A_UNROLL):
                slice_codes(gi * A_UNROLL + u, score_codes(gi * A_UNROLL + u))
            return c

        def score_tile(t, c):
            slice_codes(t, score_codes(t))
            return c

        lax.fori_loop(0, nsub // A_UNROLL, score_tiles, 0)
        lax.fori_loop(nsub // A_UNROLL * A_UNROLL, nsub, score_tile, 0)

        nchunk = (nsub + WCH - 1) // WCH

        def clear_tile(t, c):
            row = pl.multiple_of(t * 8, 8)
            plane_ref[:, pl.ds(row, 8), :] = jnp.zeros((32, 8, tq), I32)
            adm_ref[pl.ds(row, 8), :] = jnp.zeros((8, tq), I32)
            return c

        lax.fori_loop(nsub, nchunk * WCH, clear_tile, 0)

        def chunk_rows(cn):
            return pl.ds(pl.multiple_of(cn * (8 * WCH), 8 * WCH), 8 * WCH)

        def init_chunk(cn, c):
            cand_ref[chunk_rows(cn), :] = adm_ref[chunk_rows(cn), :]
            sel_ref[chunk_rows(cn), :] = jnp.zeros((8 * WCH, tq), I32)
            return c

        lax.fori_loop(0, nchunk, init_chunk, 0)

        def bit_step(it, n_sel):
            b = 31 - it

            def count_chunk(cn, cnt):
                ones = cand_ref[chunk_rows(cn), :] & plane_ref[b, chunk_rows(cn), :]
                return cnt + lax.population_count(ones)

            cnt = lax.fori_loop(0, nchunk, count_chunk, jnp.zeros((8 * WCH, tq), I32))
            n_one = jnp.sum(fold(cnt.astype(F32), jnp.add), axis=0, keepdims=True).astype(I32)
            take = (n_sel + n_one) >= topk

            def update_chunk(cn, c):
                cw = cand_ref[chunk_rows(cn), :]
                pw = plane_ref[b, chunk_rows(cn), :]
                ones = cw & pw
                cand_ref[chunk_rows(cn), :] = jnp.where(take, ones, cw & ~pw)
                sel_ref[chunk_rows(cn), :] = sel_ref[chunk_rows(cn), :] | jnp.where(take, 0, ones)
                return c

            lax.fori_loop(0, nchunk, update_chunk, 0)
            return jnp.where(take, n_sel, n_sel + n_one)

        lax.fori_loop(0, 32, bit_step, jnp.zeros((1, tq), I32))

        def final_chunk(cn, c):
            sel_ref[chunk_rows(cn), :] = sel_ref[chunk_rows(cn), :] | cand_ref[chunk_rows(cn), :]
            return c

        lax.fori_loop(0, nchunk, final_chunk, 0)

    n_here = jnp.clip(nsub - kt * subs, 0, subs)

    def sub_tile(u, near, slot):
        g = kt * subs + u
        dsub = g - i_sub
        uoff = pl.multiple_of(u * SUB, SUB)
        sw = sel_ref[pl.ds(pl.multiple_of(g * 8, 8), 8), :]
        for k in range(32):
            mb_ref[slot, 8 * k:8 * k + 8, :] = jnp.where((sw & _bit(k)) != 0, 0.0, 2.0 * NEG)
        for h in range(N_A):
            hs = slice(h * HD_A, (h + 1) * HD_A)
            kh = k_ref[0, pl.ds(uoff, SUB), hs]
            lg = jnp.dot(kh, qt_ref[0, hs, :], preferred_element_type=F32)
            if near:
                lg = lg + bias_ref[-dsub, h, :, 0:tq]
            lg = lg + mb_ref[slot]
            lg_ref[slot, h] = lg
            m_old = m_ref[h]
            m_new = jnp.maximum(m_old, jnp.max(fold(lg, jnp.maximum), axis=0, keepdims=True))
            al_ref[slot, h] = jnp.exp2(m_old - m_new)
            m_ref[h] = m_new
        for h in range(N_A):
            hs = slice(h * HD_A, (h + 1) * HD_A)
            vth = vt_ref[0, hs, pl.ds(uoff, SUB)]
            alpha = al_ref[slot, h]
            p = jnp.exp2((lg_ref[slot, h] - m_ref[h][0:1, :]).astype(BF16))
            pv = jnp.dot(jnp.concatenate([vth, jnp.ones((16, SUB), BF16)], axis=0), p,
                         preferred_element_type=F32)
            l_ref[h] = alpha * l_ref[h] + pv[HD_A:HD_A + 8, :]
            acc_ref[hs, :] = alpha[0:1, :] * acc_ref[hs, :] + pv[0:HD_A, :]

    n_far = jnp.clip(i_sub - 1 - kt * subs, 0, n_here)

    def far_pair(pi, c):
        sub_tile(2 * pi, False, 0)
        sub_tile(2 * pi + 1, False, 1)
        return c

    def far_one(u, c):
        sub_tile(u, False, 0)
        return c

    def near_one(u, c):
        sub_tile(u, True, 0)
        return c

    lax.fori_loop(0, n_far // 2, far_pair, 0)
    lax.fori_loop(n_far // 2 * 2, n_far, far_one, 0)
    lax.fori_loop(n_far, n_here, near_one, 0)

    @pl.when(tl_ref[s] == 1)
    def _():
        for h in range(N_A):
            hs = slice(h * HD_A, (h + 1) * HD_A)
            o_ref[0, hs, :] = (acc_ref[hs, :] / l_ref[h][0:1, :]).astype(BF16)


def _dsa(qt, qit, wit, ki, k, vt, bias, *, tq, tk, q_off, l_valid, topk):
    bsz, sq = qt.shape[0], qt.shape[2]
    sk = k.shape[1]
    assert sq % tq == 0 and sk % tk == 0 and tk % SUB == 0 and q_off % SUB == 0
    assert (tq % SUB == 0 or sq == tq) and tq % LANES == 0
    subs = tk // SUB
    wrows = -(-(sk // SUB) // WCH) * WCH * 8
    ti, tkk, tn, tl = [], [], [], []
    for i in range(sq // tq):
        q_last = q_off + (i + 1) * tq - 1
        lim = min(l_valid, (q_last // CHUNK + 1) * CHUNK)
        nsub = -(-lim // SUB)
        nkt = -(-nsub // subs)
        for t in range(nkt):
            ti.append(i), tkk.append(t), tn.append(nsub), tl.append(int(t == nkt - 1))
    tabs = [jnp.asarray(np.asarray(a, np.int32)) for a in (ti, tkk, tn, tl)]
    body = functools.partial(_dsa_body, tq=tq, tk=tk, q_off=q_off, l_valid=l_valid, topk=topk)
    qmap = lambda b, s, ti, tk_, tn, tl: (b, 0, ti[s])
    grid_spec = pltpu.PrefetchScalarGridSpec(
        num_scalar_prefetch=4,
        grid=(bsz, len(ti)),
        in_specs=[pl.BlockSpec((1, D_A, tq), qmap),
                  pl.BlockSpec((1, D_QI, tq), qmap),
                  pl.BlockSpec((1, H_IDX, tq), qmap),
                  pl.BlockSpec((1, sk, D_IDX), lambda b, s, *_: (b, 0, 0)),
                  pl.BlockSpec((1, tk, D_A), lambda b, s, ti, tk_, tn, tl: (b, tk_[s], 0)),
                  pl.BlockSpec((1, D_A, tk), lambda b, s, ti, tk_, tn, tl: (b, 0, tk_[s])),
                  pl.BlockSpec((2, N_A, SUB, SUB), lambda b, s, *_: (0, 0, 0, 0))],
        out_specs=pl.BlockSpec((1, D_A, tq), qmap),
        scratch_shapes=[pltpu.VMEM((32, wrows, tq), I32),
                        pltpu.VMEM((wrows, tq), I32),
                        pltpu.VMEM((wrows, tq), I32),
                        pltpu.VMEM((wrows, tq), I32),
                        pltpu.VMEM((2, SUB, tq), F32),
                        pltpu.VMEM((N_A, 8, tq), F32),
                        pltpu.VMEM((N_A, 8, tq), F32),
                        pltpu.VMEM((D_A, tq), F32),
                        pltpu.VMEM((2, N_A, SUB, tq), F32),
                        pltpu.VMEM((2, N_A, 8, tq), F32)])
    return pl.pallas_call(
        body,
        grid_spec=grid_spec,
        out_shape=jax.ShapeDtypeStruct((bsz, D_A, sq), BF16),
        compiler_params=_cparams(("parallel", "arbitrary")),
    )(*tabs, qt, qit, wit, ki, k, vt, bias)


def _mlstm_body(q_ref, k_ref, kt_ref, v_ref, og_ref, sm_ref, gr_ref, gh_ref, c0_ref, n0_ref, m0_ref,
                hb_ref, co_ref, no_ref, mo_ref, c_s, n_s, m_s, *, lc):
    c = pl.program_id(1)

    @pl.when(c == 0)
    def _():
        c_s[...] = c0_ref[0]
        for h in range(N_B):
            n_s[h] = jnp.broadcast_to(n0_ref[0, h:h + 1, :], (8, HD_B))
            m_s[h] = jnp.broadcast_to(m0_ref[0, h:h + 1, :], (8, LANES))

    row = lax.broadcasted_iota(I32, (lc, lc), 0)
    col = lax.broadcasted_iota(I32, (lc, lc), 1)
    causal = col <= row
    sm = sm_ref[0]
    gr = gr_ref[0]
    hp = lax.Precision.HIGHEST
    b_cols = jnp.dot(causal.astype(F32), sm, precision=hp, preferred_element_type=F32)
    b_rows = jnp.dot(gr, (row <= col).astype(F32), precision=hp, preferred_element_type=F32)

    for h in range(N_B):
        hs = slice(h * HD_B, (h + 1) * HD_B)
        m_prev = m_s[h][0:1, 0:1]
        bc = b_cols[:, SM_LF + h:SM_LF + h + 1]
        ig_c = sm[:, SM_IG + h:SM_IG + h + 1]
        br = b_rows[N_B + h:N_B + h + 1, :]
        ig_r = gr[h:h + 1, :]
        log_d = jnp.where(causal, bc - br + ig_r, NEG)
        log_inter = bc + m_prev
        m_t = jnp.maximum(log_inter, jnp.max(log_d, axis=1, keepdims=True))
        d = jnp.exp(log_d - m_t)
        inter = jnp.exp(log_inter - m_t)
        qh = q_ref[0, :, hs]
        kth = kt_ref[0, hs, :]
        vh = v_ref[0, :, hs]
        sc = jnp.dot(qh, kth, preferred_element_type=F32) * d
        ch = c_s[h]
        nrow = n_s[h][0:1, :]
        num = (jnp.dot(sc.astype(BF16), vh, preferred_element_type=F32)
               + inter * jnp.dot(qh, ch.astype(BF16), preferred_element_type=F32))
        den = (jnp.sum(sc, axis=1, keepdims=True)
               + inter * jnp.sum(qh.astype(F32) * nrow, axis=1, keepdims=True))
        hh = num / jnp.maximum(jnp.abs(den), jnp.exp(-m_t))
        hb = _rms(hh, gh_ref[...]) * og_ref[0, :, hs].astype(F32)
        hb_ref[0, :, hs] = hb.astype(BF16)

        m_new = m_t[lc - 1:lc, :]
        b_last = bc[lc - 1:lc, :]
        decay = jnp.exp(b_last + m_prev - m_new)
        w_r = jnp.exp(b_last - br + ig_r - m_new)
        w_c = jnp.exp(b_last - bc + ig_c - m_new)
        kw = (kth.astype(F32) * w_r).astype(BF16)
        c_s[h] = decay * ch + jnp.dot(kw, vh, preferred_element_type=F32)
        n_new = decay * nrow + jnp.sum(k_ref[0, :, hs].astype(F32) * w_c, axis=0, keepdims=True)
        n_s[h] = jnp.broadcast_to(n_new, (8, HD_B))
        m_s[h] = jnp.broadcast_to(m_new, (8, LANES))

    @pl.when(c == pl.num_programs(1) - 1)
    def _():
        co_ref[0] = c_s[...]
        for h in range(N_B):
            no_ref[0, h:h + 1, :] = n_s[h][0:1, :]
            mo_ref[0, h:h + 1, :] = m_s[h][0:1, :]


def _mlstm(q, k, kt, v, og, sm, gr, g_h, c0, n0, m0, *, lc):
    bsz, s = q.shape[0], q.shape[1]
    tokb = lambda w: pl.BlockSpec((1, lc, w), lambda b, c: (b, c, 0))
    st3 = lambda w: pl.BlockSpec((1, N_B, w), lambda b, c: (b, 0, 0))
    st4 = pl.BlockSpec((1, N_B, HD_B, HD_B), lambda b, c: (b, 0, 0, 0))
    return pl.pallas_call(
        functools.partial(_mlstm_body, lc=lc),
        grid=(bsz, s // lc),
        in_specs=[tokb(D_B), tokb(D_B),
                  pl.BlockSpec((1, D_B, lc), lambda b, c: (b, 0, c)),
                  tokb(D_B), tokb(D_B), tokb(LANES),
                  pl.BlockSpec((1, 2 * N_B, lc), lambda b, c: (b, 0, c)),
                  pl.BlockSpec((1, HD_B), lambda b, c: (0, 0)),
                  st4, st3(HD_B), st3(LANES)],
        out_specs=[tokb(D_B), st4, st3(HD_B), st3(LANES)],
        out_shape=[jax.ShapeDtypeStruct((bsz, s, D_B), BF16),
                   jax.ShapeDtypeStruct((bsz, N_B, HD_B, HD_B), F32),
                   jax.ShapeDtypeStruct((bsz, N_B, HD_B), F32),
                   jax.ShapeDtypeStruct((bsz, N_B, LANES), F32)],
        scratch_shapes=[pltpu.VMEM((N_B, HD_B, HD_B), F32),
                        pltpu.VMEM((N_B, 8, HD_B), F32),
                        pltpu.VMEM((N_B, 8, LANES), F32)],
        compiler_params=_cparams(("parallel", "arbitrary")),
    )(q, k, kt, v, og, sm, gr, g_h, c0, n0, m0)


def _mixer_front(x, ffn1, inp):
    x1 = _ffn(x, *ffn1)
    return (x1,) + tuple(_inproj(x1, *inp))


def kernel(x_prompt, x_sample, cache_k, cache_v, cache_k_idx, state_C, state_n, state_m, g_ffn1, w1_ffn1, w3_ffn1, w2_ffn1, g_mix, w_in, g_q, g_k, g_kidx, rel_bias, b_i, b_f, g_h, w_out, g_ffn2, w1_ffn2, w3_ffn2, w2_ffn2):
    bsz, seq = x_prompt.shape[0], x_prompt.shape[1]
    dbs, dseq = x_sample.shape[0], x_sample.shape[1]
    past = cache_k.shape[1]

    ffn1 = _prep_ffn(g_ffn1, w1_ffn1, w3_ffn1, w2_ffn1)
    ffn2 = _prep_ffn(g_ffn2, w1_ffn2, w3_ffn2, w2_ffn2)
    inp = _prep_inproj(g_mix, w_in, g_q, g_k, g_kidx, b_i, b_f)
    wa = w_out[:D_A].astype(BF16)
    wh = w_out[D_A:].astype(BF16)
    gh = g_h.reshape(1, HD_B)
    bias = _bias_tiles(rel_bias)

    n_p = bsz * seq
    x1, q, kf, k16, vf, v16, qi, qb, kb, vb, og, sm = _mixer_front(x_prompt.reshape(n_p, D_MODEL), ffn1, inp)
    r3 = lambda a, b_, s_: a.reshape(b_, s_, a.shape[-1])
    sm3 = r3(sm, bsz, seq)
    tr = lambda a: jnp.swapaxes(a, 1, 2)
    tq_p = min(SUB, seq)
    attn = tr(_dsa(tr(r3(q, bsz, seq)), tr(r3(qi, bsz, seq)), tr(sm3[:, :, SM_WI:SM_IG]),
                   sm3[:, :, :D_IDX].astype(BF16), r3(k16, bsz, seq), tr(r3(v16, bsz, seq)), bias,
                   tq=tq_p, tk=min(1024, seq), q_off=0, l_valid=seq, topk=min(TOPK_MAX, seq // 4)))
    gr = jnp.swapaxes(sm3[:, :, SM_IG:SM_END], 1, 2)
    kb3 = r3(kb, bsz, seq)
    hb, c_p, n_p_, m_p = _mlstm(r3(qb, bsz, seq), kb3, jnp.swapaxes(kb3, 1, 2), r3(vb, bsz, seq),
                                r3(og, bsz, seq), sm3, gr, gh,
                                jnp.zeros((bsz, N_B, HD_B, HD_B), F32), jnp.zeros((bsz, N_B, HD_B), F32),
                                jnp.zeros((bsz, N_B, LANES), F32), lc=min(SUB, seq))
    x2 = _outproj(x1, attn.reshape(n_p, D_A), hb.reshape(n_p, D_B), wa, wh)
    y_prompt = _ffn(x2, *ffn2).reshape(bsz, seq, D_MODEL)

    n_s = dbs * dseq
    x1s, qs, kfs, k16s, vfs, v16s, qis, qbs, kbs, vbs, ogs, sms = _mixer_front(
        x_sample.reshape(n_s, D_MODEL), ffn1, inp)
    sms3 = r3(sms, dbs, dseq)
    l_all = past + dseq
    tk_s = 3 * SUB
    sk = -(-l_all // tk_s) * tk_s
    padk = lambda a: jnp.pad(a, ((0, 0), (0, sk - l_all), (0, 0)))
    k_all = padk(jnp.concatenate([cache_k.reshape(dbs, past, D_A).astype(BF16), r3(k16s, dbs, dseq)], axis=1))
    v_all = padk(jnp.concatenate([cache_v.reshape(dbs, past, D_A).astype(BF16), r3(v16s, dbs, dseq)], axis=1))
    ki_all = padk(jnp.concatenate([cache_k_idx.astype(BF16), sms3[:, :, :D_IDX].astype(BF16)], axis=1))
    assert dseq <= LANES
    padq = lambda a: jnp.pad(tr(a), ((0, 0), (0, 0), (0, LANES - dseq)))
    attn_s = tr(_dsa(padq(r3(qs, dbs, dseq)), padq(r3(qis, dbs, dseq)), padq(sms3[:, :, SM_WI:SM_IG]),
                     ki_all, k_all, tr(v_all), bias,
                     tq=LANES, tk=tk_s, q_off=past, l_valid=l_all, topk=min(TOPK_MAX, l_all // 4))[:, :, :dseq])
    lc_s = LANES
    padt = lambda a: jnp.pad(a, ((0, 0), (0, lc_s - dseq), (0, 0)))
    lane = jnp.arange(LANES)
    sm_pad = jnp.where((lane >= SM_IG) & (lane < SM_LF), NEG, 0.0).astype(F32)
    sms_p = jnp.concatenate([sms3, jnp.broadcast_to(sm_pad, (dbs, lc_s - dseq, LANES))], axis=1)
    kbs3 = padt(r3(kbs, dbs, dseq))
    hbs, c_s, n_s_, m_s = _mlstm(padt(r3(qbs, dbs, dseq)), kbs3, jnp.swapaxes(kbs3, 1, 2), padt(r3(vbs, dbs, dseq)),
                                 padt(r3(ogs, dbs, dseq)), sms_p, jnp.swapaxes(sms_p[:, :, SM_IG:SM_END], 1, 2), gh,
                                 state_C.astype(F32), state_n.astype(F32),
                                 jnp.broadcast_to(state_m.astype(F32)[:, :, None], (dbs, N_B, LANES)), lc=lc_s)
    x2s = _outproj(x1s, attn_s.reshape(n_s, D_A), hbs[:, :dseq].reshape(n_s, D_B), wa, wh)
    y_sample = _ffn(x2s, *ffn2).reshape(dbs, dseq, D_MODEL)

    sd = state_C.dtype
    return (y_prompt, y_sample,
            kf.reshape(bsz, seq, N_A, HD_A), vf.reshape(bsz, seq, N_A, HD_A), sm3[:, :, :D_IDX],
            c_p.astype(sd), n_p_.astype(sd), m_p[:, :, 0].astype(sd),
            kfs.reshape(dbs, dseq, N_A, HD_A), vfs.reshape(dbs, dseq, N_A, HD_A), sms3[:, :, :D_IDX],
            c_s.astype(sd), n_s_.astype(sd), m_s[:, :, 0].astype(sd))
```

```python
import functools
import math

import numpy as np
import jax
import jax.numpy as jnp
from jax import lax
from jax.experimental import pallas as pl
from jax.experimental.pallas import tpu as pltpu

F32 = jnp.float32
BF16 = jnp.bfloat16
I32 = jnp.int32

D_MODEL = 2048
CHUNK = 64
N_A, HD_A = 8, 128
H_IDX, D_IDX = 16, 64
TOPK_MAX = 256
N_B, HD_B = 4, 256
D_FF = 5504
NUM_BUCKETS, MAX_DISTANCE = 32, 128
EPS = 1e-6
NEG = -1e30
INT_MIN = -(2 ** 31)
LOG2E = math.log2(math.e)

LANES = 128
D_A = N_A * HD_A
D_B = N_B * HD_B
D_QI = H_IDX * D_IDX
FF_TILE = 512
D_FF_PAD = -(-D_FF // FF_TILE) * FF_TILE
FFN_TOK_TILE = 512
PROJ_TOK_TILE = 256
TOK_TILE = 512
SUB = 256
VMEM_LIMIT = 56 * 1024 * 1024

SM_KI, SM_WI, SM_IG, SM_LF = 0, D_IDX, D_IDX + H_IDX, D_IDX + H_IDX + N_B
SM_END = SM_LF + N_B


def _cparams(sem):
    return pltpu.CompilerParams(dimension_semantics=sem, vmem_limit_bytes=VMEM_LIMIT)


def _rms(x, g):
    ms = jnp.mean(x * x, axis=-1, keepdims=True)
    return x * lax.rsqrt(ms + EPS) * g


def _ffn_body(x_ref, g_ref, w1_ref, w3_ref, w2_ref, o_ref, hn_ref):
    j = pl.program_id(1)

    @pl.when(j == 0)
    def _():
        hn_ref[...] = _rms(x_ref[...], g_ref[...]).astype(BF16)
        o_ref[...] = jnp.zeros(o_ref.shape, F32)

    h = hn_ref[...]
    a = jnp.dot(h, w1_ref[...], preferred_element_type=F32)
    b = jnp.dot(h, w3_ref[...], preferred_element_type=F32)
    u = (a * jax.nn.sigmoid(a) * b).astype(BF16)
    o_ref[...] += jnp.dot(u, w2_ref[...], preferred_element_type=F32)

    @pl.when(j == pl.num_programs(1) - 1)
    def _():
        o_ref[...] = x_ref[...] + 0.5 * o_ref[...]


def _ffn(x, g, w1, w3, w2):
    n = x.shape[0]
    tm = min(FFN_TOK_TILE, n)
    grid = (n // tm, D_FF_PAD // FF_TILE)
    return pl.pallas_call(
        _ffn_body,
        grid=grid,
        in_specs=[
            pl.BlockSpec((tm, D_MODEL), lambda i, j: (i, 0)),
            pl.BlockSpec((1, D_MODEL), lambda i, j: (0, 0)),
            pl.BlockSpec((D_MODEL, FF_TILE), lambda i, j: (0, j)),
            pl.BlockSpec((D_MODEL, FF_TILE), lambda i, j: (0, j)),
            pl.BlockSpec((FF_TILE, D_MODEL), lambda i, j: (j, 0)),
        ],
        out_specs=pl.BlockSpec((tm, D_MODEL), lambda i, j: (i, 0)),
        out_shape=jax.ShapeDtypeStruct((n, D_MODEL), F32),
        scratch_shapes=[pltpu.VMEM((tm, D_MODEL), BF16)],
        compiler_params=_cparams(("parallel", "arbitrary")),
    )(x, g, w1, w3, w2)


def _cast_pad_body(x_ref, o_ref, *, blocks_valid, cols_valid):
    x = jnp.where(pl.program_id(0) < blocks_valid, x_ref[...], 0.0)
    o_ref[:, :cols_valid] = x.astype(BF16)
    if cols_valid < o_ref.shape[1]:
        o_ref[:, cols_valid:] = jnp.zeros((o_ref.shape[0], o_ref.shape[1] - cols_valid), BF16)


def _cast_pad(w, rows, cols, block_rows):
    r, c = w.shape
    assert rows % block_rows == 0 and r % block_rows == 0 and c % LANES == 0 and cols >= c and rows >= r
    nvalid = r // block_rows
    body = functools.partial(_cast_pad_body, blocks_valid=nvalid, cols_valid=c)
    return pl.pallas_call(
        body,
        grid=(rows // block_rows,),
        in_specs=[pl.BlockSpec((block_rows, c), lambda i: (jnp.minimum(i, nvalid - 1), 0))],
        out_specs=pl.BlockSpec((block_rows, cols), lambda i: (i, 0)),
        out_shape=jax.ShapeDtypeStruct((rows, cols), BF16),
        compiler_params=_cparams(("parallel",)),
    )(w)


def _prep_ffn(g, w1, w3, w2):
    return (g.reshape(1, D_MODEL),
            _cast_pad(w1, D_MODEL, D_FF_PAD, 256),
            _cast_pad(w3, D_MODEL, D_FF_PAD, 256),
            _cast_pad(w2, D_FF_PAD, D_MODEL, LANES))


N_MAIN_GROUPS = 8


def _head_norm(p, g, heads, hd):
    outs = []
    for h in range(heads):
        outs.append(_rms(p[:, h * hd:(h + 1) * hd], g))
    return outs


def _inproj_a_body(x_ref, g_ref, w_ref, wsh_ref, wsl_ref, gq_ref, gk_ref, sg_ref, sb_ref,
                   q_ref, kf_ref, k16_ref, vf_ref, v16_ref, qi_ref, sm_ref):
    h32 = _rms(x_ref[...], g_ref[...])
    h = h32.astype(BF16)
    col = lambda c: jnp.dot(h, w_ref[:, c * D_A:(c + 1) * D_A], preferred_element_type=F32)

    h_lo = (h32 - h.astype(F32)).astype(BF16)
    ps = (jnp.dot(h, wsh_ref[...], preferred_element_type=F32)
          + jnp.dot(h_lo, wsh_ref[...], preferred_element_type=F32)
          + jnp.dot(h, wsl_ref[...], preferred_element_type=F32))
    lane = lax.broadcasted_iota(I32, ps.shape, 1)
    ms = jnp.sum(jnp.where(lane < SM_WI, ps * ps, 0.0), axis=-1, keepdims=True) * (1.0 / D_IDX)
    kin = ps * lax.rsqrt(ms + EPS) * sg_ref[...]
    z = ps + sb_ref[...]
    ls = jnp.minimum(z, 0.0) - jnp.log1p(jnp.exp(-jnp.abs(z)))
    sm_ref[...] = jnp.where(lane < SM_WI, kin,
                            jnp.where(lane < SM_IG, ps,
                                      jnp.where(lane < SM_LF, z,
                                                jnp.where(lane < SM_END, ls, 0.0))))

    for hd, qh in enumerate(_head_norm(col(0), gq_ref[...], N_A, HD_A)):
        q_ref[:, hd * HD_A:(hd + 1) * HD_A] = (qh * (HD_A ** -0.5 * LOG2E)).astype(BF16)
    for hd, kh in enumerate(_head_norm(col(1), gk_ref[...], N_A, HD_A)):
        kf_ref[:, hd * HD_A:(hd + 1) * HD_A] = kh
        k16_ref[:, hd * HD_A:(hd + 1) * HD_A] = kh.astype(BF16)
    p = col(2)
    vf_ref[...] = p
    v16_ref[...] = p.astype(BF16)
    qi_ref[...] = col(3).astype(BF16)


def _inproj_b_body(x_ref, g_ref, w_ref, qb_ref, kb_ref, vb_ref, og_ref):
    h = _rms(x_ref[...], g_ref[...]).astype(BF16)
    col = lambda c: jnp.dot(h, w_ref[:, c * D_B:(c + 1) * D_B], preferred_element_type=F32)
    qb_ref[...] = col(0).astype(BF16)
    kb_ref[...] = (col(1) * HD_B ** -0.5).astype(BF16)
    vb_ref[...] = col(2).astype(BF16)
    og_ref[...] = jax.nn.sigmoid(col(3)).astype(BF16)


def _inproj(x, g, wa, wb, wsh, wsl, gq, gk, sg, sb):
    n = x.shape[0]
    tm = min(PROJ_TOK_TILE, n)
    tok = lambda w: pl.BlockSpec((tm, w), lambda i: (i, 0))
    const = lambda a: pl.BlockSpec(a.shape, lambda i: (0,) * a.ndim)
    wide = lambda dt: jax.ShapeDtypeStruct((n, D_A), dt)
    outs_a = pl.pallas_call(
        _inproj_a_body,
        grid=(n // tm,),
        in_specs=[tok(D_MODEL)] + [const(a) for a in (g, wa, wsh, wsl, gq, gk, sg, sb)],
        out_specs=[tok(D_A)] * 6 + [tok(LANES)],
        out_shape=[wide(BF16), wide(F32), wide(BF16), wide(F32), wide(BF16), wide(BF16),
                   jax.ShapeDtypeStruct((n, LANES), F32)],
        compiler_params=_cparams(("parallel",)),
    )(x, g, wa, wsh, wsl, gq, gk, sg, sb)
    outs_b = pl.pallas_call(
        _inproj_b_body,
        grid=(n // tm,),
        in_specs=[tok(D_MODEL), const(g), const(wb)],
        out_specs=[tok(D_B)] * 4,
        out_shape=[wide(BF16)] * 4,
        compiler_params=_cparams(("parallel",)),
    )(x, g, wb)
    q, kf, k16, vf, v16, qi, sm = outs_a
    return (q, kf, k16, vf, v16, qi) + tuple(outs_b) + (sm,)


def _prep_inproj(g_mix, w_in, g_q, g_k, g_kidx, b_i, b_f):
    o_ki = 4 * D_A
    o_qb = o_ki + D_IDX + H_IDX
    o_ib = o_qb + 4 * D_B
    wa = w_in[:, :o_ki].astype(BF16)
    wb = w_in[:, o_qb:o_ib].astype(BF16)
    ws = jnp.concatenate([w_in[:, o_ki:o_qb], w_in[:, o_ib:],
                          jnp.zeros((D_MODEL, LANES - SM_END), F32)], axis=1)
    wsh = ws.astype(BF16)
    wsl = (ws - wsh.astype(F32)).astype(BF16)
    sg = jnp.concatenate([g_kidx, jnp.ones((LANES - D_IDX,), F32)]).reshape(1, LANES)
    sb = jnp.concatenate([jnp.zeros((SM_IG,), F32), b_i, b_f,
                          jnp.zeros((LANES - SM_END,), F32)]).reshape(1, LANES)
    return (g_mix.reshape(1, D_MODEL), wa, wb, wsh, wsl,
            g_q.reshape(1, HD_A), g_k.reshape(1, HD_A), sg, sb)


def _outproj_body(x_ref, a_ref, h_ref, wa_ref, wh_ref, o_ref):
    o_ref[...] = (x_ref[...]
                  + jnp.dot(a_ref[...], wa_ref[...], preferred_element_type=F32)
                  + jnp.dot(h_ref[...], wh_ref[...], preferred_element_type=F32))


def _outproj(x, a, h, wa, wh):
    n = x.shape[0]
    tm = min(TOK_TILE, n)
    return pl.pallas_call(
        _outproj_body,
        grid=(n // tm,),
        in_specs=[pl.BlockSpec((tm, D_MODEL), lambda i: (i, 0)),
                  pl.BlockSpec((tm, D_A), lambda i: (i, 0)),
                  pl.BlockSpec((tm, D_B), lambda i: (i, 0)),
                  pl.BlockSpec((D_A, D_MODEL), lambda i: (0, 0)),
                  pl.BlockSpec((D_B, D_MODEL), lambda i: (0, 0))],
        out_specs=pl.BlockSpec((tm, D_MODEL), lambda i: (i, 0)),
        out_shape=jax.ShapeDtypeStruct((n, D_MODEL), F32),
        compiler_params=_cparams(("parallel",)),
    )(x, a, h, wa, wh)


def _bucket_thresholds():
    nb = NUM_BUCKETS // 2
    max_exact = nb // 2
    span = nb - max_exact
    ratio = MAX_DISTANCE // max_exact
    out = []
    for m in range(1, span):
        n = max_exact
        while n ** span < max_exact ** span * ratio ** m:
            n += 1
        out.append(n)
    return tuple(out)


def _bias_body(tbl_ref, o_ref):
    o = pl.program_id(0)
    h = pl.program_id(1)
    nb = NUM_BUCKETS // 2
    max_exact = nb // 2
    row = lax.broadcasted_iota(I32, (SUB, SUB), 0)
    col = lax.broadcasted_iota(I32, (SUB, SUB), 1)
    rel = row - o * SUB - col
    n = jnp.abs(rel)
    large = jnp.full((SUB, SUB), max_exact, I32)
    for t in _bucket_thresholds():
        large = large + (n >= t).astype(I32)
    bucket = jnp.where(rel > 0, nb, 0) + jnp.where(n < max_exact, n, large)
    val = jnp.zeros((SUB, SUB), F32)
    for bk in range(NUM_BUCKETS):
        val = jnp.where(bucket == bk, tbl_ref[bk, h], val)
    o_ref[0, 0] = (val - tbl_ref[nb - 1, h]) * LOG2E


def _bias_tiles(rel_bias):
    return pl.pallas_call(
        _bias_body,
        grid=(2, N_A),
        in_specs=[pl.BlockSpec(memory_space=pltpu.SMEM)],
        out_specs=pl.BlockSpec((1, 1, SUB, SUB), lambda o, h: (o, h, 0, 0)),
        out_shape=jax.ShapeDtypeStruct((2, N_A, SUB, SUB), F32),
    )(rel_bias)


WCH = 4
A_UNROLL = 4
FAR_SUBS = 2


def _bit(k):
    return INT_MIN if k == 31 else 1 << k


def _bit_transpose32(a):
    a = list(a)
    j, m = 16, 0x0000FFFF
    while j:
        for k in range(32):
            if not k & j:
                t = (lax.shift_right_logical(a[k], jnp.int32(j)) ^ a[k + j]) & m
                a[k + j] = a[k + j] ^ t
                a[k] = a[k] ^ jnp.left_shift(t, jnp.int32(j))
        j >>= 1
        m ^= m << j
    return a


def _dsa_body(ti_ref, tk_ref, tn_ref, tl_ref,
              qt_ref, qit_ref, wit_ref, ki_ref, k_ref, vt_ref, bias_ref, o_ref,
              plane_ref, adm_ref, cand_ref, sel_ref, mb_ref, m_ref, l_ref, acc_ref, lg_ref, al_ref,
              *, tq, tk, q_off, l_valid, topk):
    s = pl.program_id(1)
    i = ti_ref[s]
    kt = tk_ref[s]
    nsub = tn_ref[s]
    subs = tk // SUB
    q_pos0 = q_off + i * tq
    i_sub = q_off // SUB + (i * tq) // SUB

    def fold(x, op, rows=8):
        parts = [x[r:r + rows] for r in range(0, x.shape[0], rows)]
        while len(parts) > 1:
            parts = [op(parts[a], parts[a + 1]) for a in range(0, len(parts), 2)]
        return parts[0]

    @pl.when(kt == 0)
    def _():
        m_ref[...] = jnp.full(m_ref.shape, NEG, F32)
        l_ref[...] = jnp.zeros(l_ref.shape, F32)
        acc_ref[...] = jnp.zeros(acc_ref.shape, F32)

        qchunk = (q_pos0 + lax.broadcasted_iota(I32, (SUB, tq), 1)) >> 6
        krow = lax.broadcasted_iota(I32, (SUB, tq), 0)

        def score_codes(t):
            off = pl.multiple_of(t * SUB, SUB)
            kit = ki_ref[0, pl.ds(off, SUB), :]
            sc = jnp.zeros((SUB, tq), F32)
            for j in range(H_IDX):
                sj = jnp.dot(kit, qit_ref[0, j * D_IDX:(j + 1) * D_IDX, :], preferred_element_type=F32)
                sc = sc + jnp.maximum(sj, 0.0) * wit_ref[0, j:j + 1, :]
            bits = lax.bitcast_convert_type(sc, I32)
            return bits ^ ((bits >> 31) | INT_MIN)

        def slice_codes(t, code):
            kpos = t * SUB + krow
            adm = ((kpos >> 6) <= qchunk) & (kpos < l_valid)
            planes = _bit_transpose32([code[8 * k:8 * k + 8] for k in range(32)])
            row = pl.multiple_of(t * 8, 8)
            aw = jnp.zeros((8, tq), I32)
            for k in range(32):
                plane_ref[k, pl.ds(row, 8), :] = planes[k]
                aw = aw | jnp.where(adm[8 * k:8 * k + 8], _bit(k), 0)
            adm_ref[pl.ds(row, 8), :] = aw

        def score_tiles(gi, c):
            for u in range(A_UNROLL):
                slice_codes(gi * A_UNROLL + u, score_codes(gi * A_UNROLL + u))
            return c

        def score_tile(t, c):
            slice_codes(t, score_codes(t))
            return c

        lax.fori_loop(0, nsub // A_UNROLL, score_tiles, 0)
        lax.fori_loop(nsub // A_UNROLL * A_UNROLL, nsub, score_tile, 0)

        nchunk = (nsub + WCH - 1) // WCH

        def clear_tile(t, c):
            row = pl.multiple_of(t * 8, 8)
            plane_ref[:, pl.ds(row, 8), :] = jnp.zeros((32, 8, tq), I32)
            adm_ref[pl.ds(row, 8), :] = jnp.zeros((8, tq), I32)
            return c

        lax.fori_loop(nsub, nchunk * WCH, clear_tile, 0)

        def chunk_rows(cn):
            return pl.ds(pl.multiple_of(cn * (8 * WCH), 8 * WCH), 8 * WCH)

        def init_chunk(cn, c):
            cand_ref[chunk_rows(cn), :] = adm_ref[chunk_rows(cn), :]
            sel_ref[chunk_rows(cn), :] = jnp.zeros((8 * WCH, tq), I32)
            return c

        lax.fori_loop(0, nchunk, init_chunk, 0)

        def bit_step(it, n_sel):
            b = 31 - it

            def count_chunk(cn, cnt):
                ones = cand_ref[chunk_rows(cn), :] & plane_ref[b, chunk_rows(cn), :]
                return cnt + lax.population_count(ones)

            cnt = lax.fori_loop(0, nchunk, count_chunk, jnp.zeros((8 * WCH, tq), I32))
            n_one = jnp.sum(fold(cnt.astype(F32), jnp.add), axis=0, keepdims=True).astype(I32)
            take = (n_sel + n_one) >= topk

            def update_chunk(cn, c):
                cw = cand_ref[chunk_rows(cn), :]
                pw = plane_ref[b, chunk_rows(cn), :]
                ones = cw & pw
                cand_ref[chunk_rows(cn), :] = jnp.where(take, ones, cw & ~pw)
                sel_ref[chunk_rows(cn), :] = sel_ref[chunk_rows(cn), :] | jnp.where(take, 0, ones)
                return c

            lax.fori_loop(0, nchunk, update_chunk, 0)
            return jnp.where(take, n_sel, n_sel + n_one)

        lax.fori_loop(0, 32, bit_step, jnp.zeros((1, tq), I32))

        def final_chunk(cn, c):
            sel_ref[chunk_rows(cn), :] = sel_ref[chunk_rows(cn), :] | cand_ref[chunk_rows(cn), :]
            return c

        lax.fori_loop(0, nchunk, final_chunk, 0)

    n_here = jnp.clip(nsub - kt * subs, 0, subs)

    def key_block(u, nsubs, near):
        nk = nsubs * SUB
        g = kt * subs + u
        dsub = g - i_sub
        uoff = pl.multiple_of(u * SUB, SUB)
        for a in range(nsubs):
            sw = sel_ref[pl.ds(pl.multiple_of((g + a) * 8, 8), 8), :]
            for k in range(32):
                r = a * SUB + 8 * k
                mb_ref[r:r + 8, :] = jnp.where((sw & _bit(k)) != 0, 0.0, 2.0 * NEG)
        for h in range(N_A):
            hs = slice(h * HD_A, (h + 1) * HD_A)
            kh = k_ref[0, pl.ds(uoff, nk), hs]
            lg = jnp.dot(kh, qt_ref[0, hs, :], preferred_element_type=F32)
            if near:
                lg = lg + bias_ref[-dsub, h, :, 0:tq]
            lg = lg + mb_ref[0:nk, :]
            lg_ref[h, 0:nk, :] = lg
            m_old = m_ref[h]
            m_new = jnp.maximum(m_old, jnp.max(fold(lg, jnp.maximum), axis=0, keepdims=True))
            al_ref[h] = jnp.exp2(m_old - m_new)
            m_ref[h] = m_new
        for h in range(N_A):
            hs = slice(h * HD_A, (h + 1) * HD_A)
            vth = vt_ref[0, hs, pl.ds(uoff, nk)]
            alpha = al_ref[h]
            p = jnp.exp2((lg_ref[h, 0:nk, :] - m_ref[h][0:1, :]).astype(BF16))
            pv = jnp.dot(jnp.concatenate([vth, jnp.ones((16, nk), BF16)], axis=0), p,
                         preferred_element_type=F32)
            l_ref[h] = alpha * l_ref[h] + pv[HD_A:HD_A + 8, :]
            acc_ref[hs, :] = alpha[0:1, :] * acc_ref[hs, :] + pv[0:HD_A, :]

    n_far = jnp.clip(i_sub - 1 - kt * subs, 0, n_here)

    def far_block(bi, c):
        key_block(bi * FAR_SUBS, FAR_SUBS, False)
        return c

    def far_one(u, c):
        key_block(u, 1, False)
        return c

    def near_one(u, c):
        key_block(u, 1, True)
        return c

    lax.fori_loop(0, n_far // FAR_SUBS, far_block, 0)
    lax.fori_loop(n_far // FAR_SUBS * FAR_SUBS, n_far, far_one, 0)
    lax.fori_loop(n_far, n_here, near_one, 0)

    @pl.when(tl_ref[s] == 1)
    def _():
        for h in range(N_A):
            hs = slice(h * HD_A, (h + 1) * HD_A)
            o_ref[0, hs, :] = (acc_ref[hs, :] / l_ref[h][0:1, :]).astype(BF16)


def _dsa(qt, qit, wit, ki, k, vt, bias, *, tq, tk, q_off, l_valid, topk):
    bsz, sq = qt.shape[0], qt.shape[2]
    sk = k.shape[1]
    assert sq % tq == 0 and sk % tk == 0 and tk % SUB == 0 and q_off % SUB == 0
    assert (tq % SUB == 0 or sq == tq) and tq % LANES == 0
    subs = tk // SUB
    wrows = -(-(sk // SUB) // WCH) * WCH * 8
    ti, tkk, tn, tl = [], [], [], []
    for i in range(sq // tq):
        q_last = q_off + (i + 1) * tq - 1
        lim = min(l_valid, (q_last // CHUNK + 1) * CHUNK)
        nsub = -(-lim // SUB)
        nkt = -(-nsub // subs)
        for t in range(nkt):
            ti.append(i), tkk.append(t), tn.append(nsub), tl.append(int(t == nkt - 1))
    tabs = [jnp.asarray(np.asarray(a, np.int32)) for a in (ti, tkk, tn, tl)]
    body = functools.partial(_dsa_body, tq=tq, tk=tk, q_off=q_off, l_valid=l_valid, topk=topk)
    qmap = lambda b, s, ti, tk_, tn, tl: (b, 0, ti[s])
    grid_spec = pltpu.PrefetchScalarGridSpec(
        num_scalar_prefetch=4,
        grid=(bsz, len(ti)),
        in_specs=[pl.BlockSpec((1, D_A, tq), qmap),
                  pl.BlockSpec((1, D_QI, tq), qmap),
                  pl.BlockSpec((1, H_IDX, tq), qmap),
                  pl.BlockSpec((1, sk, D_IDX), lambda b, s, *_: (b, 0, 0)),
                  pl.BlockSpec((1, tk, D_A), lambda b, s, ti, tk_, tn, tl: (b, tk_[s], 0)),
                  pl.BlockSpec((1, D_A, tk), lambda b, s, ti, tk_, tn, tl: (b, 0, tk_[s])),
                  pl.BlockSpec((2, N_A, SUB, SUB), lambda b, s, *_: (0, 0, 0, 0))],
        out_specs=pl.BlockSpec((1, D_A, tq), qmap),
        scratch_shapes=[pltpu.VMEM((32, wrows, tq), I32),
                        pltpu.VMEM((wrows, tq), I32),
                        pltpu.VMEM((wrows, tq), I32),
                        pltpu.VMEM((wrows, tq), I32),
                        pltpu.VMEM((FAR_SUBS * SUB, tq), F32),
                        pltpu.VMEM((N_A, 8, tq), F32),
                        pltpu.VMEM((N_A, 8, tq), F32),
                        pltpu.VMEM((D_A, tq), F32),
                        pltpu.VMEM((N_A, FAR_SUBS * SUB, tq), F32),
                        pltpu.VMEM((N_A, 8, tq), F32)])
    return pl.pallas_call(
        body,
        grid_spec=grid_spec,
        out_shape=jax.ShapeDtypeStruct((bsz, D_A, sq), BF16),
        compiler_params=_cparams(("parallel", "arbitrary")),
    )(*tabs, qt, qit, wit, ki, k, vt, bias)


def _pack_cache_body(k4_ref, v4_ref, kn_ref, vnt_ref, ko_ref, vto_ref, *, ncache):
    c = pl.program_id(1)

    @pl.when(c < ncache)
    def _():
        for h in range(N_A):
            hs = slice(h * HD_A, (h + 1) * HD_A)
            ko_ref[0, :, hs] = k4_ref[0, pl.ds(h, SUB, stride=N_A), :].astype(BF16)
            vto_ref[0, hs, :] = v4_ref[0, pl.ds(h, SUB, stride=N_A), :].T.astype(BF16)

    @pl.when(c == ncache)
    def _():
        ko_ref[0] = kn_ref[0]
        vto_ref[0] = vnt_ref[0]


def _pack_cache(cache_k, cache_v, k_new, vt_new):
    bsz, past = cache_k.shape[0], cache_k.shape[1]
    ncache = past // SUB
    cmap = lambda b, c: (b, jnp.minimum(c, ncache - 1), 0)
    rows = lambda a: a.reshape(bsz, past * N_A, HD_A)
    return pl.pallas_call(
        functools.partial(_pack_cache_body, ncache=ncache),
        grid=(bsz, ncache + 1),
        in_specs=[pl.BlockSpec((1, SUB * N_A, HD_A), cmap),
                  pl.BlockSpec((1, SUB * N_A, HD_A), cmap),
                  pl.BlockSpec((1, SUB, D_A), lambda b, c: (b, 0, 0)),
                  pl.BlockSpec((1, D_A, SUB), lambda b, c: (b, 0, 0))],
        out_specs=[pl.BlockSpec((1, SUB, D_A), lambda b, c: (b, c, 0)),
                   pl.BlockSpec((1, D_A, SUB), lambda b, c: (b, 0, c))],
        out_shape=[jax.ShapeDtypeStruct((bsz, past + SUB, D_A), BF16),
                   jax.ShapeDtypeStruct((bsz, D_A, past + SUB), BF16)],
        compiler_params=_cparams(("parallel", "arbitrary")),
    )(rows(cache_k), rows(cache_v), k_new, vt_new)


def _mlstm_body(q_ref, k_ref, kt_ref, v_ref, og_ref, sm_ref, gr_ref, gh_ref, c0_ref, n0_ref, m0_ref,
                hb_ref, co_ref, no_ref, mo_ref, c_s, n_s, m_s, *, lc):
    c = pl.program_id(1)

    @pl.when(c == 0)
    def _():
        c_s[...] = c0_ref[0]
        for h in range(N_B):
            n_s[h] = jnp.broadcast_to(n0_ref[0, h:h + 1, :], (8, HD_B))
            m_s[h] = jnp.broadcast_to(m0_ref[0, h:h + 1, :], (8, LANES))

    row = lax.broadcasted_iota(I32, (lc, lc), 0)
    col = lax.broadcasted_iota(I32, (lc, lc), 1)
    causal = col <= row
    sm = sm_ref[0]
    gr = gr_ref[0]
    hp = lax.Precision.HIGHEST
    b_cols = jnp.dot(causal.astype(F32), sm, precision=hp, preferred_element_type=F32)
    b_rows = jnp.dot(gr, (row <= col).astype(F32), precision=hp, preferred_element_type=F32)

    for h in range(N_B):
        hs = slice(h * HD_B, (h + 1) * HD_B)
        m_prev = m_s[h][0:1, 0:1]
        bc = b_cols[:, SM_LF + h:SM_LF + h + 1]
        ig_c = sm[:, SM_IG + h:SM_IG + h + 1]
        br = b_rows[N_B + h:N_B + h + 1, :]
        ig_r = gr[h:h + 1, :]
        log_d = jnp.where(causal, bc - br + ig_r, NEG)
        log_inter = bc + m_prev
        m_t = jnp.maximum(log_inter, jnp.max(log_d, axis=1, keepdims=True))
        d = jnp.exp(log_d - m_t)
        inter = jnp.exp(log_inter - m_t)
        qh = q_ref[0, :, hs]
        kth = kt_ref[0, hs, :]
        vh = v_ref[0, :, hs]
        sc = jnp.dot(qh, kth, preferred_element_type=F32) * d
        ch = c_s[h]
        nrow = n_s[h][0:1, :]
        num = (jnp.dot(sc.astype(BF16), vh, preferred_element_type=F32)
               + inter * jnp.dot(qh, ch.astype(BF16), preferred_element_type=F32))
        den = (jnp.sum(sc, axis=1, keepdims=True)
               + inter * jnp.sum(qh.astype(F32) * nrow, axis=1, keepdims=True))
        hh = num / jnp.maximum(jnp.abs(den), jnp.exp(-m_t))
        hb = _rms(hh, gh_ref[...]) * og_ref[0, :, hs].astype(F32)
        hb_ref[0, :, hs] = hb.astype(BF16)

        m_new = m_t[lc - 1:lc, :]
        b_last = bc[lc - 1:lc, :]
        decay = jnp.exp(b_last + m_prev - m_new)
        w_r = jnp.exp(b_last - br + ig_r - m_new)
        w_c = jnp.exp(b_last - bc + ig_c - m_new)
        kw = (kth.astype(F32) * w_r).astype(BF16)
        c_s[h] = decay * ch + jnp.dot(kw, vh, preferred_element_type=F32)
        n_new = decay * nrow + jnp.sum(k_ref[0, :, hs].astype(F32) * w_c, axis=0, keepdims=True)
        n_s[h] = jnp.broadcast_to(n_new, (8, HD_B))
        m_s[h] = jnp.broadcast_to(m_new, (8, LANES))

    @pl.when(c == pl.num_programs(1) - 1)
    def _():
        co_ref[0] = c_s[...]
        for h in range(N_B):
            no_ref[0, h:h + 1, :] = n_s[h][0:1, :]
            mo_ref[0, h:h + 1, :] = m_s[h][0:1, :]


def _mlstm(q, k, kt, v, og, sm, gr, g_h, c0, n0, m0, *, lc):
    bsz, s = q.shape[0], q.shape[1]
    tokb = lambda w: pl.BlockSpec((1, lc, w), lambda b, c: (b, c, 0))
    st3 = lambda w: pl.BlockSpec((1, N_B, w), lambda b, c: (b, 0, 0))
    st4 = pl.BlockSpec((1, N_B, HD_B, HD_B), lambda b, c: (b, 0, 0, 0))
    return pl.pallas_call(
        functools.partial(_mlstm_body, lc=lc),
        grid=(bsz, s // lc),
        in_specs=[tokb(D_B), tokb(D_B),
                  pl.BlockSpec((1, D_B, lc), lambda b, c: (b, 0, c)),
                  tokb(D_B), tokb(D_B), tokb(LANES),
                  pl.BlockSpec((1, 2 * N_B, lc), lambda b, c: (b, 0, c)),
                  pl.BlockSpec((1, HD_B), lambda b, c: (0, 0)),
                  st4, st3(HD_B), st3(LANES)],
        out_specs=[tokb(D_B), st4, st3(HD_B), st3(LANES)],
        out_shape=[jax.ShapeDtypeStruct((bsz, s, D_B), BF16),
                   jax.ShapeDtypeStruct((bsz, N_B, HD_B, HD_B), F32),
                   jax.ShapeDtypeStruct((bsz, N_B, HD_B), F32),
                   jax.ShapeDtypeStruct((bsz, N_B, LANES), F32)],
        scratch_shapes=[pltpu.VMEM((N_B, HD_B, HD_B), F32),
                        pltpu.VMEM((N_B, 8, HD_B), F32),
                        pltpu.VMEM((N_B, 8, LANES), F32)],
        compiler_params=_cparams(("parallel", "arbitrary")),
    )(q, k, kt, v, og, sm, gr, g_h, c0, n0, m0)


def _mixer_front(x, ffn1, inp):
    x1 = _ffn(x, *ffn1)
    return (x1,) + tuple(_inproj(x1, *inp))


def kernel(x_prompt, x_sample, cache_k, cache_v, cache_k_idx, state_C, state_n, state_m, g_ffn1, w1_ffn1, w3_ffn1, w2_ffn1, g_mix, w_in, g_q, g_k, g_kidx, rel_bias, b_i, b_f, g_h, w_out, g_ffn2, w1_ffn2, w3_ffn2, w2_ffn2):
    bsz, seq = x_prompt.shape[0], x_prompt.shape[1]
    dbs, dseq = x_sample.shape[0], x_sample.shape[1]
    past = cache_k.shape[1]

    ffn1 = _prep_ffn(g_ffn1, w1_ffn1, w3_ffn1, w2_ffn1)
    ffn2 = _prep_ffn(g_ffn2, w1_ffn2, w3_ffn2, w2_ffn2)
    inp = _prep_inproj(g_mix, w_in, g_q, g_k, g_kidx, b_i, b_f)
    wa = w_out[:D_A].astype(BF16)
    wh = w_out[D_A:].astype(BF16)
    gh = g_h.reshape(1, HD_B)
    bias = _bias_tiles(rel_bias)

    n_p = bsz * seq
    x1, q, kf, k16, vf, v16, qi, qb, kb, vb, og, sm = _mixer_front(x_prompt.reshape(n_p, D_MODEL), ffn1, inp)
    r3 = lambda a, b_, s_: a.reshape(b_, s_, a.shape[-1])
    sm3 = r3(sm, bsz, seq)
    tr = lambda a: jnp.swapaxes(a, 1, 2)
    tq_p = min(SUB, seq)
    attn = tr(_dsa(tr(r3(q, bsz, seq)), tr(r3(qi, bsz, seq)), tr(sm3[:, :, SM_WI:SM_IG]),
                   sm3[:, :, :D_IDX].astype(BF16), r3(k16, bsz, seq), tr(r3(v16, bsz, seq)), bias,
                   tq=tq_p, tk=min(1024, seq), q_off=0, l_valid=seq, topk=min(TOPK_MAX, seq // 4)))
    gr = jnp.swapaxes(sm3[:, :, SM_IG:SM_END], 1, 2)
    kb3 = r3(kb, bsz, seq)
    hb, c_p, n_p_, m_p = _mlstm(r3(qb, bsz, seq), kb3, jnp.swapaxes(kb3, 1, 2), r3(vb, bsz, seq),
                                r3(og, bsz, seq), sm3, gr, gh,
                                jnp.zeros((bsz, N_B, HD_B, HD_B), F32), jnp.zeros((bsz, N_B, HD_B), F32),
                                jnp.zeros((bsz, N_B, LANES), F32), lc=min(SUB, seq))
    x2 = _outproj(x1, attn.reshape(n_p, D_A), hb.reshape(n_p, D_B), wa, wh)
    y_prompt = _ffn(x2, *ffn2).reshape(bsz, seq, D_MODEL)

    n_s = dbs * dseq
    x1s, qs, kfs, k16s, vfs, v16s, qis, qbs, kbs, vbs, ogs, sms = _mixer_front(
        x_sample.reshape(n_s, D_MODEL), ffn1, inp)
    sms3 = r3(sms, dbs, dseq)
    l_all = past + dseq
    tk_s = 3 * SUB
    sk = -(-l_all // tk_s) * tk_s
    padk = lambda a: jnp.pad(a, ((0, 0), (0, sk - l_all), (0, 0)))
    assert past % SUB == 0 and sk - past == SUB
    padn = lambda a: jnp.pad(a, ((0, 0), (0, SUB - dseq), (0, 0)))
    k_all, vt_all = _pack_cache(cache_k, cache_v, padn(r3(k16s, dbs, dseq)), tr(padn(r3(v16s, dbs, dseq))))
    ki_all = padk(jnp.concatenate([cache_k_idx.astype(BF16), sms3[:, :, :D_IDX].astype(BF16)], axis=1))
    assert dseq <= LANES
    padq = lambda a: jnp.pad(tr(a), ((0, 0), (0, 0), (0, LANES - dseq)))
    attn_s = tr(_dsa(padq(r3(qs, dbs, dseq)), padq(r3(qis, dbs, dseq)), padq(sms3[:, :, SM_WI:SM_IG]),
                     ki_all, k_all, vt_all, bias,
                     tq=LANES, tk=tk_s, q_off=past, l_valid=l_all, topk=min(TOPK_MAX, l_all // 4))[:, :, :dseq])
    lc_s = LANES
    padt = lambda a: jnp.pad(a, ((0, 0), (0, lc_s - dseq), (0, 0)))
    lane = jnp.arange(LANES)
    sm_pad = jnp.where((lane >= SM_IG) & (lane < SM_LF), NEG, 0.0).astype(F32)
    sms_p = jnp.concatenate([sms3, jnp.broadcast_to(sm_pad, (dbs, lc_s - dseq, LANES))], axis=1)
    kbs3 = padt(r3(kbs, dbs, dseq))
    hbs, c_s, n_s_, m_s = _mlstm(padt(r3(qbs, dbs, dseq)), kbs3, jnp.swapaxes(kbs3, 1, 2), padt(r3(vbs, dbs, dseq)),
                                 padt(r3(ogs, dbs, dseq)), sms_p, jnp.swapaxes(sms_p[:, :, SM_IG:SM_END], 1, 2), gh,
                                 state_C.astype(F32), state_n.astype(F32),
                                 jnp.broadcast_to(state_m.astype(F32)[:, :, None], (dbs, N_B, LANES)), lc=lc_s)
    x2s = _outproj(x1s, attn_s.reshape(n_s, D_A), hbs[:, :dseq].reshape(n_s, D_B), wa, wh)
    y_sample = _ffn(x2s, *ffn2).reshape(dbs, dseq, D_MODEL)

    sd = state_C.dtype
    return (y_prompt, y_sample,
            kf.reshape(bsz, seq, N_A, HD_A), vf.reshape(bsz, seq, N_A, HD_A), sm3[:, :, :D_IDX],
            c_p.astype(sd), n_p_.astype(sd), m_p[:, :, 0].astype(sd),
            kfs.reshape(dbs, dseq, N_A, HD_A), vfs.reshape(dbs, dseq, N_A, HD_A), sms3[:, :, :D_IDX],
            c_s.astype(sd), n_s_.astype(sd), m_s[:, :, 0].astype(sd))
```

```python
import functools
import math

import numpy as np
import jax
import jax.numpy as jnp
from jax import lax
from jax.experimental import pallas as pl
from jax.experimental.pallas import tpu as pltpu

F32 = jnp.float32
BF16 = jnp.bfloat16
I32 = jnp.int32

D_MODEL = 2048
CHUNK = 64
N_A, HD_A = 8, 128
H_IDX, D_IDX = 16, 64
TOPK_MAX = 256
N_B, HD_B = 4, 256
D_FF = 5504
NUM_BUCKETS, MAX_DISTANCE = 32, 128
EPS = 1e-6
NEG = -1e30
INT_MIN = -(2 ** 31)
LOG2E = math.log2(math.e)

LANES = 128
D_A = N_A * HD_A
D_B = N_B * HD_B
D_QI = H_IDX * D_IDX
FF_TILE = 512
D_FF_PAD = -(-D_FF // FF_TILE) * FF_TILE
FFN_TOK_TILE = 512
PROJ_TOK_TILE = 256
TOK_TILE = 512
SUB = 256
VMEM_LIMIT = 56 * 1024 * 1024

SM_KI, SM_WI, SM_IG, SM_LF = 0, D_IDX, D_IDX + H_IDX, D_IDX + H_IDX + N_B
SM_END = SM_LF + N_B


def _cparams(sem):
    return pltpu.CompilerParams(dimension_semantics=sem, vmem_limit_bytes=VMEM_LIMIT)


def _rms(x, g):
    ms = jnp.mean(x * x, axis=-1, keepdims=True)
    return x * lax.rsqrt(ms + EPS) * g


def _ffn_body(x_ref, g_ref, w1_ref, w3_ref, w2_ref, o_ref, hn_ref):
    j = pl.program_id(1)

    @pl.when(j == 0)
    def _():
        hn_ref[...] = _rms(x_ref[...], g_ref[...]).astype(BF16)
        o_ref[...] = jnp.zeros(o_ref.shape, F32)

    h = hn_ref[...]
    a = jnp.dot(h, w1_ref[...], preferred_element_type=F32)
    b = jnp.dot(h, w3_ref[...], preferred_element_type=F32)
    u = (a * jax.nn.sigmoid(a) * b).astype(BF16)
    o_ref[...] += jnp.dot(u, w2_ref[...], preferred_element_type=F32)

    @pl.when(j == pl.num_programs(1) - 1)
    def _():
        o_ref[...] = x_ref[...] + 0.5 * o_ref[...]


def _ffn(x, g, w1, w3, w2):
    n = x.shape[0]
    tm = min(FFN_TOK_TILE, n)
    grid = (n // tm, D_FF_PAD // FF_TILE)
    return pl.pallas_call(
        _ffn_body,
        grid=grid,
        in_specs=[
            pl.BlockSpec((tm, D_MODEL), lambda i, j: (i, 0)),
            pl.BlockSpec((1, D_MODEL), lambda i, j: (0, 0)),
            pl.BlockSpec((D_MODEL, FF_TILE), lambda i, j: (0, j)),
            pl.BlockSpec((D_MODEL, FF_TILE), lambda i, j: (0, j)),
            pl.BlockSpec((FF_TILE, D_MODEL), lambda i, j: (j, 0)),
        ],
        out_specs=pl.BlockSpec((tm, D_MODEL), lambda i, j: (i, 0)),
        out_shape=jax.ShapeDtypeStruct((n, D_MODEL), F32),
        scratch_shapes=[pltpu.VMEM((tm, D_MODEL), BF16)],
        compiler_params=_cparams(("parallel", "arbitrary")),
    )(x, g, w1, w3, w2)


def _cast_pad_body(x_ref, o_ref, *, blocks_valid, cols_valid):
    x = jnp.where(pl.program_id(0) < blocks_valid, x_ref[...], 0.0)
    o_ref[:, :cols_valid] = x.astype(BF16)
    if cols_valid < o_ref.shape[1]:
        o_ref[:, cols_valid:] = jnp.zeros((o_ref.shape[0], o_ref.shape[1] - cols_valid), BF16)


def _cast_pad(w, rows, cols, block_rows):
    r, c = w.shape
    assert rows % block_rows == 0 and r % block_rows == 0 and c % LANES == 0 and cols >= c and rows >= r
    nvalid = r // block_rows
    body = functools.partial(_cast_pad_body, blocks_valid=nvalid, cols_valid=c)
    return pl.pallas_call(
        body,
        grid=(rows // block_rows,),
        in_specs=[pl.BlockSpec((block_rows, c), lambda i: (jnp.minimum(i, nvalid - 1), 0))],
        out_specs=pl.BlockSpec((block_rows, cols), lambda i: (i, 0)),
        out_shape=jax.ShapeDtypeStruct((rows, cols), BF16),
        compiler_params=_cparams(("parallel",)),
    )(w)


def _prep_ffn(g, w1, w3, w2):
    return (g.reshape(1, D_MODEL),
            _cast_pad(w1, D_MODEL, D_FF_PAD, 256),
            _cast_pad(w3, D_MODEL, D_FF_PAD, 256),
            _cast_pad(w2, D_FF_PAD, D_MODEL, LANES))


N_MAIN_GROUPS = 8


def _head_norm(p, g, heads, hd):
    outs = []
    for h in range(heads):
        outs.append(_rms(p[:, h * hd:(h + 1) * hd], g))
    return outs


def _inproj_a_body(x_ref, g_ref, w_ref, wsh_ref, wsl_ref, gq_ref, gk_ref, sg_ref, sb_ref,
                   q_ref, kf_ref, k16_ref, vf_ref, v16_ref, qi_ref, sm_ref):
    h32 = _rms(x_ref[...], g_ref[...])
    h = h32.astype(BF16)
    col = lambda c: jnp.dot(h, w_ref[:, c * D_A:(c + 1) * D_A], preferred_element_type=F32)

    h_lo = (h32 - h.astype(F32)).astype(BF16)
    ps = (jnp.dot(h, wsh_ref[...], preferred_element_type=F32)
          + jnp.dot(h_lo, wsh_ref[...], preferred_element_type=F32)
          + jnp.dot(h, wsl_ref[...], preferred_element_type=F32))
    lane = lax.broadcasted_iota(I32, ps.shape, 1)
    ms = jnp.sum(jnp.where(lane < SM_WI, ps * ps, 0.0), axis=-1, keepdims=True) * (1.0 / D_IDX)
    kin = ps * lax.rsqrt(ms + EPS) * sg_ref[...]
    z = ps + sb_ref[...]
    ls = jnp.minimum(z, 0.0) - jnp.log1p(jnp.exp(-jnp.abs(z)))
    sm_ref[...] = jnp.where(lane < SM_WI, kin,
                            jnp.where(lane < SM_IG, ps,
                                      jnp.where(lane < SM_LF, z,
                                                jnp.where(lane < SM_END, ls, 0.0))))

    for hd, qh in enumerate(_head_norm(col(0), gq_ref[...], N_A, HD_A)):
        q_ref[:, hd * HD_A:(hd + 1) * HD_A] = (qh * (HD_A ** -0.5 * LOG2E)).astype(BF16)
    for hd, kh in enumerate(_head_norm(col(1), gk_ref[...], N_A, HD_A)):
        kf_ref[:, hd * HD_A:(hd + 1) * HD_A] = kh
        k16_ref[:, hd * HD_A:(hd + 1) * HD_A] = kh.astype(BF16)
    p = col(2)
    vf_ref[...] = p
    v16_ref[...] = p.astype(BF16)
    qi_ref[...] = col(3).astype(BF16)


def _inproj_b_body(x_ref, g_ref, w_ref, qb_ref, kb_ref, vb_ref, og_ref):
    h = _rms(x_ref[...], g_ref[...]).astype(BF16)
    col = lambda c: jnp.dot(h, w_ref[:, c * D_B:(c + 1) * D_B], preferred_element_type=F32)
    qb_ref[...] = col(0).astype(BF16)
    kb_ref[...] = (col(1) * HD_B ** -0.5).astype(BF16)
    vb_ref[...] = col(2).astype(BF16)
    og_ref[...] = jax.nn.sigmoid(col(3)).astype(BF16)


def _inproj(x, g, wa, wb, wsh, wsl, gq, gk, sg, sb):
    n = x.shape[0]
    tm = min(PROJ_TOK_TILE, n)
    tok = lambda w: pl.BlockSpec((tm, w), lambda i: (i, 0))
    const = lambda a: pl.BlockSpec(a.shape, lambda i: (0,) * a.ndim)
    wide = lambda dt: jax.ShapeDtypeStruct((n, D_A), dt)
    outs_a = pl.pallas_call(
        _inproj_a_body,
        grid=(n // tm,),
        in_specs=[tok(D_MODEL)] + [const(a) for a in (g, wa, wsh, wsl, gq, gk, sg, sb)],
        out_specs=[tok(D_A)] * 6 + [tok(LANES)],
        out_shape=[wide(BF16), wide(F32), wide(BF16), wide(F32), wide(BF16), wide(BF16),
                   jax.ShapeDtypeStruct((n, LANES), F32)],
        compiler_params=_cparams(("parallel",)),
    )(x, g, wa, wsh, wsl, gq, gk, sg, sb)
    outs_b = pl.pallas_call(
        _inproj_b_body,
        grid=(n // tm,),
        in_specs=[tok(D_MODEL), const(g), const(wb)],
        out_specs=[tok(D_B)] * 4,
        out_shape=[wide(BF16)] * 4,
        compiler_params=_cparams(("parallel",)),
    )(x, g, wb)
    q, kf, k16, vf, v16, qi, sm = outs_a
    return (q, kf, k16, vf, v16, qi) + tuple(outs_b) + (sm,)


def _prep_inproj(g_mix, w_in, g_q, g_k, g_kidx, b_i, b_f):
    o_ki = 4 * D_A
    o_qb = o_ki + D_IDX + H_IDX
    o_ib = o_qb + 4 * D_B
    wa = w_in[:, :o_ki].astype(BF16)
    wb = w_in[:, o_qb:o_ib].astype(BF16)
    ws = jnp.concatenate([w_in[:, o_ki:o_qb], w_in[:, o_ib:],
                          jnp.zeros((D_MODEL, LANES - SM_END), F32)], axis=1)
    wsh = ws.astype(BF16)
    wsl = (ws - wsh.astype(F32)).astype(BF16)
    sg = jnp.concatenate([g_kidx, jnp.ones((LANES - D_IDX,), F32)]).reshape(1, LANES)
    sb = jnp.concatenate([jnp.zeros((SM_IG,), F32), b_i, b_f,
                          jnp.zeros((LANES - SM_END,), F32)]).reshape(1, LANES)
    return (g_mix.reshape(1, D_MODEL), wa, wb, wsh, wsl,
            g_q.reshape(1, HD_A), g_k.reshape(1, HD_A), sg, sb)


def _outproj_body(x_ref, a_ref, h_ref, wa_ref, wh_ref, o_ref):
    o_ref[...] = (x_ref[...]
                  + jnp.dot(a_ref[...], wa_ref[...], preferred_element_type=F32)
                  + jnp.dot(h_ref[...], wh_ref[...], preferred_element_type=F32))


def _outproj(x, a, h, wa, wh):
    n = x.shape[0]
    tm = min(TOK_TILE, n)
    return pl.pallas_call(
        _outproj_body,
        grid=(n // tm,),
        in_specs=[pl.BlockSpec((tm, D_MODEL), lambda i: (i, 0)),
                  pl.BlockSpec((tm, D_A), lambda i: (i, 0)),
                  pl.BlockSpec((tm, D_B), lambda i: (i, 0)),
                  pl.BlockSpec((D_A, D_MODEL), lambda i: (0, 0)),
                  pl.BlockSpec((D_B, D_MODEL), lambda i: (0, 0))],
        out_specs=pl.BlockSpec((tm, D_MODEL), lambda i: (i, 0)),
        out_shape=jax.ShapeDtypeStruct((n, D_MODEL), F32),
        compiler_params=_cparams(("parallel",)),
    )(x, a, h, wa, wh)


def _bucket_thresholds():
    nb = NUM_BUCKETS // 2
    max_exact = nb // 2
    span = nb - max_exact
    ratio = MAX_DISTANCE // max_exact
    out = []
    for m in range(1, span):
        n = max_exact
        while n ** span < max_exact ** span * ratio ** m:
            n += 1
        out.append(n)
    return tuple(out)


def _bias_body(tbl_ref, o_ref):
    o = pl.program_id(0)
    h = pl.program_id(1)
    nb = NUM_BUCKETS // 2
    max_exact = nb // 2
    row = lax.broadcasted_iota(I32, (SUB, SUB), 0)
    col = lax.broadcasted_iota(I32, (SUB, SUB), 1)
    rel = row - o * SUB - col
    n = jnp.abs(rel)
    large = jnp.full((SUB, SUB), max_exact, I32)
    for t in _bucket_thresholds():
        large = large + (n >= t).astype(I32)
    bucket = jnp.where(rel > 0, nb, 0) + jnp.where(n < max_exact, n, large)
    val = jnp.zeros((SUB, SUB), F32)
    for bk in range(NUM_BUCKETS):
        val = jnp.where(bucket == bk, tbl_ref[bk, h], val)
    o_ref[0, 0] = (val - tbl_ref[nb - 1, h]) * LOG2E


def _bias_tiles(rel_bias):
    return pl.pallas_call(
        _bias_body,
        grid=(2, N_A),
        in_specs=[pl.BlockSpec(memory_space=pltpu.SMEM)],
        out_specs=pl.BlockSpec((1, 1, SUB, SUB), lambda o, h: (o, h, 0, 0)),
        out_shape=jax.ShapeDtypeStruct((2, N_A, SUB, SUB), F32),
    )(rel_bias)


WCH = 4
A_UNROLL = 4
FAR_SUBS = 2


def _bit(k):
    return INT_MIN if k == 31 else 1 << k


def _bit_transpose32(a):
    a = list(a)
    j, m = 16, 0x0000FFFF
    while j:
        for k in range(32):
            if not k & j:
                t = (lax.shift_right_logical(a[k], jnp.int32(j)) ^ a[k + j]) & m
                a[k + j] = a[k + j] ^ t
                a[k] = a[k] ^ jnp.left_shift(t, jnp.int32(j))
        j >>= 1
        m ^= m << j
    return a


def _dsa_body(ti_ref, tk_ref, tn_ref, tl_ref,
              qt_ref, qit_ref, wit_ref, ki_ref, k_ref, vt_ref, bias_ref, o_ref,
              plane_ref, adm_ref, cand_ref, sel_ref, mb_ref, m_ref, l_ref, acc_ref, lg_ref, al_ref,
              *, tq, tk, q_off, l_valid, topk):
    s = pl.program_id(1)
    i = ti_ref[s]
    kt = tk_ref[s]
    nsub = tn_ref[s]
    subs = tk // SUB
    q_pos0 = q_off + i * tq
    i_sub = q_off // SUB + (i * tq) // SUB

    def fold(x, op, rows=8):
        parts = [x[r:r + rows] for r in range(0, x.shape[0], rows)]
        while len(parts) > 1:
            parts = [op(parts[a], parts[a + 1]) for a in range(0, len(parts), 2)]
        return parts[0]

    @pl.when(kt == 0)
    def _():
        m_ref[...] = jnp.full(m_ref.shape, NEG, F32)
        l_ref[...] = jnp.zeros(l_ref.shape, F32)
        acc_ref[...] = jnp.zeros(acc_ref.shape, F32)

        qchunk = (q_pos0 + lax.broadcasted_iota(I32, (SUB, tq), 1)) >> 6
        krow = lax.broadcasted_iota(I32, (SUB, tq), 0)

        def score_codes(t):
            off = pl.multiple_of(t * SUB, SUB)
            kit = ki_ref[0, pl.ds(off, SUB), :]
            sc = jnp.zeros((SUB, tq), F32)
            for j in range(H_IDX):
                sj = jnp.dot(kit, qit_ref[0, j * D_IDX:(j + 1) * D_IDX, :], preferred_element_type=F32)
                sc = sc + jnp.maximum(sj, 0.0) * wit_ref[0, j:j + 1, :]
            bits = lax.bitcast_convert_type(sc, I32)
            return bits ^ ((bits >> 31) | INT_MIN)

        def slice_codes(t, code, all_admissible):
            planes = _bit_transpose32([code[8 * k:8 * k + 8] for k in range(32)])
            row = pl.multiple_of(t * 8, 8)
            for k in range(32):
                plane_ref[k, pl.ds(row, 8), :] = planes[k]
            if all_admissible:
                adm_ref[pl.ds(row, 8), :] = jnp.full((8, tq), -1, I32)
            else:
                kpos = t * SUB + krow
                adm = ((kpos >> 6) <= qchunk) & (kpos < l_valid)
                aw = jnp.zeros((8, tq), I32)
                for k in range(32):
                    aw = aw | jnp.where(adm[8 * k:8 * k + 8], _bit(k), 0)
                adm_ref[pl.ds(row, 8), :] = aw

        n_open = jnp.minimum(l_valid, ((q_pos0 >> 6) + 1) * CHUNK) // SUB

        def score_tiles(gi, c):
            for u in range(A_UNROLL):
                slice_codes(gi * A_UNROLL + u, score_codes(gi * A_UNROLL + u), True)
            return c

        def score_tile(t, c):
            slice_codes(t, score_codes(t), False)
            return c

        lax.fori_loop(0, n_open // A_UNROLL, score_tiles, 0)
        lax.fori_loop(n_open // A_UNROLL * A_UNROLL, nsub, score_tile, 0)

        nchunk = (nsub + WCH - 1) // WCH

        def clear_tile(t, c):
            row = pl.multiple_of(t * 8, 8)
            plane_ref[:, pl.ds(row, 8), :] = jnp.zeros((32, 8, tq), I32)
            adm_ref[pl.ds(row, 8), :] = jnp.zeros((8, tq), I32)
            return c

        lax.fori_loop(nsub, nchunk * WCH, clear_tile, 0)

        def chunk_rows(cn):
            return pl.ds(pl.multiple_of(cn * (8 * WCH), 8 * WCH), 8 * WCH)

        def init_chunk(cn, c):
            cand_ref[chunk_rows(cn), :] = adm_ref[chunk_rows(cn), :]
            sel_ref[chunk_rows(cn), :] = jnp.zeros((8 * WCH, tq), I32)
            return c

        lax.fori_loop(0, nchunk, init_chunk, 0)

        def count_cand(plane_of):
            def count_chunk(cn, cnt):
                ones = cand_ref[chunk_rows(cn), :] & plane_of(cn)
                return cnt + lax.population_count(ones)

            cnt = lax.fori_loop(0, nchunk, count_chunk, jnp.zeros((8 * WCH, tq), I32))
            return jnp.sum(fold(cnt.astype(F32), jnp.add), axis=0, keepdims=True).astype(I32)

        def radix_step(plane_of, n_sel):
            n_one = count_cand(plane_of)
            take = (n_sel + n_one) >= topk

            def update_chunk(cn, c):
                cw = cand_ref[chunk_rows(cn), :]
                pw = plane_of(cn)
                ones = cw & pw
                cand_ref[chunk_rows(cn), :] = jnp.where(take, ones, cw & ~pw)
                sel_ref[chunk_rows(cn), :] = sel_ref[chunk_rows(cn), :] | jnp.where(take, 0, ones)
                return c

            lax.fori_loop(0, nchunk, update_chunk, 0)
            return jnp.where(take, n_sel, n_sel + n_one)

        n_sel = lax.fori_loop(
            0, 32, lambda it, n: radix_step(lambda cn: plane_ref[31 - it, chunk_rows(cn), :], n),
            jnp.zeros((1, tq), I32))

        n_tie = count_cand(lambda cn: jnp.int32(-1))
        crowded = jnp.max(jnp.where((n_sel + n_tie > topk) & (n_tie > 1), 1, 0))

        @pl.when(crowded > 0)
        def _():
            wrow = lax.broadcasted_iota(I32, (8 * WCH, tq), 0)
            n = n_sel
            for tb in reversed(range(max(1, (plane_ref.shape[1] // 8 - 1).bit_length()))):
                n = radix_step(lambda cn, tb=tb: (((cn * WCH + (wrow >> 3)) >> tb) & 1) - 1, n)
            for low in (0x0000FFFF, 0x00FF00FF, 0x0F0F0F0F, 0x33333333, 0x55555555):
                n = radix_step(lambda cn, low=low: jnp.int32(low), n)
            for rb in (2, 1, 0):
                n = radix_step(lambda cn, rb=rb: ((wrow >> rb) & 1) - 1, n)

        def final_chunk(cn, c):
            sel_ref[chunk_rows(cn), :] = sel_ref[chunk_rows(cn), :] | cand_ref[chunk_rows(cn), :]
            return c

        lax.fori_loop(0, nchunk, final_chunk, 0)

    n_here = jnp.clip(nsub - kt * subs, 0, subs)

    def key_block(u, nsubs, near):
        nk = nsubs * SUB
        g = kt * subs + u
        dsub = g - i_sub
        uoff = pl.multiple_of(u * SUB, SUB)
        for a in range(nsubs):
            sw = sel_ref[pl.ds(pl.multiple_of((g + a) * 8, 8), 8), :]
            for k in range(32):
                r = a * SUB + 8 * k
                mb_ref[r:r + 8, :] = jnp.where((sw & _bit(k)) != 0, 0.0, 2.0 * NEG)
        for h in range(N_A):
            hs = slice(h * HD_A, (h + 1) * HD_A)
            kh = k_ref[0, pl.ds(uoff, nk), hs]
            lg = jnp.dot(kh, qt_ref[0, hs, :], preferred_element_type=F32)
            if near:
                lg = lg + bias_ref[-dsub, h, :, 0:tq]
            lg = lg + mb_ref[0:nk, :]
            lg_ref[h, 0:nk, :] = lg
            m_old = m_ref[h]
            m_new = jnp.maximum(m_old, jnp.max(fold(lg, jnp.maximum), axis=0, keepdims=True))
            al_ref[h] = jnp.exp2(m_old - m_new)
            m_ref[h] = m_new
        for h in range(N_A):
            hs = slice(h * HD_A, (h + 1) * HD_A)
            vth = vt_ref[0, hs, pl.ds(uoff, nk)]
            alpha = al_ref[h]
            p = jnp.exp2((lg_ref[h, 0:nk, :] - m_ref[h][0:1, :]).astype(BF16))
            pv = jnp.dot(jnp.concatenate([vth, jnp.ones((16, nk), BF16)], axis=0), p,
                         preferred_element_type=F32)
            l_ref[h] = alpha * l_ref[h] + pv[HD_A:HD_A + 8, :]
            acc_ref[hs, :] = alpha[0:1, :] * acc_ref[hs, :] + pv[0:HD_A, :]

    n_far = jnp.clip(i_sub - 1 - kt * subs, 0, n_here)

    def near_one(u, c):
        key_block(u, 1, True)
        return c

    done = 0
    width = FAR_SUBS
    while width > subs:
        width //= 2
    while width >= 1:
        def far_block(bi, c, width=width, done=done):
            key_block(done + bi * width, width, False)
            return c

        count = (n_far - done) // width
        lax.fori_loop(0, count, far_block, 0)
        done = done + count * width
        width //= 2
    lax.fori_loop(n_far, n_here, near_one, 0)

    @pl.when(tl_ref[s] == 1)
    def _():
        for h in range(N_A):
            hs = slice(h * HD_A, (h + 1) * HD_A)
            o_ref[0, hs, :] = (acc_ref[hs, :] / l_ref[h][0:1, :]).astype(BF16)


def _dsa(qt, qit, wit, ki, k, vt, bias, *, tq, tk, q_off, l_valid, topk):
    bsz, sq = qt.shape[0], qt.shape[2]
    sk = k.shape[1]
    assert sq % tq == 0 and sk % tk == 0 and tk % SUB == 0 and q_off % SUB == 0
    assert (tq % SUB == 0 or sq == tq) and tq % LANES == 0
    subs = tk // SUB
    wrows = -(-(sk // SUB) // WCH) * WCH * 8
    ti, tkk, tn, tl = [], [], [], []
    for i in range(sq // tq):
        q_last = q_off + (i + 1) * tq - 1
        lim = min(l_valid, (q_last // CHUNK + 1) * CHUNK)
        nsub = -(-lim // SUB)
        nkt = -(-nsub // subs)
        for t in range(nkt):
            ti.append(i), tkk.append(t), tn.append(nsub), tl.append(int(t == nkt - 1))
    tabs = [jnp.asarray(np.asarray(a, np.int32)) for a in (ti, tkk, tn, tl)]
    body = functools.partial(_dsa_body, tq=tq, tk=tk, q_off=q_off, l_valid=l_valid, topk=topk)
    qmap = lambda b, s, ti, tk_, tn, tl: (b, 0, ti[s])
    grid_spec = pltpu.PrefetchScalarGridSpec(
        num_scalar_prefetch=4,
        grid=(bsz, len(ti)),
        in_specs=[pl.BlockSpec((1, D_A, tq), qmap),
                  pl.BlockSpec((1, D_QI, tq), qmap),
                  pl.BlockSpec((1, H_IDX, tq), qmap),
                  pl.BlockSpec((1, sk, D_IDX), lambda b, s, *_: (b, 0, 0)),
                  pl.BlockSpec((1, tk, D_A), lambda b, s, ti, tk_, tn, tl: (b, tk_[s], 0)),
                  pl.BlockSpec((1, D_A, tk), lambda b, s, ti, tk_, tn, tl: (b, 0, tk_[s])),
                  pl.BlockSpec((2, N_A, SUB, SUB), lambda b, s, *_: (0, 0, 0, 0))],
        out_specs=pl.BlockSpec((1, D_A, tq), qmap),
        scratch_shapes=[pltpu.VMEM((32, wrows, tq), I32),
                        pltpu.VMEM((wrows, tq), I32),
                        pltpu.VMEM((wrows, tq), I32),
                        pltpu.VMEM((wrows, tq), I32),
                        pltpu.VMEM((FAR_SUBS * SUB, tq), F32),
                        pltpu.VMEM((N_A, 8, tq), F32),
                        pltpu.VMEM((N_A, 8, tq), F32),
                        pltpu.VMEM((D_A, tq), F32),
                        pltpu.VMEM((N_A, FAR_SUBS * SUB, tq), F32),
                        pltpu.VMEM((N_A, 8, tq), F32)])
    return pl.pallas_call(
        body,
        grid_spec=grid_spec,
        out_shape=jax.ShapeDtypeStruct((bsz, D_A, sq), BF16),
        compiler_params=_cparams(("parallel", "arbitrary")),
    )(*tabs, qt, qit, wit, ki, k, vt, bias)


def _pack_cache_body(k4_ref, v4_ref, kn_ref, vnt_ref, ko_ref, vto_ref, *, ncache):
    c = pl.program_id(1)

    @pl.when(c < ncache)
    def _():
        for h in range(N_A):
            hs = slice(h * HD_A, (h + 1) * HD_A)
            ko_ref[0, :, hs] = k4_ref[0, pl.ds(h, SUB, stride=N_A), :].astype(BF16)
            vto_ref[0, hs, :] = v4_ref[0, pl.ds(h, SUB, stride=N_A), :].T.astype(BF16)

    @pl.when(c == ncache)
    def _():
        ko_ref[0] = kn_ref[0]
        vto_ref[0] = vnt_ref[0]


def _pack_cache(cache_k, cache_v, k_new, vt_new):
    bsz, past = cache_k.shape[0], cache_k.shape[1]
    ncache = past // SUB
    cmap = lambda b, c: (b, jnp.minimum(c, ncache - 1), 0)
    rows = lambda a: a.reshape(bsz, past * N_A, HD_A)
    return pl.pallas_call(
        functools.partial(_pack_cache_body, ncache=ncache),
        grid=(bsz, ncache + 1),
        in_specs=[pl.BlockSpec((1, SUB * N_A, HD_A), cmap),
                  pl.BlockSpec((1, SUB * N_A, HD_A), cmap),
                  pl.BlockSpec((1, SUB, D_A), lambda b, c: (b, 0, 0)),
                  pl.BlockSpec((1, D_A, SUB), lambda b, c: (b, 0, 0))],
        out_specs=[pl.BlockSpec((1, SUB, D_A), lambda b, c: (b, c, 0)),
                   pl.BlockSpec((1, D_A, SUB), lambda b, c: (b, 0, c))],
        out_shape=[jax.ShapeDtypeStruct((bsz, past + SUB, D_A), BF16),
                   jax.ShapeDtypeStruct((bsz, D_A, past + SUB), BF16)],
        compiler_params=_cparams(("parallel", "arbitrary")),
    )(rows(cache_k), rows(cache_v), k_new, vt_new)


def _mlstm_body(q_ref, k_ref, kt_ref, v_ref, og_ref, sm_ref, gr_ref, gh_ref, c0_ref, n0_ref, m0_ref,
                hb_ref, co_ref, no_ref, mo_ref, c_s, n_s, m_s, *, lc):
    c = pl.program_id(1)

    @pl.when(c == 0)
    def _():
        c_s[...] = c0_ref[0]
        for h in range(N_B):
            n_s[h] = jnp.broadcast_to(n0_ref[0, h:h + 1, :], (8, HD_B))
            m_s[h] = jnp.broadcast_to(m0_ref[0, h:h + 1, :], (8, LANES))

    row = lax.broadcasted_iota(I32, (lc, lc), 0)
    col = lax.broadcasted_iota(I32, (lc, lc), 1)
    causal = col <= row
    sm = sm_ref[0]
    gr = gr_ref[0]
    hp = lax.Precision.HIGHEST
    b_cols = jnp.dot(causal.astype(F32), sm, precision=hp, preferred_element_type=F32)
    b_rows = jnp.dot(gr, (row <= col).astype(F32), precision=hp, preferred_element_type=F32)

    for h in range(N_B):
        hs = slice(h * HD_B, (h + 1) * HD_B)
        m_prev = m_s[h][0:1, 0:1]
        bc = b_cols[:, SM_LF + h:SM_LF + h + 1]
        ig_c = sm[:, SM_IG + h:SM_IG + h + 1]
        br = b_rows[N_B + h:N_B + h + 1, :]
        ig_r = gr[h:h + 1, :]
        log_d = jnp.where(causal, bc - br + ig_r, NEG)
        log_inter = bc + m_prev
        m_t = jnp.maximum(log_inter, jnp.max(log_d, axis=1, keepdims=True))
        d = jnp.exp(log_d - m_t)
        inter = jnp.exp(log_inter - m_t)
        qh = q_ref[0, :, hs]
        kth = kt_ref[0, hs, :]
        vh = v_ref[0, :, hs]
        sc = jnp.dot(qh, kth, preferred_element_type=F32) * d
        ch = c_s[h]
        nrow = n_s[h][0:1, :]
        num = (jnp.dot(sc.astype(BF16), vh, preferred_element_type=F32)
               + inter * jnp.dot(qh, ch.astype(BF16), preferred_element_type=F32))
        den = (jnp.sum(sc, axis=1, keepdims=True)
               + inter * jnp.sum(qh.astype(F32) * nrow, axis=1, keepdims=True))
        hh = num / jnp.maximum(jnp.abs(den), jnp.exp(-m_t))
        hb = _rms(hh, gh_ref[...]) * og_ref[0, :, hs].astype(F32)
        hb_ref[0, :, hs] = hb.astype(BF16)

        m_new = m_t[lc - 1:lc, :]
        b_last = bc[lc - 1:lc, :]
        decay = jnp.exp(b_last + m_prev - m_new)
        w_r = jnp.exp(b_last - br + ig_r - m_new)
        w_c = jnp.exp(b_last - bc + ig_c - m_new)
        kw = (kth.astype(F32) * w_r).astype(BF16)
        c_s[h] = decay * ch + jnp.dot(kw, vh, preferred_element_type=F32)
        n_new = decay * nrow + jnp.sum(k_ref[0, :, hs].astype(F32) * w_c, axis=0, keepdims=True)
        n_s[h] = jnp.broadcast_to(n_new, (8, HD_B))
        m_s[h] = jnp.broadcast_to(m_new, (8, LANES))

    @pl.when(c == pl.num_programs(1) - 1)
    def _():
        co_ref[0] = c_s[...]
        for h in range(N_B):
            no_ref[0, h:h + 1, :] = n_s[h][0:1, :]
            mo_ref[0, h:h + 1, :] = m_s[h][0:1, :]


def _mlstm(q, k, kt, v, og, sm, gr, g_h, c0, n0, m0, *, lc):
    bsz, s = q.shape[0], q.shape[1]
    tokb = lambda w: pl.BlockSpec((1, lc, w), lambda b, c: (b, c, 0))
    st3 = lambda w: pl.BlockSpec((1, N_B, w), lambda b, c: (b, 0, 0))
    st4 = pl.BlockSpec((1, N_B, HD_B, HD_B), lambda b, c: (b, 0, 0, 0))
    return pl.pallas_call(
        functools.partial(_mlstm_body, lc=lc),
        grid=(bsz, s // lc),
        in_specs=[tokb(D_B), tokb(D_B),
                  pl.BlockSpec((1, D_B, lc), lambda b, c: (b, 0, c)),
                  tokb(D_B), tokb(D_B), tokb(LANES),
                  pl.BlockSpec((1, 2 * N_B, lc), lambda b, c: (b, 0, c)),
                  pl.BlockSpec((1, HD_B), lambda b, c: (0, 0)),
                  st4, st3(HD_B), st3(LANES)],
        out_specs=[tokb(D_B), st4, st3(HD_B), st3(LANES)],
        out_shape=[jax.ShapeDtypeStruct((bsz, s, D_B), BF16),
                   jax.ShapeDtypeStruct((bsz, N_B, HD_B, HD_B), F32),
                   jax.ShapeDtypeStruct((bsz, N_B, HD_B), F32),
                   jax.ShapeDtypeStruct((bsz, N_B, LANES), F32)],
        scratch_shapes=[pltpu.VMEM((N_B, HD_B, HD_B), F32),
                        pltpu.VMEM((N_B, 8, HD_B), F32),
                        pltpu.VMEM((N_B, 8, LANES), F32)],
        compiler_params=_cparams(("parallel", "arbitrary")),
    )(q, k, kt, v, og, sm, gr, g_h, c0, n0, m0)


def _mixer_front(x, ffn1, inp):
    x1 = _ffn(x, *ffn1)
    return (x1,) + tuple(_inproj(x1, *inp))


def kernel(x_prompt, x_sample, cache_k, cache_v, cache_k_idx, state_C, state_n, state_m, g_ffn1, w1_ffn1, w3_ffn1, w2_ffn1, g_mix, w_in, g_q, g_k, g_kidx, rel_bias, b_i, b_f, g_h, w_out, g_ffn2, w1_ffn2, w3_ffn2, w2_ffn2):
    bsz, seq = x_prompt.shape[0], x_prompt.shape[1]
    dbs, dseq = x_sample.shape[0], x_sample.shape[1]
    past = cache_k.shape[1]

    ffn1 = _prep_ffn(g_ffn1, w1_ffn1, w3_ffn1, w2_ffn1)
    ffn2 = _prep_ffn(g_ffn2, w1_ffn2, w3_ffn2, w2_ffn2)
    inp = _prep_inproj(g_mix, w_in, g_q, g_k, g_kidx, b_i, b_f)
    wa = w_out[:D_A].astype(BF16)
    wh = w_out[D_A:].astype(BF16)
    gh = g_h.reshape(1, HD_B)
    bias = _bias_tiles(rel_bias)

    n_p = bsz * seq
    x1, q, kf, k16, vf, v16, qi, qb, kb, vb, og, sm = _mixer_front(x_prompt.reshape(n_p, D_MODEL), ffn1, inp)
    r3 = lambda a, b_, s_: a.reshape(b_, s_, a.shape[-1])
    sm3 = r3(sm, bsz, seq)
    tr = lambda a: jnp.swapaxes(a, 1, 2)
    tq_p = min(SUB, seq)
    attn = tr(_dsa(tr(r3(q, bsz, seq)), tr(r3(qi, bsz, seq)), tr(sm3[:, :, SM_WI:SM_IG]),
                   sm3[:, :, :D_IDX].astype(BF16), r3(k16, bsz, seq), tr(r3(v16, bsz, seq)), bias,
                   tq=tq_p, tk=min(1024, seq), q_off=0, l_valid=seq, topk=min(TOPK_MAX, seq // 4)))
    gr = jnp.swapaxes(sm3[:, :, SM_IG:SM_END], 1, 2)
    kb3 = r3(kb, bsz, seq)
    hb, c_p, n_p_, m_p = _mlstm(r3(qb, bsz, seq), kb3, jnp.swapaxes(kb3, 1, 2), r3(vb, bsz, seq),
                                r3(og, bsz, seq), sm3, gr, gh,
                                jnp.zeros((bsz, N_B, HD_B, HD_B), F32), jnp.zeros((bsz, N_B, HD_B), F32),
                                jnp.zeros((bsz, N_B, LANES), F32), lc=min(SUB, seq))
    x2 = _outproj(x1, attn.reshape(n_p, D_A), hb.reshape(n_p, D_B), wa, wh)
    y_prompt = _ffn(x2, *ffn2).reshape(bsz, seq, D_MODEL)

    n_s = dbs * dseq
    x1s, qs, kfs, k16s, vfs, v16s, qis, qbs, kbs, vbs, ogs, sms = _mixer_front(
        x_sample.reshape(n_s, D_MODEL), ffn1, inp)
    sms3 = r3(sms, dbs, dseq)
    l_all = past + dseq
    tk_s = 3 * SUB
    sk = -(-l_all // tk_s) * tk_s
    padk = lambda a: jnp.pad(a, ((0, 0), (0, sk - l_all), (0, 0)))
    assert past % SUB == 0 and sk - past == SUB
    padn = lambda a: jnp.pad(a, ((0, 0), (0, SUB - dseq), (0, 0)))
    k_all, vt_all = _pack_cache(cache_k, cache_v, padn(r3(k16s, dbs, dseq)), tr(padn(r3(v16s, dbs, dseq))))
    ki_all = padk(jnp.concatenate([cache_k_idx.astype(BF16), sms3[:, :, :D_IDX].astype(BF16)], axis=1))
    assert dseq <= LANES
    padq = lambda a: jnp.pad(tr(a), ((0, 0), (0, 0), (0, LANES - dseq)))
    attn_s = tr(_dsa(padq(r3(qs, dbs, dseq)), padq(r3(qis, dbs, dseq)), padq(sms3[:, :, SM_WI:SM_IG]),
                     ki_all, k_all, vt_all, bias,
                     tq=LANES, tk=tk_s, q_off=past, l_valid=l_all, topk=min(TOPK_MAX, l_all // 4))[:, :, :dseq])
    lc_s = LANES
    padt = lambda a: jnp.pad(a, ((0, 0), (0, lc_s - dseq), (0, 0)))
    lane = jnp.arange(LANES)
    sm_pad = jnp.where((lane >= SM_IG) & (lane < SM_LF), NEG, 0.0).astype(F32)
    sms_p = jnp.concatenate([sms3, jnp.broadcast_to(sm_pad, (dbs, lc_s - dseq, LANES))], axis=1)
    kbs3 = padt(r3(kbs, dbs, dseq))
    hbs, c_s, n_s_, m_s = _mlstm(padt(r3(qbs, dbs, dseq)), kbs3, jnp.swapaxes(kbs3, 1, 2), padt(r3(vbs, dbs, dseq)),
                                 padt(r3(ogs, dbs, dseq)), sms_p, jnp.swapaxes(sms_p[:, :, SM_IG:SM_END], 1, 2), gh,
                                 state_C.astype(F32), state_n.astype(F32),
                                 jnp.broadcast_to(state_m.astype(F32)[:, :, None], (dbs, N_B, LANES)), lc=lc_s)
    x2s = _outproj(x1s, attn_s.reshape(n_s, D_A), hbs[:, :dseq].reshape(n_s, D_B), wa, wh)
    y_sample = _ffn(x2s, *ffn2).reshape(dbs, dseq, D_MODEL)

    sd = state_C.dtype
    return (y_prompt, y_sample,
            kf.reshape(bsz, seq, N_A, HD_A), vf.reshape(bsz, seq, N_A, HD_A), sm3[:, :, :D_IDX],
            c_p.astype(sd), n_p_.astype(sd), m_p[:, :, 0].astype(sd),
            kfs.reshape(dbs, dseq, N_A, HD_A), vfs.reshape(dbs, dseq, N_A, HD_A), sms3[:, :, :D_IDX],
            c_s.astype(sd), n_s_.astype(sd), m_s[:, :, 0].astype(sd))
```

```python
import functools
import math

import numpy as np
import jax
import jax.numpy as jnp
from jax import lax
from jax.experimental import pallas as pl
from jax.experimental.pallas import tpu as pltpu

F32 = jnp.float32
BF16 = jnp.bfloat16
I32 = jnp.int32

D_MODEL = 2048
CHUNK = 64
N_A, HD_A = 8, 128
H_IDX, D_IDX = 16, 64
TOPK_MAX = 256
N_B, HD_B = 4, 256
D_FF = 5504
NUM_BUCKETS, MAX_DISTANCE = 32, 128
EPS = 1e-6
NEG = -1e30
INT_MIN = -(2 ** 31)
LOG2E = math.log2(math.e)

LANES = 128
D_A = N_A * HD_A
D_B = N_B * HD_B
D_QI = H_IDX * D_IDX
FF_TILE = 512
D_FF_PAD = -(-D_FF // FF_TILE) * FF_TILE
FFN_TOK_TILE = 512
PROJ_TOK_TILE = 256
TOK_TILE = 512
SUB = 256
VMEM_LIMIT = 56 * 1024 * 1024

SM_KI, SM_WI, SM_IG, SM_LF = 0, D_IDX, D_IDX + H_IDX, D_IDX + H_IDX + N_B
SM_END = SM_LF + N_B


def _cparams(sem):
    return pltpu.CompilerParams(dimension_semantics=sem, vmem_limit_bytes=VMEM_LIMIT)


def _rms(x, g):
    ms = jnp.mean(x * x, axis=-1, keepdims=True)
    return x * lax.rsqrt(ms + EPS) * g


def _ffn_body(x_ref, g_ref, w1_ref, w3_ref, w2_ref, o_ref, hn_ref):
    j = pl.program_id(1)

    @pl.when(j == 0)
    def _():
        hn_ref[...] = _rms(x_ref[...], g_ref[...]).astype(BF16)
        o_ref[...] = jnp.zeros(o_ref.shape, F32)

    h = hn_ref[...]
    a = jnp.dot(h, w1_ref[...], preferred_element_type=F32)
    b = jnp.dot(h, w3_ref[...], preferred_element_type=F32)
    u = (a * jax.nn.sigmoid(a) * b).astype(BF16)
    o_ref[...] += jnp.dot(u, w2_ref[...], preferred_element_type=F32)

    @pl.when(j == pl.num_programs(1) - 1)
    def _():
        o_ref[...] = x_ref[...] + 0.5 * o_ref[...]


def _ffn(x, g, w1, w3, w2):
    n = x.shape[0]
    tm = min(FFN_TOK_TILE, n)
    grid = (n // tm, D_FF_PAD // FF_TILE)
    return pl.pallas_call(
        _ffn_body,
        grid=grid,
        in_specs=[
            pl.BlockSpec((tm, D_MODEL), lambda i, j: (i, 0)),
            pl.BlockSpec((1, D_MODEL), lambda i, j: (0, 0)),
            pl.BlockSpec((D_MODEL, FF_TILE), lambda i, j: (0, j)),
            pl.BlockSpec((D_MODEL, FF_TILE), lambda i, j: (0, j)),
            pl.BlockSpec((FF_TILE, D_MODEL), lambda i, j: (j, 0)),
        ],
        out_specs=pl.BlockSpec((tm, D_MODEL), lambda i, j: (i, 0)),
        out_shape=jax.ShapeDtypeStruct((n, D_MODEL), F32),
        scratch_shapes=[pltpu.VMEM((tm, D_MODEL), BF16)],
        compiler_params=_cparams(("parallel", "arbitrary")),
    )(x, g, w1, w3, w2)


def _cast_pad_body(x_ref, o_ref, *, blocks_valid, cols_valid):
    x = jnp.where(pl.program_id(0) < blocks_valid, x_ref[...], 0.0)
    o_ref[:, :cols_valid] = x.astype(BF16)
    if cols_valid < o_ref.shape[1]:
        o_ref[:, cols_valid:] = jnp.zeros((o_ref.shape[0], o_ref.shape[1] - cols_valid), BF16)


def _cast_pad(w, rows, cols, block_rows):
    r, c = w.shape
    assert rows % block_rows == 0 and r % block_rows == 0 and c % LANES == 0 and cols >= c and rows >= r
    nvalid = r // block_rows
    body = functools.partial(_cast_pad_body, blocks_valid=nvalid, cols_valid=c)
    return pl.pallas_call(
        body,
        grid=(rows // block_rows,),
        in_specs=[pl.BlockSpec((block_rows, c), lambda i: (jnp.minimum(i, nvalid - 1), 0))],
        out_specs=pl.BlockSpec((block_rows, cols), lambda i: (i, 0)),
        out_shape=jax.ShapeDtypeStruct((rows, cols), BF16),
        compiler_params=_cparams(("parallel",)),
    )(w)


def _prep_ffn(g, w1, w3, w2):
    return (g.reshape(1, D_MODEL),
            _cast_pad(w1, D_MODEL, D_FF_PAD, 256),
            _cast_pad(w3, D_MODEL, D_FF_PAD, 256),
            _cast_pad(w2, D_FF_PAD, D_MODEL, LANES))


N_MAIN_GROUPS = 8


def _head_norm(p, g, heads, hd):
    outs = []
    for h in range(heads):
        outs.append(_rms(p[:, h * hd:(h + 1) * hd], g))
    return outs


def _inproj_a_body(x_ref, g_ref, w_ref, wsh_ref, wsl_ref, gq_ref, gk_ref, sg_ref, sb_ref,
                   q_ref, kf_ref, k16_ref, vf_ref, v16_ref, qi_ref, sm_ref):
    h32 = _rms(x_ref[...], g_ref[...])
    h = h32.astype(BF16)
    col = lambda c: jnp.dot(h, w_ref[:, c * D_A:(c + 1) * D_A], preferred_element_type=F32)

    h_lo = (h32 - h.astype(F32)).astype(BF16)
    ps = (jnp.dot(h, wsh_ref[...], preferred_element_type=F32)
          + jnp.dot(h_lo, wsh_ref[...], preferred_element_type=F32)
          + jnp.dot(h, wsl_ref[...], preferred_element_type=F32))
    lane = lax.broadcasted_iota(I32, ps.shape, 1)
    ms = jnp.sum(jnp.where(lane < SM_WI, ps * ps, 0.0), axis=-1, keepdims=True) * (1.0 / D_IDX)
    kin = ps * lax.rsqrt(ms + EPS) * sg_ref[...]
    z = ps + sb_ref[...]
    ls = jnp.minimum(z, 0.0) - jnp.log1p(jnp.exp(-jnp.abs(z)))
    sm_ref[...] = jnp.where(lane < SM_WI, kin,
                            jnp.where(lane < SM_IG, ps,
                                      jnp.where(lane < SM_LF, z,
                                                jnp.where(lane < SM_END, ls, 0.0))))

    for hd, qh in enumerate(_head_norm(col(0), gq_ref[...], N_A, HD_A)):
        q_ref[:, hd * HD_A:(hd + 1) * HD_A] = (qh * (HD_A ** -0.5 * LOG2E)).astype(BF16)
    for hd, kh in enumerate(_head_norm(col(1), gk_ref[...], N_A, HD_A)):
        kf_ref[:, hd * HD_A:(hd + 1) * HD_A] = kh
        k16_ref[:, hd * HD_A:(hd + 1) * HD_A] = kh.astype(BF16)
    p = col(2)
    vf_ref[...] = p
    v16_ref[...] = p.astype(BF16)
    qi_ref[...] = col(3).astype(BF16)


def _inproj_b_body(x_ref, g_ref, w_ref, qb_ref, kb_ref, vb_ref, og_ref):
    h = _rms(x_ref[...], g_ref[...]).astype(BF16)
    col = lambda c: jnp.dot(h, w_ref[:, c * D_B:(c + 1) * D_B], preferred_element_type=F32)
    qb_ref[...] = col(0).astype(BF16)
    kb_ref[...] = (col(1) * HD_B ** -0.5).astype(BF16)
    vb_ref[...] = col(2).astype(BF16)
    og_ref[...] = jax.nn.sigmoid(col(3)).astype(BF16)


def _inproj(x, g, wa, wb, wsh, wsl, gq, gk, sg, sb):
    n = x.shape[0]
    tm = min(PROJ_TOK_TILE, n)
    tok = lambda w: pl.BlockSpec((tm, w), lambda i: (i, 0))
    const = lambda a: pl.BlockSpec(a.shape, lambda i: (0,) * a.ndim)
    wide = lambda dt: jax.ShapeDtypeStruct((n, D_A), dt)
    outs_a = pl.pallas_call(
        _inproj_a_body,
        grid=(n // tm,),
        in_specs=[tok(D_MODEL)] + [const(a) for a in (g, wa, wsh, wsl, gq, gk, sg, sb)],
        out_specs=[tok(D_A)] * 6 + [tok(LANES)],
        out_shape=[wide(BF16), wide(F32), wide(BF16), wide(F32), wide(BF16), wide(BF16),
                   jax.ShapeDtypeStruct((n, LANES), F32)],
        compiler_params=_cparams(("parallel",)),
    )(x, g, wa, wsh, wsl, gq, gk, sg, sb)
    outs_b = pl.pallas_call(
        _inproj_b_body,
        grid=(n // tm,),
        in_specs=[tok(D_MODEL), const(g), const(wb)],
        out_specs=[tok(D_B)] * 4,
        out_shape=[wide(BF16)] * 4,
        compiler_params=_cparams(("parallel",)),
    )(x, g, wb)
    q, kf, k16, vf, v16, qi, sm = outs_a
    return (q, kf, k16, vf, v16, qi) + tuple(outs_b) + (sm,)


def _prep_inproj(g_mix, w_in, g_q, g_k, g_kidx, b_i, b_f):
    o_ki = 4 * D_A
    o_qb = o_ki + D_IDX + H_IDX
    o_ib = o_qb + 4 * D_B
    wa = w_in[:, :o_ki].astype(BF16)
    wb = w_in[:, o_qb:o_ib].astype(BF16)
    ws = jnp.concatenate([w_in[:, o_ki:o_qb], w_in[:, o_ib:],
                          jnp.zeros((D_MODEL, LANES - SM_END), F32)], axis=1)
    wsh = ws.astype(BF16)
    wsl = (ws - wsh.astype(F32)).astype(BF16)
    sg = jnp.concatenate([g_kidx, jnp.ones((LANES - D_IDX,), F32)]).reshape(1, LANES)
    sb = jnp.concatenate([jnp.zeros((SM_IG,), F32), b_i, b_f,
                          jnp.zeros((LANES - SM_END,), F32)]).reshape(1, LANES)
    return (g_mix.reshape(1, D_MODEL), wa, wb, wsh, wsl,
            g_q.reshape(1, HD_A), g_k.reshape(1, HD_A), sg, sb)


def _outproj_body(x_ref, a_ref, h_ref, wa_ref, wh_ref, o_ref):
    o_ref[...] = (x_ref[...]
                  + jnp.dot(a_ref[...], wa_ref[...], preferred_element_type=F32)
                  + jnp.dot(h_ref[...], wh_ref[...], preferred_element_type=F32))


def _outproj(x, a, h, wa, wh):
    n = x.shape[0]
    tm = min(TOK_TILE, n)
    return pl.pallas_call(
        _outproj_body,
        grid=(n // tm,),
        in_specs=[pl.BlockSpec((tm, D_MODEL), lambda i: (i, 0)),
                  pl.BlockSpec((tm, D_A), lambda i: (i, 0)),
                  pl.BlockSpec((tm, D_B), lambda i: (i, 0)),
                  pl.BlockSpec((D_A, D_MODEL), lambda i: (0, 0)),
                  pl.BlockSpec((D_B, D_MODEL), lambda i: (0, 0))],
        out_specs=pl.BlockSpec((tm, D_MODEL), lambda i: (i, 0)),
        out_shape=jax.ShapeDtypeStruct((n, D_MODEL), F32),
        compiler_params=_cparams(("parallel",)),
    )(x, a, h, wa, wh)


def _bucket_thresholds():
    nb = NUM_BUCKETS // 2
    max_exact = nb // 2
    span = nb - max_exact
    ratio = MAX_DISTANCE // max_exact
    out = []
    for m in range(1, span):
        n = max_exact
        while n ** span < max_exact ** span * ratio ** m:
            n += 1
        out.append(n)
    return tuple(out)


def _bias_body(tbl_ref, o_ref):
    o = pl.program_id(0)
    h = pl.program_id(1)
    nb = NUM_BUCKETS // 2
    max_exact = nb // 2
    row = lax.broadcasted_iota(I32, (SUB, SUB), 0)
    col = lax.broadcasted_iota(I32, (SUB, SUB), 1)
    rel = row - o * SUB - col
    n = jnp.abs(rel)
    large = jnp.full((SUB, SUB), max_exact, I32)
    for t in _bucket_thresholds():
        large = large + (n >= t).astype(I32)
    bucket = jnp.where(rel > 0, nb, 0) + jnp.where(n < max_exact, n, large)
    val = jnp.zeros((SUB, SUB), F32)
    for bk in range(NUM_BUCKETS):
        val = jnp.where(bucket == bk, tbl_ref[bk, h], val)
    o_ref[0, 0] = (val - tbl_ref[nb - 1, h]) * LOG2E


def _bias_tiles(rel_bias):
    return pl.pallas_call(
        _bias_body,
        grid=(2, N_A),
        in_specs=[pl.BlockSpec(memory_space=pltpu.SMEM)],
        out_specs=pl.BlockSpec((1, 1, SUB, SUB), lambda o, h: (o, h, 0, 0)),
        out_shape=jax.ShapeDtypeStruct((2, N_A, SUB, SUB), F32),
    )(rel_bias)


WCH = 4
A_UNROLL = 4
FAR_SUBS = 2


def _bit(k):
    return INT_MIN if k == 31 else 1 << k


def _bit_transpose32(a):
    a = list(a)
    j, m = 16, 0x0000FFFF
    while j:
        for k in range(32):
            if not k & j:
                t = (lax.shift_right_logical(a[k], jnp.int32(j)) ^ a[k + j]) & m
                a[k + j] = a[k + j] ^ t
                a[k] = a[k] ^ jnp.left_shift(t, jnp.int32(j))
        j >>= 1
        m ^= m << j
    return a


def _dsa_body(ti_ref, tk_ref, tn_ref, tl_ref,
              qt_ref, qit_ref, wit_ref, ki_ref, k_ref, vt_ref, bias_ref, o_ref,
              plane_ref, adm_ref, cand_ref, sel_ref, mb_ref, m_ref, l_ref, acc_ref, lg_ref, al_ref,
              *, tq, tk, q_off, l_valid, topk):
    s = pl.program_id(1)
    i = ti_ref[s]
    kt = tk_ref[s]
    nsub = tn_ref[s]
    subs = tk // SUB
    q_pos0 = q_off + i * tq
    i_sub = q_off // SUB + (i * tq) // SUB

    def fold(x, op, rows=8):
        parts = [x[r:r + rows] for r in range(0, x.shape[0], rows)]
        while len(parts) > 1:
            parts = [op(parts[a], parts[a + 1]) for a in range(0, len(parts), 2)]
        return parts[0]

    @pl.when(kt == 0)
    def _():
        m_ref[...] = jnp.full(m_ref.shape, NEG, F32)
        l_ref[...] = jnp.zeros(l_ref.shape, F32)
        acc_ref[...] = jnp.zeros(acc_ref.shape, F32)

        qchunk = (q_pos0 + lax.broadcasted_iota(I32, (SUB, tq), 1)) >> 6
        krow = lax.broadcasted_iota(I32, (SUB, tq), 0)

        def score_codes(t):
            off = pl.multiple_of(t * SUB, SUB)
            kit = ki_ref[0, pl.ds(off, SUB), :]
            sc = jnp.zeros((SUB, tq), F32)
            for j in range(H_IDX):
                sj = jnp.dot(kit, qit_ref[0, j * D_IDX:(j + 1) * D_IDX, :], preferred_element_type=F32)
                sc = sc + jnp.maximum(sj, 0.0) * wit_ref[0, j:j + 1, :]
            bits = lax.bitcast_convert_type(sc, I32)
            return bits ^ ((bits >> 31) | INT_MIN)

        def slice_codes(t, code, all_admissible):
            planes = _bit_transpose32([code[8 * k:8 * k + 8] for k in range(32)])
            row = pl.multiple_of(t * 8, 8)
            for k in range(32):
                plane_ref[k, pl.ds(row, 8), :] = planes[k]
            if all_admissible:
                adm_ref[pl.ds(row, 8), :] = jnp.full((8, tq), -1, I32)
            else:
                kpos = t * SUB + krow
                adm = ((kpos >> 6) <= qchunk) & (kpos < l_valid)
                aw = jnp.zeros((8, tq), I32)
                for k in range(32):
                    aw = aw | jnp.where(adm[8 * k:8 * k + 8], _bit(k), 0)
                adm_ref[pl.ds(row, 8), :] = aw

        n_open = jnp.minimum(l_valid, ((q_pos0 >> 6) + 1) * CHUNK) // SUB

        def score_tiles(gi, c):
            for u in range(A_UNROLL):
                slice_codes(gi * A_UNROLL + u, score_codes(gi * A_UNROLL + u), True)
            return c

        def score_tile(t, c):
            slice_codes(t, score_codes(t), False)
            return c

        lax.fori_loop(0, n_open // A_UNROLL, score_tiles, 0)
        lax.fori_loop(n_open // A_UNROLL * A_UNROLL, nsub, score_tile, 0)

        nchunk = (nsub + WCH - 1) // WCH

        def clear_tile(t, c):
            row = pl.multiple_of(t * 8, 8)
            plane_ref[:, pl.ds(row, 8), :] = jnp.zeros((32, 8, tq), I32)
            adm_ref[pl.ds(row, 8), :] = jnp.zeros((8, tq), I32)
            return c

        lax.fori_loop(nsub, nchunk * WCH, clear_tile, 0)

        def chunk_rows(cn):
            return pl.ds(pl.multiple_of(cn * (8 * WCH), 8 * WCH), 8 * WCH)

        def init_chunk(cn, c):
            cand_ref[chunk_rows(cn), :] = adm_ref[chunk_rows(cn), :]
            sel_ref[chunk_rows(cn), :] = jnp.zeros((8 * WCH, tq), I32)
            return c

        lax.fori_loop(0, nchunk, init_chunk, 0)

        def lane_total(cnt):
            return jnp.sum(fold(cnt.astype(F32), jnp.add), axis=0, keepdims=True).astype(I32)

        def count_cand(plane_of):
            def count_chunk(cn, cnt):
                ones = cand_ref[chunk_rows(cn), :] & plane_of(cn)
                return cnt + lax.population_count(ones)

            return lane_total(lax.fori_loop(0, nchunk, count_chunk, jnp.zeros((8 * WCH, tq), I32)))

        def radix_step(plane_of, n_sel, n_one, next_plane_of):
            take = (n_sel + n_one) >= topk

            def update_chunk(cn, cnt):
                cw = cand_ref[chunk_rows(cn), :]
                pw = plane_of(cn)
                ones = cw & pw
                kept = jnp.where(take, ones, cw & ~pw)
                cand_ref[chunk_rows(cn), :] = kept
                sel_ref[chunk_rows(cn), :] = sel_ref[chunk_rows(cn), :] | jnp.where(take, 0, ones)
                return cnt + lax.population_count(kept & next_plane_of(cn))

            cnt = lax.fori_loop(0, nchunk, update_chunk, jnp.zeros((8 * WCH, tq), I32))
            return jnp.where(take, n_sel, n_sel + n_one), lane_total(cnt)

        def code_plane(b):
            return lambda cn: plane_ref[b, chunk_rows(cn), :]

        def code_step(it, st):
            return radix_step(code_plane(31 - it), st[0], st[1], code_plane(jnp.maximum(30 - it, 0)))

        n_sel, _ = lax.fori_loop(0, 32, code_step, (jnp.zeros((1, tq), I32), count_cand(code_plane(31))))

        n_tie = count_cand(lambda cn: jnp.int32(-1))
        crowded = jnp.max(jnp.where((n_sel + n_tie > topk) & (n_tie > 1), 1, 0))

        @pl.when(crowded > 0)
        def _():
            wrow = lax.broadcasted_iota(I32, (8 * WCH, tq), 0)
            index_planes = []
            for tb in reversed(range(max(1, (plane_ref.shape[1] // 8 - 1).bit_length()))):
                index_planes.append(lambda cn, tb=tb: (((cn * WCH + (wrow >> 3)) >> tb) & 1) - 1)
            for low in (0x0000FFFF, 0x00FF00FF, 0x0F0F0F0F, 0x33333333, 0x55555555):
                index_planes.append(lambda cn, low=low: jnp.int32(low))
            for rb in (2, 1, 0):
                index_planes.append(lambda cn, rb=rb: ((wrow >> rb) & 1) - 1)
            index_planes.append(index_planes[-1])
            n, n_one = n_sel, count_cand(index_planes[0])
            for this_plane, next_plane in zip(index_planes[:-1], index_planes[1:]):
                n, n_one = radix_step(this_plane, n, n_one, next_plane)

        def final_chunk(cn, c):
            sel_ref[chunk_rows(cn), :] = sel_ref[chunk_rows(cn), :] | cand_ref[chunk_rows(cn), :]
            return c

        lax.fori_loop(0, nchunk, final_chunk, 0)

    n_here = jnp.clip(nsub - kt * subs, 0, subs)

    def key_block(u, nsubs, near):
        nk = nsubs * SUB
        g = kt * subs + u
        dsub = g - i_sub
        uoff = pl.multiple_of(u * SUB, SUB)
        for a in range(nsubs):
            sw = sel_ref[pl.ds(pl.multiple_of((g + a) * 8, 8), 8), :]
            for k in range(32):
                r = a * SUB + 8 * k
                mb_ref[r:r + 8, :] = jnp.where((sw & _bit(k)) != 0, 0.0, 2.0 * NEG)
        def logits_sweep(h):
            hs = slice(h * HD_A, (h + 1) * HD_A)
            kh = k_ref[0, pl.ds(uoff, nk), hs]
            lg = jnp.dot(kh, qt_ref[0, hs, :], preferred_element_type=F32)
            if near:
                lg = lg + bias_ref[-dsub, h, :, 0:tq]
            lg = lg + mb_ref[0:nk, :]
            lg_ref[h, 0:nk, :] = lg
            m_old = m_ref[h]
            m_new = jnp.maximum(m_old, jnp.max(fold(lg, jnp.maximum), axis=0, keepdims=True))
            al_ref[h] = jnp.exp2(m_old - m_new)
            m_ref[h] = m_new

        def values_sweep(h):
            hs = slice(h * HD_A, (h + 1) * HD_A)
            vth = vt_ref[0, hs, pl.ds(uoff, nk)]
            alpha = al_ref[h]
            p = jnp.exp2((lg_ref[h, 0:nk, :] - m_ref[h][0:1, :]).astype(BF16))
            pv = jnp.dot(jnp.concatenate([vth, jnp.ones((16, nk), BF16)], axis=0), p,
                         preferred_element_type=F32)
            l_ref[h] = alpha * l_ref[h] + pv[HD_A:HD_A + 8, :]
            acc_ref[hs, :] = alpha[0:1, :] * acc_ref[hs, :] + pv[0:HD_A, :]

        for h in range(N_A):
            logits_sweep(h)
        for h in range(N_A):
            values_sweep(h)

    n_far = jnp.clip(i_sub - 1 - kt * subs, 0, n_here)

    def near_one(u, c):
        key_block(u, 1, True)
        return c

    done = 0
    width = FAR_SUBS
    while width > subs:
        width //= 2
    while width >= 1:
        def far_block(bi, c, width=width, done=done):
            key_block(done + bi * width, width, False)
            return c

        count = (n_far - done) // width
        lax.fori_loop(0, count, far_block, 0)
        done = done + count * width
        width //= 2
    lax.fori_loop(n_far, n_here, near_one, 0)

    @pl.when(tl_ref[s] == 1)
    def _():
        for h in range(N_A):
            hs = slice(h * HD_A, (h + 1) * HD_A)
            o_ref[0, hs, :] = (acc_ref[hs, :] / l_ref[h][0:1, :]).astype(BF16)


def _dsa(qt, qit, wit, ki, k, vt, bias, *, tq, tk, q_off, l_valid, topk):
    bsz, sq = qt.shape[0], qt.shape[2]
    sk = k.shape[1]
    assert sq % tq == 0 and sk % tk == 0 and tk % SUB == 0 and q_off % SUB == 0
    assert (tq % SUB == 0 or sq == tq) and tq % LANES == 0
    subs = tk // SUB
    wrows = -(-(sk // SUB) // WCH) * WCH * 8
    ti, tkk, tn, tl = [], [], [], []
    for i in range(sq // tq):
        q_last = q_off + (i + 1) * tq - 1
        lim = min(l_valid, (q_last // CHUNK + 1) * CHUNK)
        nsub = -(-lim // SUB)
        nkt = -(-nsub // subs)
        for t in range(nkt):
            ti.append(i), tkk.append(t), tn.append(nsub), tl.append(int(t == nkt - 1))
    tabs = [jnp.asarray(np.asarray(a, np.int32)) for a in (ti, tkk, tn, tl)]
    body = functools.partial(_dsa_body, tq=tq, tk=tk, q_off=q_off, l_valid=l_valid, topk=topk)
    qmap = lambda b, s, ti, tk_, tn, tl: (b, 0, ti[s])
    grid_spec = pltpu.PrefetchScalarGridSpec(
        num_scalar_prefetch=4,
        grid=(bsz, len(ti)),
        in_specs=[pl.BlockSpec((1, D_A, tq), qmap),
                  pl.BlockSpec((1, D_QI, tq), qmap),
                  pl.BlockSpec((1, H_IDX, tq), qmap),
                  pl.BlockSpec((1, sk, D_IDX), lambda b, s, *_: (b, 0, 0)),
                  pl.BlockSpec((1, tk, D_A), lambda b, s, ti, tk_, tn, tl: (b, tk_[s], 0)),
                  pl.BlockSpec((1, D_A, tk), lambda b, s, ti, tk_, tn, tl: (b, 0, tk_[s])),
                  pl.BlockSpec((2, N_A, SUB, SUB), lambda b, s, *_: (0, 0, 0, 0))],
        out_specs=pl.BlockSpec((1, D_A, tq), qmap),
        scratch_shapes=[pltpu.VMEM((32, wrows, tq), I32),
                        pltpu.VMEM((wrows, tq), I32),
                        pltpu.VMEM((wrows, tq), I32),
                        pltpu.VMEM((wrows, tq), I32),
                        pltpu.VMEM((FAR_SUBS * SUB, tq), F32),
                        pltpu.VMEM((N_A, 8, tq), F32),
                        pltpu.VMEM((N_A, 8, tq), F32),
                        pltpu.VMEM((D_A, tq), F32),
                        pltpu.VMEM((N_A, FAR_SUBS * SUB, tq), F32),
                        pltpu.VMEM((N_A, 8, tq), F32)])
    return pl.pallas_call(
        body,
        grid_spec=grid_spec,
        out_shape=jax.ShapeDtypeStruct((bsz, D_A, sq), BF16),
        compiler_params=_cparams(("parallel", "arbitrary")),
    )(*tabs, qt, qit, wit, ki, k, vt, bias)


def _pack_cache_body(k4_ref, v4_ref, kn_ref, vnt_ref, ko_ref, vto_ref, *, ncache):
    c = pl.program_id(1)

    @pl.when(c < ncache)
    def _():
        for h in range(N_A):
            hs = slice(h * HD_A, (h + 1) * HD_A)
            ko_ref[0, :, hs] = k4_ref[0, pl.ds(h, SUB, stride=N_A), :].astype(BF16)
            vto_ref[0, hs, :] = v4_ref[0, pl.ds(h, SUB, stride=N_A), :].T.astype(BF16)

    @pl.when(c == ncache)
    def _():
        ko_ref[0] = kn_ref[0]
        vto_ref[0] = vnt_ref[0]


def _pack_cache(cache_k, cache_v, k_new, vt_new):
    bsz, past = cache_k.shape[0], cache_k.shape[1]
    ncache = past // SUB
    cmap = lambda b, c: (b, jnp.minimum(c, ncache - 1), 0)
    rows = lambda a: a.reshape(bsz, past * N_A, HD_A)
    return pl.pallas_call(
        functools.partial(_pack_cache_body, ncache=ncache),
        grid=(bsz, ncache + 1),
        in_specs=[pl.BlockSpec((1, SUB * N_A, HD_A), cmap),
                  pl.BlockSpec((1, SUB * N_A, HD_A), cmap),
                  pl.BlockSpec((1, SUB, D_A), lambda b, c: (b, 0, 0)),
                  pl.BlockSpec((1, D_A, SUB), lambda b, c: (b, 0, 0))],
        out_specs=[pl.BlockSpec((1, SUB, D_A), lambda b, c: (b, c, 0)),
                   pl.BlockSpec((1, D_A, SUB), lambda b, c: (b, 0, c))],
        out_shape=[jax.ShapeDtypeStruct((bsz, past + SUB, D_A), BF16),
                   jax.ShapeDtypeStruct((bsz, D_A, past + SUB), BF16)],
        compiler_params=_cparams(("parallel", "arbitrary")),
    )(rows(cache_k), rows(cache_v), k_new, vt_new)


def _mlstm_body(q_ref, k_ref, kt_ref, v_ref, og_ref, sm_ref, gr_ref, gh_ref, c0_ref, n0_ref, m0_ref,
                hb_ref, co_ref, no_ref, mo_ref, c_s, n_s, m_s, *, lc):
    c = pl.program_id(1)

    @pl.when(c == 0)
    def _():
        c_s[...] = c0_ref[0]
        for h in range(N_B):
            n_s[h] = jnp.broadcast_to(n0_ref[0, h:h + 1, :], (8, HD_B))
            m_s[h] = jnp.broadcast_to(m0_ref[0, h:h + 1, :], (8, LANES))

    row = lax.broadcasted_iota(I32, (lc, lc), 0)
    col = lax.broadcasted_iota(I32, (lc, lc), 1)
    causal = col <= row
    sm = sm_ref[0]
    gr = gr_ref[0]
    hp = lax.Precision.HIGHEST
    b_cols = jnp.dot(causal.astype(F32), sm, precision=hp, preferred_element_type=F32)
    b_rows = jnp.dot(gr, (row <= col).astype(F32), precision=hp, preferred_element_type=F32)

    for h in range(N_B):
        hs = slice(h * HD_B, (h + 1) * HD_B)
        m_prev = m_s[h][0:1, 0:1]
        bc = b_cols[:, SM_LF + h:SM_LF + h + 1]
        ig_c = sm[:, SM_IG + h:SM_IG + h + 1]
        br = b_rows[N_B + h:N_B + h + 1, :]
        ig_r = gr[h:h + 1, :]
        log_d = jnp.where(causal, bc - br + ig_r, NEG)
        log_inter = bc + m_prev
        m_t = jnp.maximum(log_inter, jnp.max(log_d, axis=1, keepdims=True))
        d = jnp.exp(log_d - m_t)
        inter = jnp.exp(log_inter - m_t)
        qh = q_ref[0, :, hs]
        kth = kt_ref[0, hs, :]
        vh = v_ref[0, :, hs]
        sc = jnp.dot(qh, kth, preferred_element_type=F32) * d
        ch = c_s[h]
        nrow = n_s[h][0:1, :]
        num = (jnp.dot(sc.astype(BF16), vh, preferred_element_type=F32)
               + inter * jnp.dot(qh, ch.astype(BF16), preferred_element_type=F32))
        den = (jnp.sum(sc, axis=1, keepdims=True)
               + inter * jnp.sum(qh.astype(F32) * nrow, axis=1, keepdims=True))
        hh = num / jnp.maximum(jnp.abs(den), jnp.exp(-m_t))
        hb = _rms(hh, gh_ref[...]) * og_ref[0, :, hs].astype(F32)
        hb_ref[0, :, hs] = hb.astype(BF16)

        m_new = m_t[lc - 1:lc, :]
        b_last = bc[lc - 1:lc, :]
        decay = jnp.exp(b_last + m_prev - m_new)
        w_r = jnp.exp(b_last - br + ig_r - m_new)
        w_c = jnp.exp(b_last - bc + ig_c - m_new)
        kw = (kth.astype(F32) * w_r).astype(BF16)
        c_s[h] = decay * ch + jnp.dot(kw, vh, preferred_element_type=F32)
        n_new = decay * nrow + jnp.sum(k_ref[0, :, hs].astype(F32) * w_c, axis=0, keepdims=True)
        n_s[h] = jnp.broadcast_to(n_new, (8, HD_B))
        m_s[h] = jnp.broadcast_to(m_new, (8, LANES))

    @pl.when(c == pl.num_programs(1) - 1)
    def _():
        co_ref[0] = c_s[...]
        for h in range(N_B):
            no_ref[0, h:h + 1, :] = n_s[h][0:1, :]
            mo_ref[0, h:h + 1, :] = m_s[h][0:1, :]


def _mlstm(q, k, kt, v, og, sm, gr, g_h, c0, n0, m0, *, lc):
    bsz, s = q.shape[0], q.shape[1]
    tokb = lambda w: pl.BlockSpec((1, lc, w), lambda b, c: (b, c, 0))
    st3 = lambda w: pl.BlockSpec((1, N_B, w), lambda b, c: (b, 0, 0))
    st4 = pl.BlockSpec((1, N_B, HD_B, HD_B), lambda b, c: (b, 0, 0, 0))
    return pl.pallas_call(
        functools.partial(_mlstm_body, lc=lc),
        grid=(bsz, s // lc),
        in_specs=[tokb(D_B), tokb(D_B),
                  pl.BlockSpec((1, D_B, lc), lambda b, c: (b, 0, c)),
                  tokb(D_B), tokb(D_B), tokb(LANES),
                  pl.BlockSpec((1, 2 * N_B, lc), lambda b, c: (b, 0, c)),
                  pl.BlockSpec((1, HD_B), lambda b, c: (0, 0)),
                  st4, st3(HD_B), st3(LANES)],
        out_specs=[tokb(D_B), st4, st3(HD_B), st3(LANES)],
        out_shape=[jax.ShapeDtypeStruct((bsz, s, D_B), BF16),
                   jax.ShapeDtypeStruct((bsz, N_B, HD_B, HD_B), F32),
                   jax.ShapeDtypeStruct((bsz, N_B, HD_B), F32),
                   jax.ShapeDtypeStruct((bsz, N_B, LANES), F32)],
        scratch_shapes=[pltpu.VMEM((N_B, HD_B, HD_B), F32),
                        pltpu.VMEM((N_B, 8, HD_B), F32),
                        pltpu.VMEM((N_B, 8, LANES), F32)],
        compiler_params=_cparams(("parallel", "arbitrary")),
    )(q, k, kt, v, og, sm, gr, g_h, c0, n0, m0)


def _mixer_front(x, ffn1, inp):
    x1 = _ffn(x, *ffn1)
    return (x1,) + tuple(_inproj(x1, *inp))


def kernel(x_prompt, x_sample, cache_k, cache_v, cache_k_idx, state_C, state_n, state_m, g_ffn1, w1_ffn1, w3_ffn1, w2_ffn1, g_mix, w_in, g_q, g_k, g_kidx, rel_bias, b_i, b_f, g_h, w_out, g_ffn2, w1_ffn2, w3_ffn2, w2_ffn2):
    bsz, seq = x_prompt.shape[0], x_prompt.shape[1]
    dbs, dseq = x_sample.shape[0], x_sample.shape[1]
    past = cache_k.shape[1]

    ffn1 = _prep_ffn(g_ffn1, w1_ffn1, w3_ffn1, w2_ffn1)
    ffn2 = _prep_ffn(g_ffn2, w1_ffn2, w3_ffn2, w2_ffn2)
    inp = _prep_inproj(g_mix, w_in, g_q, g_k, g_kidx, b_i, b_f)
    wa = w_out[:D_A].astype(BF16)
    wh = w_out[D_A:].astype(BF16)
    gh = g_h.reshape(1, HD_B)
    bias = _bias_tiles(rel_bias)

    n_p = bsz * seq
    x1, q, kf, k16, vf, v16, qi, qb, kb, vb, og, sm = _mixer_front(x_prompt.reshape(n_p, D_MODEL), ffn1, inp)
    r3 = lambda a, b_, s_: a.reshape(b_, s_, a.shape[-1])
    sm3 = r3(sm, bsz, seq)
    tr = lambda a: jnp.swapaxes(a, 1, 2)
    tq_p = min(SUB, seq)
    attn = tr(_dsa(tr(r3(q, bsz, seq)), tr(r3(qi, bsz, seq)), tr(sm3[:, :, SM_WI:SM_IG]),
                   sm3[:, :, :D_IDX].astype(BF16), r3(k16, bsz, seq), tr(r3(v16, bsz, seq)), bias,
                   tq=tq_p, tk=min(1024, seq), q_off=0, l_valid=seq, topk=min(TOPK_MAX, seq // 4)))
    gr = jnp.swapaxes(sm3[:, :, SM_IG:SM_END], 1, 2)
    kb3 = r3(kb, bsz, seq)
    hb, c_p, n_p_, m_p = _mlstm(r3(qb, bsz, seq), kb3, jnp.swapaxes(kb3, 1, 2), r3(vb, bsz, seq),
                                r3(og, bsz, seq), sm3, gr, gh,
                                jnp.zeros((bsz, N_B, HD_B, HD_B), F32), jnp.zeros((bsz, N_B, HD_B), F32),
                                jnp.zeros((bsz, N_B, LANES), F32), lc=min(SUB, seq))
    x2 = _outproj(x1, attn.reshape(n_p, D_A), hb.reshape(n_p, D_B), wa, wh)
    y_prompt = _ffn(x2, *ffn2).reshape(bsz, seq, D_MODEL)

    n_s = dbs * dseq
    x1s, qs, kfs, k16s, vfs, v16s, qis, qbs, kbs, vbs, ogs, sms = _mixer_front(
        x_sample.reshape(n_s, D_MODEL), ffn1, inp)
    sms3 = r3(sms, dbs, dseq)
    l_all = past + dseq
    tk_s = 3 * SUB
    sk = -(-l_all // tk_s) * tk_s
    padk = lambda a: jnp.pad(a, ((0, 0), (0, sk - l_all), (0, 0)))
    assert past % SUB == 0 and sk - past == SUB
    padn = lambda a: jnp.pad(a, ((0, 0), (0, SUB - dseq), (0, 0)))
    k_all, vt_all = _pack_cache(cache_k, cache_v, padn(r3(k16s, dbs, dseq)), tr(padn(r3(v16s, dbs, dseq))))
    ki_all = padk(jnp.concatenate([cache_k_idx.astype(BF16), sms3[:, :, :D_IDX].astype(BF16)], axis=1))
    assert dseq <= LANES
    padq = lambda a: jnp.pad(tr(a), ((0, 0), (0, 0), (0, LANES - dseq)))
    attn_s = tr(_dsa(padq(r3(qs, dbs, dseq)), padq(r3(qis, dbs, dseq)), padq(sms3[:, :, SM_WI:SM_IG]),
                     ki_all, k_all, vt_all, bias,
                     tq=LANES, tk=tk_s, q_off=past, l_valid=l_all, topk=min(TOPK_MAX, l_all // 4))[:, :, :dseq])
    lc_s = LANES
    padt = lambda a: jnp.pad(a, ((0, 0), (0, lc_s - dseq), (0, 0)))
    lane = jnp.arange(LANES)
    sm_pad = jnp.where((lane >= SM_IG) & (lane < SM_LF), NEG, 0.0).astype(F32)
    sms_p = jnp.concatenate([sms3, jnp.broadcast_to(sm_pad, (dbs, lc_s - dseq, LANES))], axis=1)
    kbs3 = padt(r3(kbs, dbs, dseq))
    hbs, c_s, n_s_, m_s = _mlstm(padt(r3(qbs, dbs, dseq)), kbs3, jnp.swapaxes(kbs3, 1, 2), padt(r3(vbs, dbs, dseq)),
                                 padt(r3(ogs, dbs, dseq)), sms_p, jnp.swapaxes(sms_p[:, :, SM_IG:SM_END], 1, 2), gh,
                                 state_C.astype(F32), state_n.astype(F32),
                                 jnp.broadcast_to(state_m.astype(F32)[:, :, None], (dbs, N_B, LANES)), lc=lc_s)
    x2s = _outproj(x1s, attn_s.reshape(n_s, D_A), hbs[:, :dseq].reshape(n_s, D_B), wa, wh)
    y_sample = _ffn(x2s, *ffn2).reshape(dbs, dseq, D_MODEL)

    sd = state_C.dtype
    return (y_prompt, y_sample,
            kf.reshape(bsz, seq, N_A, HD_A), vf.reshape(bsz, seq, N_A, HD_A), sm3[:, :, :D_IDX],
            c_p.astype(sd), n_p_.astype(sd), m_p[:, :, 0].astype(sd),
            kfs.reshape(dbs, dseq, N_A, HD_A), vfs.reshape(dbs, dseq, N_A, HD_A), sms3[:, :, :D_IDX],
            c_s.astype(sd), n_s_.astype(sd), m_s[:, :, 0].astype(sd))
```

```python
import functools
import math

import numpy as np
import jax
import jax.numpy as jnp
from jax import lax
from jax.experimental import pallas as pl
from jax.experimental.pallas import tpu as pltpu

F32 = jnp.float32
BF16 = jnp.bfloat16
I32 = jnp.int32

D_MODEL = 2048
CHUNK = 64
N_A, HD_A = 8, 128
H_IDX, D_IDX = 16, 64
TOPK_MAX = 256
N_B, HD_B = 4, 256
D_FF = 5504
NUM_BUCKETS, MAX_DISTANCE = 32, 128
EPS = 1e-6
NEG = -1e30
INT_MIN = -(2 ** 31)
LOG2E = math.log2(math.e)

LANES = 128
D_A = N_A * HD_A
D_B = N_B * HD_B
D_QI = H_IDX * D_IDX
FF_TILE = 512
D_FF_PAD = -(-D_FF // FF_TILE) * FF_TILE
FFN_TOK_TILE = 512
PROJ_TOK_TILE = 256
TOK_TILE = 512
SUB = 256
MLSTM_CHUNK = 256
VMEM_LIMIT = 56 * 1024 * 1024

SM_WI, SM_IG, SM_LF = D_IDX, D_IDX + H_IDX, D_IDX + H_IDX + N_B
SM_END = SM_LF + N_B


def _cparams(sem):
    return pltpu.CompilerParams(dimension_semantics=sem, vmem_limit_bytes=VMEM_LIMIT)


def _rms(x, g):
    ms = jnp.mean(x * x, axis=-1, keepdims=True)
    return x * lax.rsqrt(ms + EPS) * g


def _ffn_body(x_ref, g_ref, w1_ref, w3_ref, w2_ref, o_ref, hn_ref):
    j = pl.program_id(1)

    @pl.when(j == 0)
    def _():
        hn_ref[...] = _rms(x_ref[...], g_ref[...]).astype(BF16)
        o_ref[...] = jnp.zeros(o_ref.shape, F32)

    h = hn_ref[...]
    a = jnp.dot(h, w1_ref[...], preferred_element_type=F32)
    b = jnp.dot(h, w3_ref[...], preferred_element_type=F32)
    u = (a * jax.nn.sigmoid(a) * b).astype(BF16)
    o_ref[...] += jnp.dot(u, w2_ref[...], preferred_element_type=F32)

    @pl.when(j == pl.num_programs(1) - 1)
    def _():
        o_ref[...] = x_ref[...] + 0.5 * o_ref[...]


def _ffn(x, g, w1, w3, w2):
    n = x.shape[0]
    tm = min(FFN_TOK_TILE, n)
    grid = (n // tm, D_FF_PAD // FF_TILE)
    return pl.pallas_call(
        _ffn_body,
        grid=grid,
        in_specs=[
            pl.BlockSpec((tm, D_MODEL), lambda i, j: (i, 0)),
            pl.BlockSpec((1, D_MODEL), lambda i, j: (0, 0)),
            pl.BlockSpec((D_MODEL, FF_TILE), lambda i, j: (0, j)),
            pl.BlockSpec((D_MODEL, FF_TILE), lambda i, j: (0, j)),
            pl.BlockSpec((FF_TILE, D_MODEL), lambda i, j: (j, 0)),
        ],
        out_specs=pl.BlockSpec((tm, D_MODEL), lambda i, j: (i, 0)),
        out_shape=jax.ShapeDtypeStruct((n, D_MODEL), F32),
        scratch_shapes=[pltpu.VMEM((tm, D_MODEL), BF16)],
        compiler_params=_cparams(("parallel", "arbitrary")),
    )(x, g, w1, w3, w2)


def _cast_pad_body(x_ref, o_ref, *, blocks_valid, cols_valid):
    x = jnp.where(pl.program_id(0) < blocks_valid, x_ref[...], 0.0)
    o_ref[:, :cols_valid] = x.astype(BF16)
    if cols_valid < o_ref.shape[1]:
        o_ref[:, cols_valid:] = jnp.zeros((o_ref.shape[0], o_ref.shape[1] - cols_valid), BF16)


def _cast_pad(w, rows, cols, block_rows):
    r, c = w.shape
    assert rows % block_rows == 0 and r % block_rows == 0 and c % LANES == 0 and cols >= c and rows >= r
    nvalid = r // block_rows
    body = functools.partial(_cast_pad_body, blocks_valid=nvalid, cols_valid=c)
    return pl.pallas_call(
        body,
        grid=(rows // block_rows,),
        in_specs=[pl.BlockSpec((block_rows, c), lambda i: (jnp.minimum(i, nvalid - 1), 0))],
        out_specs=pl.BlockSpec((block_rows, cols), lambda i: (i, 0)),
        out_shape=jax.ShapeDtypeStruct((rows, cols), BF16),
        compiler_params=_cparams(("parallel",)),
    )(w)


def _prep_ffn(g, w1, w3, w2):
    return (g.reshape(1, D_MODEL),
            _cast_pad(w1, D_MODEL, D_FF_PAD, 256),
            _cast_pad(w3, D_MODEL, D_FF_PAD, 256),
            _cast_pad(w2, D_FF_PAD, D_MODEL, LANES))


def _head_norm(p, g, heads, hd):
    outs = []
    for h in range(heads):
        outs.append(_rms(p[:, h * hd:(h + 1) * hd], g))
    return outs


def _inproj_a_body(x_ref, g_ref, w_ref, wsh_ref, wsl_ref, gq_ref, gk_ref, sg_ref, sb_ref,
                   q_ref, kf_ref, k16_ref, vf_ref, v16_ref, qi_ref, sm_ref):
    h32 = _rms(x_ref[...], g_ref[...])
    h = h32.astype(BF16)
    col = lambda c: jnp.dot(h, w_ref[:, c * D_A:(c + 1) * D_A], preferred_element_type=F32)

    h_lo = (h32 - h.astype(F32)).astype(BF16)
    ps = (jnp.dot(h, wsh_ref[...], preferred_element_type=F32)
          + jnp.dot(h_lo, wsh_ref[...], preferred_element_type=F32)
          + jnp.dot(h, wsl_ref[...], preferred_element_type=F32))
    lane = lax.broadcasted_iota(I32, ps.shape, 1)
    ms = jnp.sum(jnp.where(lane < SM_WI, ps * ps, 0.0), axis=-1, keepdims=True) * (1.0 / D_IDX)
    kin = ps * lax.rsqrt(ms + EPS) * sg_ref[...]
    z = ps + sb_ref[...]
    ls = jnp.minimum(z, 0.0) - jnp.log1p(jnp.exp(-jnp.abs(z)))
    sm_ref[...] = jnp.where(lane < SM_WI, kin,
                            jnp.where(lane < SM_IG, ps,
                                      jnp.where(lane < SM_LF, z,
                                                jnp.where(lane < SM_END, ls, 0.0))))

    for hd, qh in enumerate(_head_norm(col(0), gq_ref[...], N_A, HD_A)):
        q_ref[:, hd * HD_A:(hd + 1) * HD_A] = (qh * (HD_A ** -0.5 * LOG2E)).astype(BF16)
    for hd, kh in enumerate(_head_norm(col(1), gk_ref[...], N_A, HD_A)):
        kf_ref[:, hd * HD_A:(hd + 1) * HD_A] = kh
        k16_ref[:, hd * HD_A:(hd + 1) * HD_A] = kh.astype(BF16)
    p = col(2)
    vf_ref[...] = p
    v16_ref[...] = p.astype(BF16)
    qi_ref[...] = col(3).astype(BF16)


def _inproj_b_body(x_ref, g_ref, w_ref, qb_ref, kb_ref, vb_ref, og_ref):
    h = _rms(x_ref[...], g_ref[...]).astype(BF16)
    col = lambda c: jnp.dot(h, w_ref[:, c * D_B:(c + 1) * D_B], preferred_element_type=F32)
    qb_ref[...] = col(0).astype(BF16)
    kb_ref[...] = (col(1) * HD_B ** -0.5).astype(BF16)
    vb_ref[...] = col(2).astype(BF16)
    og_ref[...] = jax.nn.sigmoid(col(3)).astype(BF16)


def _inproj(x, g, wa, wb, wsh, wsl, gq, gk, sg, sb):
    n = x.shape[0]
    tm = min(PROJ_TOK_TILE, n)
    tok = lambda w: pl.BlockSpec((tm, w), lambda i: (i, 0))
    const = lambda a: pl.BlockSpec(a.shape, lambda i: (0,) * a.ndim)
    wide = lambda dt: jax.ShapeDtypeStruct((n, D_A), dt)
    outs_a = pl.pallas_call(
        _inproj_a_body,
        grid=(n // tm,),
        in_specs=[tok(D_MODEL)] + [const(a) for a in (g, wa, wsh, wsl, gq, gk, sg, sb)],
        out_specs=[tok(D_A)] * 6 + [tok(LANES)],
        out_shape=[wide(BF16), wide(F32), wide(BF16), wide(F32), wide(BF16), wide(BF16),
                   jax.ShapeDtypeStruct((n, LANES), F32)],
        compiler_params=_cparams(("parallel",)),
    )(x, g, wa, wsh, wsl, gq, gk, sg, sb)
    outs_b = pl.pallas_call(
        _inproj_b_body,
        grid=(n // tm,),
        in_specs=[tok(D_MODEL), const(g), const(wb)],
        out_specs=[tok(D_B)] * 4,
        out_shape=[wide(BF16)] * 4,
        compiler_params=_cparams(("parallel",)),
    )(x, g, wb)
    q, kf, k16, vf, v16, qi, sm = outs_a
    return (q, kf, k16, vf, v16, qi) + tuple(outs_b) + (sm,)


def _prep_inproj(g_mix, w_in, g_q, g_k, g_kidx, b_i, b_f):
    o_ki = 4 * D_A
    o_qb = o_ki + D_IDX + H_IDX
    o_ib = o_qb + 4 * D_B
    wa = w_in[:, :o_ki].astype(BF16)
    wb = w_in[:, o_qb:o_ib].astype(BF16)
    ws = jnp.concatenate([w_in[:, o_ki:o_qb], w_in[:, o_ib:],
                          jnp.zeros((D_MODEL, LANES - SM_END), F32)], axis=1)
    wsh = ws.astype(BF16)
    wsl = (ws - wsh.astype(F32)).astype(BF16)
    sg = jnp.concatenate([g_kidx, jnp.ones((LANES - D_IDX,), F32)]).reshape(1, LANES)
    sb = jnp.concatenate([jnp.zeros((SM_IG,), F32), b_i, b_f,
                          jnp.zeros((LANES - SM_END,), F32)]).reshape(1, LANES)
    return (g_mix.reshape(1, D_MODEL), wa, wb, wsh, wsl,
            g_q.reshape(1, HD_A), g_k.reshape(1, HD_A), sg, sb)


def _outproj_body(x_ref, a_ref, h_ref, wa_ref, wh_ref, o_ref):
    o_ref[...] = (x_ref[...]
                  + jnp.dot(a_ref[...], wa_ref[...], preferred_element_type=F32)
                  + jnp.dot(h_ref[...], wh_ref[...], preferred_element_type=F32))


def _outproj(x, a, h, wa, wh):
    n = x.shape[0]
    tm = min(TOK_TILE, n)
    return pl.pallas_call(
        _outproj_body,
        grid=(n // tm,),
        in_specs=[pl.BlockSpec((tm, D_MODEL), lambda i: (i, 0)),
                  pl.BlockSpec((tm, D_A), lambda i: (i, 0)),
                  pl.BlockSpec((tm, D_B), lambda i: (i, 0)),
                  pl.BlockSpec((D_A, D_MODEL), lambda i: (0, 0)),
                  pl.BlockSpec((D_B, D_MODEL), lambda i: (0, 0))],
        out_specs=pl.BlockSpec((tm, D_MODEL), lambda i: (i, 0)),
        out_shape=jax.ShapeDtypeStruct((n, D_MODEL), F32),
        compiler_params=_cparams(("parallel",)),
    )(x, a, h, wa, wh)


def _bucket_thresholds():
    nb = NUM_BUCKETS // 2
    max_exact = nb // 2
    span = nb - max_exact
    ratio = MAX_DISTANCE // max_exact
    out = []
    for m in range(1, span):
        n = max_exact
        while n ** span < max_exact ** span * ratio ** m:
            n += 1
        out.append(n)
    return tuple(out)


def _bias_body(tbl_ref, o_ref):
    o = pl.program_id(0)
    h = pl.program_id(1)
    nb = NUM_BUCKETS // 2
    max_exact = nb // 2
    row = lax.broadcasted_iota(I32, (SUB, SUB), 0)
    col = lax.broadcasted_iota(I32, (SUB, SUB), 1)
    rel = row - o * SUB - col
    n = jnp.abs(rel)
    large = jnp.full((SUB, SUB), max_exact, I32)
    for t in _bucket_thresholds():
        large = large + (n >= t).astype(I32)
    bucket = jnp.where(rel > 0, nb, 0) + jnp.where(n < max_exact, n, large)
    val = jnp.zeros((SUB, SUB), F32)
    for bk in range(NUM_BUCKETS):
        val = jnp.where(bucket == bk, tbl_ref[bk, h], val)
    o_ref[0, 0] = (val - tbl_ref[nb - 1, h]) * LOG2E


def _bias_tiles(rel_bias):
    return pl.pallas_call(
        _bias_body,
        grid=(2, N_A),
        in_specs=[pl.BlockSpec(memory_space=pltpu.SMEM)],
        out_specs=pl.BlockSpec((1, 1, SUB, SUB), lambda o, h: (o, h, 0, 0)),
        out_shape=jax.ShapeDtypeStruct((2, N_A, SUB, SUB), F32),
    )(rel_bias)


WCH = 4
A_UNROLL = 4
FAR_SUBS = 2


def _bit(k):
    return INT_MIN if k == 31 else 1 << k


def _bit_transpose32(a):
    a = list(a)
    j, m = 16, 0x0000FFFF
    while j:
        for k in range(32):
            if not k & j:
                t = (lax.shift_right_logical(a[k], jnp.int32(j)) ^ a[k + j]) & m
                a[k + j] = a[k + j] ^ t
                a[k] = a[k] ^ jnp.left_shift(t, jnp.int32(j))
        j >>= 1
        m ^= m << j
    return a


def _dsa_body(ti_ref, tk_ref, tn_ref, tl_ref,
              qt_ref, qit_ref, wit_ref, ki_ref, k_ref, vt_ref, bias_ref, o_ref,
              plane_ref, adm_ref, cand_ref, sel_ref, mb_ref, m_ref, l_ref, acc_ref, lg_ref, al_ref,
              *, tq, tk, q_off, l_valid, topk, packed):
    s = pl.program_id(1)
    i = ti_ref[s]
    kt = tk_ref[s]
    nsub = tn_ref[s]
    subs = tk // SUB
    q_pos0 = q_off + i * tq
    i_sub = q_off // SUB + (i * tq) // SUB

    def fold(x, op, rows=8):
        parts = [x[r:r + rows] for r in range(0, x.shape[0], rows)]
        while len(parts) > 1:
            parts = [op(parts[a], parts[a + 1]) for a in range(0, len(parts), 2)]
        return parts[0]

    @pl.when(kt == 0)
    def _():
        m_ref[...] = jnp.full(m_ref.shape, NEG, F32)
        l_ref[...] = jnp.zeros(l_ref.shape, F32)
        acc_ref[...] = jnp.zeros(acc_ref.shape, F32)

        qchunk = (q_pos0 + lax.broadcasted_iota(I32, (SUB, tq), 1)) >> 6
        krow = lax.broadcasted_iota(I32, (SUB, tq), 0)

        def score_codes(t):
            off = pl.multiple_of(t * SUB, SUB)
            kit = ki_ref[0, pl.ds(off, SUB), :]
            if packed:
                x = jnp.dot(kit, qit_ref[0], preferred_element_type=F32)
                x = jnp.maximum(x, 0.0) * wit_ref[0]
                sc = x[:, 0:tq] + x[:, tq:2 * tq]
                shift = tq // 2
                while shift >= 2 * tq // H_IDX:
                    sc = sc + pltpu.roll(sc, shift, axis=1)
                    shift //= 2
            else:
                sc = jnp.zeros((SUB, tq), F32)
                for j in range(H_IDX):
                    sj = jnp.dot(kit, qit_ref[0, j * D_IDX:(j + 1) * D_IDX, :], preferred_element_type=F32)
                    sc = sc + jnp.maximum(sj, 0.0) * wit_ref[0, j:j + 1, :]
            bits = lax.bitcast_convert_type(sc, I32)
            return bits ^ ((bits >> 31) | INT_MIN)

        def slice_codes(t, code, all_admissible):
            planes = _bit_transpose32([code[8 * k:8 * k + 8] for k in range(32)])
            row = pl.multiple_of(t * 8, 8)
            for k in range(32):
                plane_ref[k, pl.ds(row, 8), :] = planes[k]
            if all_admissible:
                adm_ref[pl.ds(row, 8), :] = jnp.full((8, tq), -1, I32)
            else:
                kpos = t * SUB + krow
                adm = ((kpos >> 6) <= qchunk) & (kpos < l_valid)
                aw = jnp.zeros((8, tq), I32)
                for k in range(32):
                    aw = aw | jnp.where(adm[8 * k:8 * k + 8], _bit(k), 0)
                adm_ref[pl.ds(row, 8), :] = aw

        n_open = jnp.minimum(l_valid, ((q_pos0 >> 6) + 1) * CHUNK) // SUB

        def score_tiles(gi, c):
            for u in range(A_UNROLL):
                slice_codes(gi * A_UNROLL + u, score_codes(gi * A_UNROLL + u), True)
            return c

        def score_tile(t, c):
            slice_codes(t, score_codes(t), False)
            return c

        lax.fori_loop(0, n_open // A_UNROLL, score_tiles, 0)
        lax.fori_loop(n_open // A_UNROLL * A_UNROLL, nsub, score_tile, 0)

        nchunk = (nsub + WCH - 1) // WCH

        def clear_tile(t, c):
            row = pl.multiple_of(t * 8, 8)
            plane_ref[:, pl.ds(row, 8), :] = jnp.zeros((32, 8, tq), I32)
            adm_ref[pl.ds(row, 8), :] = jnp.zeros((8, tq), I32)
            return c

        lax.fori_loop(nsub, nchunk * WCH, clear_tile, 0)

        def chunk_rows(cn):
            return pl.ds(pl.multiple_of(cn * (8 * WCH), 8 * WCH), 8 * WCH)

        def init_chunk(cn, c):
            cand_ref[chunk_rows(cn), :] = adm_ref[chunk_rows(cn), :]
            sel_ref[chunk_rows(cn), :] = jnp.zeros((8 * WCH, tq), I32)
            return c

        lax.fori_loop(0, nchunk, init_chunk, 0)

        def lane_total(cnt):
            return jnp.sum(fold(cnt.astype(F32), jnp.add), axis=0, keepdims=True).astype(I32)

        def count_cand(plane_of):
            def count_chunk(cn, cnt):
                ones = cand_ref[chunk_rows(cn), :] & plane_of(cn)
                return cnt + lax.population_count(ones)

            return lane_total(lax.fori_loop(0, nchunk, count_chunk, jnp.zeros((8 * WCH, tq), I32)))

        def radix_step(plane_of, n_sel, n_one, next_plane_of):
            take = (n_sel + n_one) >= topk

            def update_chunk(cn, cnt):
                cw = cand_ref[chunk_rows(cn), :]
                pw = plane_of(cn)
                ones = cw & pw
                kept = jnp.where(take, ones, cw & ~pw)
                cand_ref[chunk_rows(cn), :] = kept
                sel_ref[chunk_rows(cn), :] = sel_ref[chunk_rows(cn), :] | jnp.where(take, 0, ones)
                return cnt + lax.population_count(kept & next_plane_of(cn))

            cnt = lax.fori_loop(0, nchunk, update_chunk, jnp.zeros((8 * WCH, tq), I32))
            return jnp.where(take, n_sel, n_sel + n_one), lane_total(cnt)

        def code_plane(b):
            return lambda cn: plane_ref[b, chunk_rows(cn), :]

        def code_step(it, st):
            return radix_step(code_plane(31 - it), st[0], st[1], code_plane(jnp.maximum(30 - it, 0)))

        n_sel, _ = lax.fori_loop(0, 32, code_step, (jnp.zeros((1, tq), I32), count_cand(code_plane(31))))

        n_tie = count_cand(lambda cn: jnp.int32(-1))
        crowded = jnp.max(jnp.where((n_sel + n_tie > topk) & (n_tie > 1), 1, 0))

        @pl.when(crowded > 0)
        def _():
            wrow = lax.broadcasted_iota(I32, (8 * WCH, tq), 0)
            index_planes = []
            for tb in reversed(range(max(1, (plane_ref.shape[1] // 8 - 1).bit_length()))):
                index_planes.append(lambda cn, tb=tb: (((cn * WCH + (wrow >> 3)) >> tb) & 1) - 1)
            for low in (0x0000FFFF, 0x00FF00FF, 0x0F0F0F0F, 0x33333333, 0x55555555):
                index_planes.append(lambda cn, low=low: jnp.int32(low))
            for rb in (2, 1, 0):
                index_planes.append(lambda cn, rb=rb: ((wrow >> rb) & 1) - 1)
            index_planes.append(index_planes[-1])
            n, n_one = n_sel, count_cand(index_planes[0])
            for this_plane, next_plane in zip(index_planes[:-1], index_planes[1:]):
                n, n_one = radix_step(this_plane, n, n_one, next_plane)

        def final_chunk(cn, c):
            sel_ref[chunk_rows(cn), :] = sel_ref[chunk_rows(cn), :] | cand_ref[chunk_rows(cn), :]
            return c

        lax.fori_loop(0, nchunk, final_chunk, 0)

    n_here = jnp.clip(nsub - kt * subs, 0, subs)

    def key_block(u, nsubs, near):
        nk = nsubs * SUB
        g = kt * subs + u
        dsub = g - i_sub
        uoff = pl.multiple_of(u * SUB, SUB)
        for a in range(nsubs):
            sw = sel_ref[pl.ds(pl.multiple_of((g + a) * 8, 8), 8), :]
            for k in range(32):
                r = a * SUB + 8 * k
                mb_ref[r:r + 8, :] = jnp.where((sw & _bit(k)) != 0, 0.0, 2.0 * NEG)
        def logits_sweep(h):
            hs = slice(h * HD_A, (h + 1) * HD_A)
            kh = k_ref[0, pl.ds(uoff, nk), hs]
            lg = jnp.dot(kh, qt_ref[0, hs, :], preferred_element_type=F32)
            if near:
                lg = lg + bias_ref[-dsub, h, :, 0:tq]
            lg = lg + mb_ref[0:nk, :]
            lg_ref[h, 0:nk, :] = lg
            m_old = m_ref[h]
            m_new = jnp.maximum(m_old, jnp.max(fold(lg, jnp.maximum), axis=0, keepdims=True))
            al_ref[h] = jnp.exp2(m_old - m_new)
            m_ref[h] = m_new

        def values_sweep(h):
            hs = slice(h * HD_A, (h + 1) * HD_A)
            vth = vt_ref[0, hs, pl.ds(uoff, nk)]
            alpha = al_ref[h]
            p = jnp.exp2((lg_ref[h, 0:nk, :] - m_ref[h][0:1, :]).astype(BF16))
            pv = jnp.dot(jnp.concatenate([vth, jnp.ones((16, nk), BF16)], axis=0), p,
                         preferred_element_type=F32)
            l_ref[h] = alpha * l_ref[h] + pv[HD_A:HD_A + 8, :]
            acc_ref[hs, :] = alpha[0:1, :] * acc_ref[hs, :] + pv[0:HD_A, :]

        for h in range(N_A):
            logits_sweep(h)
        for h in range(N_A):
            values_sweep(h)

    n_far = jnp.clip(i_sub - 1 - kt * subs, 0, n_here)

    def near_one(u, c):
        key_block(u, 1, True)
        return c

    done = 0
    width = FAR_SUBS
    while width > subs:
        width //= 2
    while width >= 1:
        def far_block(bi, c, width=width, done=done):
            key_block(done + bi * width, width, False)
            return c

        count = (n_far - done) // width
        lax.fori_loop(0, count, far_block, 0)
        done = done + count * width
        width //= 2
    lax.fori_loop(n_far, n_here, near_one, 0)

    @pl.when(tl_ref[s] == 1)
    def _():
        for h in range(N_A):
            hs = slice(h * HD_A, (h + 1) * HD_A)
            o_ref[0, hs, :] = (acc_ref[hs, :] / l_ref[h][0:1, :]).astype(BF16)


def _dsa(qt, qit, wit, ki, k, vt, bias, *, tq, tk, q_off, l_valid, topk, packed=False):
    bsz, sq = qt.shape[0], qt.shape[2]
    sk = k.shape[1]
    assert sq % tq == 0 and sk % tk == 0 and tk % SUB == 0 and q_off % SUB == 0
    assert (tq % SUB == 0 or sq == tq) and tq % LANES == 0 and (not packed or sq == tq)
    subs = tk // SUB
    wrows = -(-(sk // SUB) // WCH) * WCH * 8
    ti, tkk, tn, tl = [], [], [], []
    for i in range(sq // tq):
        q_last = q_off + (i + 1) * tq - 1
        lim = min(l_valid, (q_last // CHUNK + 1) * CHUNK)
        nsub = -(-lim // SUB)
        nkt = -(-nsub // subs)
        for t in range(nkt):
            ti.append(i), tkk.append(t), tn.append(nsub), tl.append(int(t == nkt - 1))
    tabs = [jnp.asarray(np.asarray(a, np.int32)) for a in (ti, tkk, tn, tl)]
    body = functools.partial(_dsa_body, tq=tq, tk=tk, q_off=q_off, l_valid=l_valid, topk=topk, packed=packed)
    qmap = lambda b, s, ti, tk_, tn, tl: (b, 0, ti[s])
    if packed:
        idx_specs = [pl.BlockSpec((1, D_IDX, 2 * tq), qmap), pl.BlockSpec((1, 1, 2 * tq), qmap)]
    else:
        idx_specs = [pl.BlockSpec((1, D_QI, tq), qmap), pl.BlockSpec((1, H_IDX, tq), qmap)]
    grid_spec = pltpu.PrefetchScalarGridSpec(
        num_scalar_prefetch=4,
        grid=(bsz, len(ti)),
        in_specs=[pl.BlockSpec((1, D_A, tq), qmap),
                  *idx_specs,
                  pl.BlockSpec((1, sk, D_IDX), lambda b, s, *_: (b, 0, 0)),
                  pl.BlockSpec((1, tk, D_A), lambda b, s, ti, tk_, tn, tl: (b, tk_[s], 0)),
                  pl.BlockSpec((1, D_A, tk), lambda b, s, ti, tk_, tn, tl: (b, 0, tk_[s])),
                  pl.BlockSpec((2, N_A, SUB, SUB), lambda b, s, *_: (0, 0, 0, 0))],
        out_specs=pl.BlockSpec((1, D_A, tq), qmap),
        scratch_shapes=[pltpu.VMEM((32, wrows, tq), I32),
                        pltpu.VMEM((wrows, tq), I32),
                        pltpu.VMEM((wrows, tq), I32),
                        pltpu.VMEM((wrows, tq), I32),
                        pltpu.VMEM((FAR_SUBS * SUB, tq), F32),
                        pltpu.VMEM((N_A, 8, tq), F32),
                        pltpu.VMEM((N_A, 8, tq), F32),
                        pltpu.VMEM((D_A, tq), F32),
                        pltpu.VMEM((N_A, FAR_SUBS * SUB, tq), F32),
                        pltpu.VMEM((N_A, 8, tq), F32)])
    return pl.pallas_call(
        body,
        grid_spec=grid_spec,
        out_shape=jax.ShapeDtypeStruct((bsz, D_A, sq), BF16),
        compiler_params=_cparams(("parallel", "arbitrary")),
    )(*tabs, qt, qit, wit, ki, k, vt, bias)


def _pack_cache_body(k4_ref, v4_ref, kn_ref, vnt_ref, ko_ref, vto_ref, *, ncache):
    c = pl.program_id(1)

    @pl.when(c < ncache)
    def _():
        for h in range(N_A):
            hs = slice(h * HD_A, (h + 1) * HD_A)
            ko_ref[0, :, hs] = k4_ref[0, pl.ds(h, SUB, stride=N_A), :].astype(BF16)
            vto_ref[0, hs, :] = v4_ref[0, pl.ds(h, SUB, stride=N_A), :].T.astype(BF16)

    @pl.when(c == ncache)
    def _():
        ko_ref[0] = kn_ref[0]
        vto_ref[0] = vnt_ref[0]


def _pack_cache(cache_k, cache_v, k_new, vt_new):
    bsz, past = cache_k.shape[0], cache_k.shape[1]
    ncache = past // SUB
    cmap = lambda b, c: (b, jnp.minimum(c, ncache - 1), 0)
    rows = lambda a: a.reshape(bsz, past * N_A, HD_A)
    return pl.pallas_call(
        functools.partial(_pack_cache_body, ncache=ncache),
        grid=(bsz, ncache + 1),
        in_specs=[pl.BlockSpec((1, SUB * N_A, HD_A), cmap),
                  pl.BlockSpec((1, SUB * N_A, HD_A), cmap),
                  pl.BlockSpec((1, SUB, D_A), lambda b, c: (b, 0, 0)),
                  pl.BlockSpec((1, D_A, SUB), lambda b, c: (b, 0, 0))],
        out_specs=[pl.BlockSpec((1, SUB, D_A), lambda b, c: (b, c, 0)),
                   pl.BlockSpec((1, D_A, SUB), lambda b, c: (b, 0, c))],
        out_shape=[jax.ShapeDtypeStruct((bsz, past + SUB, D_A), BF16),
                   jax.ShapeDtypeStruct((bsz, D_A, past + SUB), BF16)],
        compiler_params=_cparams(("parallel", "arbitrary")),
    )(rows(cache_k), rows(cache_v), k_new, vt_new)


def _mlstm_body(q_ref, k_ref, kt_ref, v_ref, og_ref, sm_ref, gr_ref, gh_ref, c0_ref, n0_ref, m0_ref,
                hb_ref, co_ref, no_ref, mo_ref, c_s, n_s, m_s, *, lc, bb):
    c = pl.program_id(1)

    @pl.when(c == 0)
    def _():
        for s in range(bb):
            for h in range(N_B):
                c_s[s * N_B + h] = c0_ref[s, h]
                n_s[s * N_B + h] = jnp.broadcast_to(n0_ref[s, h:h + 1, :], (8, HD_B))
                m_s[s * N_B + h] = jnp.broadcast_to(m0_ref[s, h:h + 1, :], (8, LANES))

    row = lax.broadcasted_iota(I32, (lc, lc), 0)
    col = lax.broadcasted_iota(I32, (lc, lc), 1)
    causal = col <= row
    hp = lax.Precision.HIGHEST
    lower = causal.astype(F32)
    upper = (row <= col).astype(F32)

    for s in range(bb):
        sm = sm_ref[s]
        gr = gr_ref[s]
        b_cols = jnp.dot(lower, sm, precision=hp, preferred_element_type=F32)
        b_rows = jnp.dot(gr, upper, precision=hp, preferred_element_type=F32)
        for h in range(N_B):
            sh = s * N_B + h
            hs = slice(h * HD_B, (h + 1) * HD_B)
            m_prev = m_s[sh][0:1, 0:1]
            bc = b_cols[:, SM_LF + h:SM_LF + h + 1]
            ig_c = sm[:, SM_IG + h:SM_IG + h + 1]
            br = b_rows[N_B + h:N_B + h + 1, :]
            ig_r = gr[h:h + 1, :]
            log_d = jnp.where(causal, bc - br + ig_r, NEG)
            log_inter = bc + m_prev
            m_t = jnp.maximum(log_inter, jnp.max(log_d, axis=1, keepdims=True))
            d = jnp.exp(log_d - m_t)
            inter = jnp.exp(log_inter - m_t)
            qh = q_ref[s, :, hs]
            kth = kt_ref[s, hs, :]
            vh = v_ref[s, :, hs]
            sc = jnp.dot(qh, kth, preferred_element_type=F32) * d
            ch = c_s[sh]
            nrow = n_s[sh][0:1, :]
            num = (jnp.dot(sc.astype(BF16), vh, preferred_element_type=F32)
                   + inter * jnp.dot(qh, ch.astype(BF16), preferred_element_type=F32))
            den = (jnp.sum(sc, axis=1, keepdims=True)
                   + inter * jnp.sum(qh.astype(F32) * nrow, axis=1, keepdims=True))
            hh = num / jnp.maximum(jnp.abs(den), jnp.exp(-m_t))
            hb = _rms(hh, gh_ref[...]) * og_ref[s, :, hs].astype(F32)
            hb_ref[s, :, hs] = hb.astype(BF16)

            m_new = m_t[lc - 1:lc, :]
            b_last = bc[lc - 1:lc, :]
            decay = jnp.exp(b_last + m_prev - m_new)
            w_r = jnp.exp(b_last - br + ig_r - m_new)
            w_c = jnp.exp(b_last - bc + ig_c - m_new)
            kw = (kth.astype(F32) * w_r).astype(BF16)
            c_s[sh] = decay * ch + jnp.dot(kw, vh, preferred_element_type=F32)
            n_new = decay * nrow + jnp.sum(k_ref[s, :, hs].astype(F32) * w_c, axis=0, keepdims=True)
            n_s[sh] = jnp.broadcast_to(n_new, (8, HD_B))
            m_s[sh] = jnp.broadcast_to(m_new, (8, LANES))

    @pl.when(c == pl.num_programs(1) - 1)
    def _():
        for s in range(bb):
            for h in range(N_B):
                co_ref[s, h] = c_s[s * N_B + h]
                no_ref[s, h:h + 1, :] = n_s[s * N_B + h][0:1, :]
                mo_ref[s, h:h + 1, :] = m_s[s * N_B + h][0:1, :]


def _mlstm(q, k, kt, v, og, sm, gr, g_h, c0, n0, m0, *, lc):
    bsz, s = q.shape[0], q.shape[1]
    bb = 1
    tokb = lambda w: pl.BlockSpec((bb, lc, w), lambda b, c: (b, c, 0))
    st3 = lambda w: pl.BlockSpec((bb, N_B, w), lambda b, c: (b, 0, 0))
    st4 = pl.BlockSpec((bb, N_B, HD_B, HD_B), lambda b, c: (b, 0, 0, 0))
    return pl.pallas_call(
        functools.partial(_mlstm_body, lc=lc, bb=bb),
        grid=(bsz // bb, s // lc),
        in_specs=[tokb(D_B), tokb(D_B),
                  pl.BlockSpec((bb, D_B, lc), lambda b, c: (b, 0, c)),
                  tokb(D_B), tokb(D_B), tokb(LANES),
                  pl.BlockSpec((bb, 2 * N_B, lc), lambda b, c: (b, 0, c)),
                  pl.BlockSpec((1, HD_B), lambda b, c: (0, 0)),
                  st4, st3(HD_B), st3(LANES)],
        out_specs=[tokb(D_B), st4, st3(HD_B), st3(LANES)],
        out_shape=[jax.ShapeDtypeStruct((bsz, s, D_B), BF16),
                   jax.ShapeDtypeStruct((bsz, N_B, HD_B, HD_B), F32),
                   jax.ShapeDtypeStruct((bsz, N_B, HD_B), F32),
                   jax.ShapeDtypeStruct((bsz, N_B, LANES), F32)],
        scratch_shapes=[pltpu.VMEM((bb * N_B, HD_B, HD_B), F32),
                        pltpu.VMEM((bb * N_B, 8, HD_B), F32),
                        pltpu.VMEM((bb * N_B, 8, LANES), F32)],
        compiler_params=_cparams(("parallel", "arbitrary")),
    )(q, k, kt, v, og, sm, gr, g_h, c0, n0, m0)


def _mixer_front(x, ffn1, inp):
    x1 = _ffn(x, *ffn1)
    return (x1,) + tuple(_inproj(x1, *inp))


def kernel(x_prompt, x_sample, cache_k, cache_v, cache_k_idx, state_C, state_n, state_m, g_ffn1, w1_ffn1, w3_ffn1, w2_ffn1, g_mix, w_in, g_q, g_k, g_kidx, rel_bias, b_i, b_f, g_h, w_out, g_ffn2, w1_ffn2, w3_ffn2, w2_ffn2):
    bsz, seq = x_prompt.shape[0], x_prompt.shape[1]
    dbs, dseq = x_sample.shape[0], x_sample.shape[1]
    past = cache_k.shape[1]

    ffn1 = _prep_ffn(g_ffn1, w1_ffn1, w3_ffn1, w2_ffn1)
    ffn2 = _prep_ffn(g_ffn2, w1_ffn2, w3_ffn2, w2_ffn2)
    inp = _prep_inproj(g_mix, w_in, g_q, g_k, g_kidx, b_i, b_f)
    wa = w_out[:D_A].astype(BF16)
    wh = w_out[D_A:].astype(BF16)
    gh = g_h.reshape(1, HD_B)
    bias = _bias_tiles(rel_bias)

    n_p = bsz * seq
    x1, q, kf, k16, vf, v16, qi, qb, kb, vb, og, sm = _mixer_front(x_prompt.reshape(n_p, D_MODEL), ffn1, inp)
    r3 = lambda a, b_, s_: a.reshape(b_, s_, a.shape[-1])
    sm3 = r3(sm, bsz, seq)
    tr = lambda a: jnp.swapaxes(a, 1, 2)
    tq_p = min(SUB, seq)
    attn = tr(_dsa(tr(r3(q, bsz, seq)), tr(r3(qi, bsz, seq)), tr(sm3[:, :, SM_WI:SM_IG]),
                   sm3[:, :, :D_IDX].astype(BF16), r3(k16, bsz, seq), tr(r3(v16, bsz, seq)), bias,
                   tq=tq_p, tk=min(1024, seq), q_off=0, l_valid=seq, topk=min(TOPK_MAX, seq // 4)))
    gr = jnp.swapaxes(sm3[:, :, SM_IG:SM_END], 1, 2)
    kb3 = r3(kb, bsz, seq)
    hb, c_p, n_p_, m_p = _mlstm(r3(qb, bsz, seq), kb3, jnp.swapaxes(kb3, 1, 2), r3(vb, bsz, seq),
                                r3(og, bsz, seq), sm3, gr, gh,
                                jnp.zeros((bsz, N_B, HD_B, HD_B), F32), jnp.zeros((bsz, N_B, HD_B), F32),
                                jnp.zeros((bsz, N_B, LANES), F32), lc=min(MLSTM_CHUNK, seq))
    x2 = _outproj(x1, attn.reshape(n_p, D_A), hb.reshape(n_p, D_B), wa, wh)
    y_prompt = _ffn(x2, *ffn2).reshape(bsz, seq, D_MODEL)

    n_s = dbs * dseq
    x1s, qs, kfs, k16s, vfs, v16s, qis, qbs, kbs, vbs, ogs, sms = _mixer_front(
        x_sample.reshape(n_s, D_MODEL), ffn1, inp)
    sms3 = r3(sms, dbs, dseq)
    l_all = past + dseq
    tk_s = 3 * SUB
    sk = -(-l_all // tk_s) * tk_s
    padk = lambda a: jnp.pad(a, ((0, 0), (0, sk - l_all), (0, 0)))
    assert past % SUB == 0 and sk - past == SUB
    padn = lambda a: jnp.pad(a, ((0, 0), (0, SUB - dseq), (0, 0)))
    k_all, vt_all = _pack_cache(cache_k, cache_v, padn(r3(k16s, dbs, dseq)), tr(padn(r3(v16s, dbs, dseq))))
    ki_all = padk(jnp.concatenate([cache_k_idx.astype(BF16), sms3[:, :, :D_IDX].astype(BF16)], axis=1))
    assert dseq <= LANES
    padq = lambda a: jnp.pad(tr(a), ((0, 0), (0, 0), (0, LANES - dseq)))
    qi_s, wi_s = r3(qis, dbs, dseq), sms3[:, :, SM_WI:SM_IG]
    packed = H_IDX * dseq == 2 * LANES
    if packed:
        qit_s = qi_s.reshape(dbs, dseq, H_IDX, D_IDX).transpose(0, 3, 2, 1).reshape(dbs, D_IDX, 2 * LANES)
        wit_s = tr(wi_s).reshape(dbs, 1, 2 * LANES)
    else:
        qit_s, wit_s = padq(qi_s), padq(wi_s)
    attn_s = tr(_dsa(padq(r3(qs, dbs, dseq)), qit_s, wit_s, ki_all, k_all, vt_all, bias,
                     tq=LANES, tk=tk_s, q_off=past, l_valid=l_all, topk=min(TOPK_MAX, l_all // 4),
                     packed=packed)[:, :, :dseq])
    lc_s = LANES
    padt = lambda a: jnp.pad(a, ((0, 0), (0, lc_s - dseq), (0, 0)))
    lane = jnp.arange(LANES)
    sm_pad = jnp.where((lane >= SM_IG) & (lane < SM_LF), NEG, 0.0).astype(F32)
    sms_p = jnp.concatenate([sms3, jnp.broadcast_to(sm_pad, (dbs, lc_s - dseq, LANES))], axis=1)
    kbs3 = padt(r3(kbs, dbs, dseq))
    hbs, c_s, n_s_, m_s = _mlstm(padt(r3(qbs, dbs, dseq)), kbs3, jnp.swapaxes(kbs3, 1, 2), padt(r3(vbs, dbs, dseq)),
                                 padt(r3(ogs, dbs, dseq)), sms_p, jnp.swapaxes(sms_p[:, :, SM_IG:SM_END], 1, 2), gh,
                                 state_C.astype(F32), state_n.astype(F32),
                                 jnp.broadcast_to(state_m.astype(F32)[:, :, None], (dbs, N_B, LANES)), lc=lc_s)
    x2s = _outproj(x1s, attn_s.reshape(n_s, D_A), hbs[:, :dseq].reshape(n_s, D_B), wa, wh)
    y_sample = _ffn(x2s, *ffn2).reshape(dbs, dseq, D_MODEL)

    sd = state_C.dtype
    return (y_prompt, y_sample,
            kf.reshape(bsz, seq, N_A, HD_A), vf.reshape(bsz, seq, N_A, HD_A), sm3[:, :, :D_IDX],
            c_p.astype(sd), n_p_.astype(sd), m_p[:, :, 0].astype(sd),
            kfs.reshape(dbs, dseq, N_A, HD_A), vfs.reshape(dbs, dseq, N_A, HD_A), sms3[:, :, :D_IDX],
            c_s.astype(sd), n_s_.astype(sd), m_s[:, :, 0].astype(sd))
```

```python
import functools
import math

import numpy as np
import jax
import jax.numpy as jnp
from jax import lax
from jax.experimental import pallas as pl
from jax.experimental.pallas import tpu as pltpu

F32 = jnp.float32
BF16 = jnp.bfloat16
I32 = jnp.int32

D_MODEL = 2048
CHUNK = 64
N_A, HD_A = 8, 128
H_IDX, D_IDX = 16, 64
TOPK_MAX = 256
N_B, HD_B = 4, 256
D_FF = 5504
NUM_BUCKETS, MAX_DISTANCE = 32, 128
EPS = 1e-6
NEG = -1e30
INT_MIN = -(2 ** 31)
LOG2E = math.log2(math.e)

LANES = 128
D_A = N_A * HD_A
D_B = N_B * HD_B
D_QI = H_IDX * D_IDX
FF_TILE = 512
D_FF_PAD = -(-D_FF // FF_TILE) * FF_TILE
FFN_TOK_TILE = 512
PROJ_TOK_TILE = 256
TOK_TILE = 512
SUB = 256
DSA_KEY_TILE = 2048
MLSTM_CHUNK = 256
VMEM_LIMIT = 56 * 1024 * 1024

SM_WI, SM_IG, SM_LF = D_IDX, D_IDX + H_IDX, D_IDX + H_IDX + N_B
SM_END = SM_LF + N_B


def _cparams(sem):
    return pltpu.CompilerParams(dimension_semantics=sem, vmem_limit_bytes=VMEM_LIMIT)


def _rms(x, g):
    ms = jnp.mean(x * x, axis=-1, keepdims=True)
    return x * lax.rsqrt(ms + EPS) * g


def _ffn_body(x_ref, g_ref, w1_ref, w3_ref, w2_ref, o_ref, hn_ref):
    j = pl.program_id(1)

    @pl.when(j == 0)
    def _():
        hn_ref[...] = _rms(x_ref[...], g_ref[...]).astype(BF16)
        o_ref[...] = jnp.zeros(o_ref.shape, F32)

    h = hn_ref[...]
    a = jnp.dot(h, w1_ref[...], preferred_element_type=F32)
    b = jnp.dot(h, w3_ref[...], preferred_element_type=F32)
    u = (a * jax.nn.sigmoid(a) * b).astype(BF16)
    o_ref[...] += jnp.dot(u, w2_ref[...], preferred_element_type=F32)

    @pl.when(j == pl.num_programs(1) - 1)
    def _():
        o_ref[...] = x_ref[...] + 0.5 * o_ref[...]


def _ffn(x, g, w1, w3, w2):
    n = x.shape[0]
    tm = min(FFN_TOK_TILE, n)
    grid = (n // tm, D_FF_PAD // FF_TILE)
    return pl.pallas_call(
        _ffn_body,
        grid=grid,
        in_specs=[
            pl.BlockSpec((tm, D_MODEL), lambda i, j: (i, 0)),
            pl.BlockSpec((1, D_MODEL), lambda i, j: (0, 0)),
            pl.BlockSpec((D_MODEL, FF_TILE), lambda i, j: (0, j)),
            pl.BlockSpec((D_MODEL, FF_TILE), lambda i, j: (0, j)),
            pl.BlockSpec((FF_TILE, D_MODEL), lambda i, j: (j, 0)),
        ],
        out_specs=pl.BlockSpec((tm, D_MODEL), lambda i, j: (i, 0)),
        out_shape=jax.ShapeDtypeStruct((n, D_MODEL), F32),
        scratch_shapes=[pltpu.VMEM((tm, D_MODEL), BF16)],
        compiler_params=_cparams(("parallel", "arbitrary")),
    )(x, g, w1, w3, w2)


def _cast_pad_body(x_ref, o_ref, *, blocks_valid, cols_valid):
    x = jnp.where(pl.program_id(0) < blocks_valid, x_ref[...], 0.0)
    o_ref[:, :cols_valid] = x.astype(BF16)
    if cols_valid < o_ref.shape[1]:
        o_ref[:, cols_valid:] = jnp.zeros((o_ref.shape[0], o_ref.shape[1] - cols_valid), BF16)


def _cast_pad(w, rows, cols, block_rows):
    r, c = w.shape
    assert rows % block_rows == 0 and r % block_rows == 0 and c % LANES == 0 and cols >= c and rows >= r
    nvalid = r // block_rows
    body = functools.partial(_cast_pad_body, blocks_valid=nvalid, cols_valid=c)
    return pl.pallas_call(
        body,
        grid=(rows // block_rows,),
        in_specs=[pl.BlockSpec((block_rows, c), lambda i: (jnp.minimum(i, nvalid - 1), 0))],
        out_specs=pl.BlockSpec((block_rows, cols), lambda i: (i, 0)),
        out_shape=jax.ShapeDtypeStruct((rows, cols), BF16),
        compiler_params=_cparams(("parallel",)),
    )(w)


def _prep_ffn(g, w1, w3, w2):
    return (g.reshape(1, D_MODEL),
            _cast_pad(w1, D_MODEL, D_FF_PAD, 256),
            _cast_pad(w3, D_MODEL, D_FF_PAD, 256),
            _cast_pad(w2, D_FF_PAD, D_MODEL, LANES))


def _head_norm(p, g, heads, hd):
    outs = []
    for h in range(heads):
        outs.append(_rms(p[:, h * hd:(h + 1) * hd], g))
    return outs


def _inproj_a_body(x_ref, g_ref, w_ref, wsh_ref, wsl_ref, gq_ref, gk_ref, sg_ref, sb_ref,
                   q_ref, kf_ref, k16_ref, vf_ref, v16_ref, qi_ref, sm_ref):
    h32 = _rms(x_ref[...], g_ref[...])
    h = h32.astype(BF16)
    col = lambda c: jnp.dot(h, w_ref[:, c * D_A:(c + 1) * D_A], preferred_element_type=F32)

    h_lo = (h32 - h.astype(F32)).astype(BF16)
    ps = (jnp.dot(h, wsh_ref[...], preferred_element_type=F32)
          + jnp.dot(h_lo, wsh_ref[...], preferred_element_type=F32)
          + jnp.dot(h, wsl_ref[...], preferred_element_type=F32))
    lane = lax.broadcasted_iota(I32, ps.shape, 1)
    ms = jnp.sum(jnp.where(lane < SM_WI, ps * ps, 0.0), axis=-1, keepdims=True) * (1.0 / D_IDX)
    kin = ps * lax.rsqrt(ms + EPS) * sg_ref[...]
    z = ps + sb_ref[...]
    ls = jnp.minimum(z, 0.0) - jnp.log1p(jnp.exp(-jnp.abs(z)))
    sm_ref[...] = jnp.where(lane < SM_WI, kin,
                            jnp.where(lane < SM_IG, ps,
                                      jnp.where(lane < SM_LF, z,
                                                jnp.where(lane < SM_END, ls, 0.0))))

    for hd, qh in enumerate(_head_norm(col(0), gq_ref[...], N_A, HD_A)):
        q_ref[:, hd * HD_A:(hd + 1) * HD_A] = (qh * (HD_A ** -0.5 * LOG2E)).astype(BF16)
    for hd, kh in enumerate(_head_norm(col(1), gk_ref[...], N_A, HD_A)):
        kf_ref[:, hd * HD_A:(hd + 1) * HD_A] = kh
        k16_ref[:, hd * HD_A:(hd + 1) * HD_A] = kh.astype(BF16)
    p = col(2)
    vf_ref[...] = p
    v16_ref[...] = p.astype(BF16)
    qi_ref[...] = col(3).astype(BF16)


def _inproj_b_body(x_ref, g_ref, w_ref, qb_ref, kb_ref, vb_ref, og_ref):
    h = _rms(x_ref[...], g_ref[...]).astype(BF16)
    col = lambda c: jnp.dot(h, w_ref[:, c * D_B:(c + 1) * D_B], preferred_element_type=F32)
    qb_ref[...] = col(0).astype(BF16)
    kb_ref[...] = (col(1) * HD_B ** -0.5).astype(BF16)
    vb_ref[...] = col(2).astype(BF16)
    og_ref[...] = jax.nn.sigmoid(col(3)).astype(BF16)


def _inproj(x, g, wa, wb, wsh, wsl, gq, gk, sg, sb):
    n = x.shape[0]
    tm = min(PROJ_TOK_TILE, n)
    tok = lambda w: pl.BlockSpec((tm, w), lambda i: (i, 0))
    const = lambda a: pl.BlockSpec(a.shape, lambda i: (0,) * a.ndim)
    wide = lambda dt: jax.ShapeDtypeStruct((n, D_A), dt)
    outs_a = pl.pallas_call(
        _inproj_a_body,
        grid=(n // tm,),
        in_specs=[tok(D_MODEL)] + [const(a) for a in (g, wa, wsh, wsl, gq, gk, sg, sb)],
        out_specs=[tok(D_A)] * 6 + [tok(LANES)],
        out_shape=[wide(BF16), wide(F32), wide(BF16), wide(F32), wide(BF16), wide(BF16),
                   jax.ShapeDtypeStruct((n, LANES), F32)],
        compiler_params=_cparams(("parallel",)),
    )(x, g, wa, wsh, wsl, gq, gk, sg, sb)
    outs_b = pl.pallas_call(
        _inproj_b_body,
        grid=(n // tm,),
        in_specs=[tok(D_MODEL), const(g), const(wb)],
        out_specs=[tok(D_B)] * 4,
        out_shape=[wide(BF16)] * 4,
        compiler_params=_cparams(("parallel",)),
    )(x, g, wb)
    q, kf, k16, vf, v16, qi, sm = outs_a
    return (q, kf, k16, vf, v16, qi) + tuple(outs_b) + (sm,)


def _prep_inproj(g_mix, w_in, g_q, g_k, g_kidx, b_i, b_f):
    o_ki = 4 * D_A
    o_qb = o_ki + D_IDX + H_IDX
    o_ib = o_qb + 4 * D_B
    wa = w_in[:, :o_ki].astype(BF16)
    wb = w_in[:, o_qb:o_ib].astype(BF16)
    ws = jnp.concatenate([w_in[:, o_ki:o_qb], w_in[:, o_ib:],
                          jnp.zeros((D_MODEL, LANES - SM_END), F32)], axis=1)
    wsh = ws.astype(BF16)
    wsl = (ws - wsh.astype(F32)).astype(BF16)
    sg = jnp.concatenate([g_kidx, jnp.ones((LANES - D_IDX,), F32)]).reshape(1, LANES)
    sb = jnp.concatenate([jnp.zeros((SM_IG,), F32), b_i, b_f,
                          jnp.zeros((LANES - SM_END,), F32)]).reshape(1, LANES)
    return (g_mix.reshape(1, D_MODEL), wa, wb, wsh, wsl,
            g_q.reshape(1, HD_A), g_k.reshape(1, HD_A), sg, sb)


def _outproj_body(x_ref, a_ref, h_ref, wa_ref, wh_ref, o_ref):
    o_ref[...] = (x_ref[...]
                  + jnp.dot(a_ref[...], wa_ref[...], preferred_element_type=F32)
                  + jnp.dot(h_ref[...], wh_ref[...], preferred_element_type=F32))


def _outproj(x, a, h, wa, wh):
    n = x.shape[0]
    tm = min(TOK_TILE, n)
    return pl.pallas_call(
        _outproj_body,
        grid=(n // tm,),
        in_specs=[pl.BlockSpec((tm, D_MODEL), lambda i: (i, 0)),
                  pl.BlockSpec((tm, D_A), lambda i: (i, 0)),
                  pl.BlockSpec((tm, D_B), lambda i: (i, 0)),
                  pl.BlockSpec((D_A, D_MODEL), lambda i: (0, 0)),
                  pl.BlockSpec((D_B, D_MODEL), lambda i: (0, 0))],
        out_specs=pl.BlockSpec((tm, D_MODEL), lambda i: (i, 0)),
        out_shape=jax.ShapeDtypeStruct((n, D_MODEL), F32),
        compiler_params=_cparams(("parallel",)),
    )(x, a, h, wa, wh)


def _bucket_thresholds():
    nb = NUM_BUCKETS // 2
    max_exact = nb // 2
    span = nb - max_exact
    ratio = MAX_DISTANCE // max_exact
    out = []
    for m in range(1, span):
        n = max_exact
        while n ** span < max_exact ** span * ratio ** m:
            n += 1
        out.append(n)
    return tuple(out)


def _bias_body(tbl_ref, o_ref):
    o = pl.program_id(0)
    h = pl.program_id(1)
    nb = NUM_BUCKETS // 2
    max_exact = nb // 2
    row = lax.broadcasted_iota(I32, (SUB, SUB), 0)
    col = lax.broadcasted_iota(I32, (SUB, SUB), 1)
    rel = row - o * SUB - col
    n = jnp.abs(rel)
    large = jnp.full((SUB, SUB), max_exact, I32)
    for t in _bucket_thresholds():
        large = large + (n >= t).astype(I32)
    bucket = jnp.where(rel > 0, nb, 0) + jnp.where(n < max_exact, n, large)
    val = jnp.zeros((SUB, SUB), F32)
    for bk in range(NUM_BUCKETS):
        val = jnp.where(bucket == bk, tbl_ref[bk, h], val)
    o_ref[0, 0] = (val - tbl_ref[nb - 1, h]) * LOG2E


def _bias_tiles(rel_bias):
    return pl.pallas_call(
        _bias_body,
        grid=(2, N_A),
        in_specs=[pl.BlockSpec(memory_space=pltpu.SMEM)],
        out_specs=pl.BlockSpec((1, 1, SUB, SUB), lambda o, h: (o, h, 0, 0)),
        out_shape=jax.ShapeDtypeStruct((2, N_A, SUB, SUB), F32),
    )(rel_bias)


WCH = 4
A_UNROLL = 4
FAR_SUBS = 2


def _bit(k):
    return INT_MIN if k == 31 else 1 << k


def _bit_transpose32(a):
    a = list(a)
    j, m = 16, 0x0000FFFF
    while j:
        for k in range(32):
            if not k & j:
                t = (lax.shift_right_logical(a[k], jnp.int32(j)) ^ a[k + j]) & m
                a[k + j] = a[k + j] ^ t
                a[k] = a[k] ^ jnp.left_shift(t, jnp.int32(j))
        j >>= 1
        m ^= m << j
    return a


def _dsa_body(ti_ref, tk_ref, tn_ref, tl_ref,
              qt_ref, qit_ref, wit_ref, ki_ref, k_ref, vt_ref, bias_ref, o_ref,
              plane_ref, adm_ref, cand_ref, sel_ref, mb_ref, m_ref, l_ref, acc_ref, lg_ref, al_ref,
              *, tq, tk, q_off, l_valid, topk, packed):
    s = pl.program_id(1)
    i = ti_ref[s]
    kt = tk_ref[s]
    nsub = tn_ref[s]
    subs = tk // SUB
    q_pos0 = q_off + i * tq
    i_sub = q_off // SUB + (i * tq) // SUB

    def fold(x, op, rows=8):
        parts = [x[r:r + rows] for r in range(0, x.shape[0], rows)]
        while len(parts) > 1:
            parts = [op(parts[a], parts[a + 1]) for a in range(0, len(parts), 2)]
        return parts[0]

    @pl.when(kt == 0)
    def _():
        m_ref[...] = jnp.full(m_ref.shape, NEG, F32)
        l_ref[...] = jnp.zeros(l_ref.shape, F32)
        acc_ref[...] = jnp.zeros(acc_ref.shape, F32)

        qchunk = (q_pos0 + lax.broadcasted_iota(I32, (SUB, tq), 1)) >> 6
        krow = lax.broadcasted_iota(I32, (SUB, tq), 0)

        def score_codes(t):
            off = pl.multiple_of(t * SUB, SUB)
            kit = ki_ref[0, pl.ds(off, SUB), :]
            if packed:
                x = jnp.dot(kit, qit_ref[0], preferred_element_type=F32)
                x = jnp.maximum(x, 0.0) * wit_ref[0]
                sc = x[:, 0:tq] + x[:, tq:2 * tq]
                shift = tq // 2
                while shift >= 2 * tq // H_IDX:
                    sc = sc + pltpu.roll(sc, shift, axis=1)
                    shift //= 2
            else:
                sc = jnp.zeros((SUB, tq), F32)
                for j in range(H_IDX):
                    sj = jnp.dot(kit, qit_ref[0, j * D_IDX:(j + 1) * D_IDX, :], preferred_element_type=F32)
                    sc = sc + jnp.maximum(sj, 0.0) * wit_ref[0, j:j + 1, :]
            bits = lax.bitcast_convert_type(sc, I32)
            return bits ^ ((bits >> 31) | INT_MIN)

        def slice_codes(t, code, all_admissible):
            planes = _bit_transpose32([code[8 * k:8 * k + 8] for k in range(32)])
            row = pl.multiple_of(t * 8, 8)
            for k in range(32):
                plane_ref[k, pl.ds(row, 8), :] = planes[k]
            if all_admissible:
                adm_ref[pl.ds(row, 8), :] = jnp.full((8, tq), -1, I32)
            else:
                kpos = t * SUB + krow
                adm = ((kpos >> 6) <= qchunk) & (kpos < l_valid)
                aw = jnp.zeros((8, tq), I32)
                for k in range(32):
                    aw = aw | jnp.where(adm[8 * k:8 * k + 8], _bit(k), 0)
                adm_ref[pl.ds(row, 8), :] = aw

        n_open = jnp.minimum(l_valid, ((q_pos0 >> 6) + 1) * CHUNK) // SUB

        def score_tiles(gi, c):
            for u in range(A_UNROLL):
                slice_codes(gi * A_UNROLL + u, score_codes(gi * A_UNROLL + u), True)
            return c

        def score_tile(t, c):
            slice_codes(t, score_codes(t), False)
            return c

        lax.fori_loop(0, n_open // A_UNROLL, score_tiles, 0)
        lax.fori_loop(n_open // A_UNROLL * A_UNROLL, nsub, score_tile, 0)

        nchunk = (nsub + WCH - 1) // WCH

        def clear_tile(t, c):
            row = pl.multiple_of(t * 8, 8)
            plane_ref[:, pl.ds(row, 8), :] = jnp.zeros((32, 8, tq), I32)
            adm_ref[pl.ds(row, 8), :] = jnp.zeros((8, tq), I32)
            return c

        lax.fori_loop(nsub, nchunk * WCH, clear_tile, 0)

        def chunk_rows(cn):
            return pl.ds(pl.multiple_of(cn * (8 * WCH), 8 * WCH), 8 * WCH)

        def init_chunk(cn, c):
            cand_ref[chunk_rows(cn), :] = adm_ref[chunk_rows(cn), :]
            sel_ref[chunk_rows(cn), :] = jnp.zeros((8 * WCH, tq), I32)
            return c

        lax.fori_loop(0, nchunk, init_chunk, 0)

        def lane_total(cnt):
            return jnp.sum(fold(cnt.astype(F32), jnp.add), axis=0, keepdims=True).astype(I32)

        def count_cand(plane_of):
            def count_chunk(cn, cnt):
                ones = cand_ref[chunk_rows(cn), :] & plane_of(cn)
                return cnt + lax.population_count(ones)

            return lane_total(lax.fori_loop(0, nchunk, count_chunk, jnp.zeros((8 * WCH, tq), I32)))

        def radix_step(plane_of, n_sel, n_one, next_plane_of):
            take = (n_sel + n_one) >= topk

            def update_chunk(cn, cnt):
                cw = cand_ref[chunk_rows(cn), :]
                pw = plane_of(cn)
                ones = cw & pw
                kept = jnp.where(take, ones, cw & ~pw)
                cand_ref[chunk_rows(cn), :] = kept
                sel_ref[chunk_rows(cn), :] = sel_ref[chunk_rows(cn), :] | jnp.where(take, 0, ones)
                return cnt + lax.population_count(kept & next_plane_of(cn))

            cnt = lax.fori_loop(0, nchunk, update_chunk, jnp.zeros((8 * WCH, tq), I32))
            return jnp.where(take, n_sel, n_sel + n_one), lane_total(cnt)

        def code_plane(b):
            return lambda cn: plane_ref[b, chunk_rows(cn), :]

        def code_step(it, st):
            return radix_step(code_plane(31 - it), st[0], st[1], code_plane(jnp.maximum(30 - it, 0)))

        n_sel, _ = lax.fori_loop(0, 32, code_step, (jnp.zeros((1, tq), I32), count_cand(code_plane(31))))

        n_tie = count_cand(lambda cn: jnp.int32(-1))
        crowded = jnp.max(jnp.where((n_sel + n_tie > topk) & (n_tie > 1), 1, 0))

        @pl.when(crowded > 0)
        def _():
            wrow = lax.broadcasted_iota(I32, (8 * WCH, tq), 0)
            index_planes = []
            for tb in reversed(range(max(1, (plane_ref.shape[1] // 8 - 1).bit_length()))):
                index_planes.append(lambda cn, tb=tb: (((cn * WCH + (wrow >> 3)) >> tb) & 1) - 1)
            for low in (0x0000FFFF, 0x00FF00FF, 0x0F0F0F0F, 0x33333333, 0x55555555):
                index_planes.append(lambda cn, low=low: jnp.int32(low))
            for rb in (2, 1, 0):
                index_planes.append(lambda cn, rb=rb: ((wrow >> rb) & 1) - 1)
            index_planes.append(index_planes[-1])
            n, n_one = n_sel, count_cand(index_planes[0])
            for this_plane, next_plane in zip(index_planes[:-1], index_planes[1:]):
                n, n_one = radix_step(this_plane, n, n_one, next_plane)

        def final_chunk(cn, c):
            sel_ref[chunk_rows(cn), :] = sel_ref[chunk_rows(cn), :] | cand_ref[chunk_rows(cn), :]
            return c

        lax.fori_loop(0, nchunk, final_chunk, 0)

    n_here = jnp.clip(nsub - kt * subs, 0, subs)

    def key_block(u, nsubs, near):
        nk = nsubs * SUB
        g = kt * subs + u
        dsub = g - i_sub
        uoff = pl.multiple_of(u * SUB, SUB)
        for a in range(nsubs):
            sw = sel_ref[pl.ds(pl.multiple_of((g + a) * 8, 8), 8), :]
            for k in range(32):
                r = a * SUB + 8 * k
                mb_ref[r:r + 8, :] = jnp.where((sw & _bit(k)) != 0, 0.0, 2.0 * NEG)
        def logits_sweep(h):
            hs = slice(h * HD_A, (h + 1) * HD_A)
            kh = k_ref[0, pl.ds(uoff, nk), hs]
            lg = jnp.dot(kh, qt_ref[0, hs, :], preferred_element_type=F32)
            if near:
                lg = lg + bias_ref[-dsub, h, :, 0:tq]
            lg = lg + mb_ref[0:nk, :]
            lg_ref[h, 0:nk, :] = lg
            m_old = m_ref[h]
            m_new = jnp.maximum(m_old, jnp.max(fold(lg, jnp.maximum), axis=0, keepdims=True))
            al_ref[h] = jnp.exp2(m_old - m_new)
            m_ref[h] = m_new

        def values_sweep(h):
            hs = slice(h * HD_A, (h + 1) * HD_A)
            vth = vt_ref[0, hs, pl.ds(uoff, nk)]
            alpha = al_ref[h]
            p = jnp.exp2((lg_ref[h, 0:nk, :] - m_ref[h][0:1, :]).astype(BF16))
            pv = jnp.dot(jnp.concatenate([vth, jnp.ones((16, nk), BF16)], axis=0), p,
                         preferred_element_type=F32)
            l_ref[h] = alpha * l_ref[h] + pv[HD_A:HD_A + 8, :]
            acc_ref[hs, :] = alpha[0:1, :] * acc_ref[hs, :] + pv[0:HD_A, :]

        for h in range(N_A):
            logits_sweep(h)
        for h in range(N_A):
            values_sweep(h)

    n_far = jnp.clip(i_sub - 1 - kt * subs, 0, n_here)

    def near_one(u, c):
        key_block(u, 1, True)
        return c

    done = 0
    width = FAR_SUBS
    while width > subs:
        width //= 2
    while width >= 1:
        def far_block(bi, c, width=width, done=done):
            key_block(done + bi * width, width, False)
            return c

        count = (n_far - done) // width
        lax.fori_loop(0, count, far_block, 0)
        done = done + count * width
        width //= 2
    lax.fori_loop(n_far, n_here, near_one, 0)

    @pl.when(tl_ref[s] == 1)
    def _():
        for h in range(N_A):
            hs = slice(h * HD_A, (h + 1) * HD_A)
            o_ref[0, hs, :] = (acc_ref[hs, :] / l_ref[h][0:1, :]).astype(BF16)


def _dsa(qt, qit, wit, ki, k, vt, bias, *, tq, tk, q_off, l_valid, topk, packed=False):
    bsz, sq = qt.shape[0], qt.shape[2]
    sk = k.shape[1]
    assert sq % tq == 0 and sk % tk == 0 and tk % SUB == 0 and q_off % SUB == 0
    assert (tq % SUB == 0 or sq == tq) and tq % LANES == 0 and (not packed or sq == tq)
    subs = tk // SUB
    wrows = -(-(sk // SUB) // WCH) * WCH * 8
    ti, tkk, tn, tl = [], [], [], []
    for i in range(sq // tq):
        q_last = q_off + (i + 1) * tq - 1
        lim = min(l_valid, (q_last // CHUNK + 1) * CHUNK)
        nsub = -(-lim // SUB)
        nkt = -(-nsub // subs)
        for t in range(nkt):
            ti.append(i), tkk.append(t), tn.append(nsub), tl.append(int(t == nkt - 1))
    tabs = [jnp.asarray(np.asarray(a, np.int32)) for a in (ti, tkk, tn, tl)]
    body = functools.partial(_dsa_body, tq=tq, tk=tk, q_off=q_off, l_valid=l_valid, topk=topk, packed=packed)
    qmap = lambda b, s, ti, tk_, tn, tl: (b, 0, ti[s])
    if packed:
        idx_specs = [pl.BlockSpec((1, D_IDX, 2 * tq), qmap), pl.BlockSpec((1, 1, 2 * tq), qmap)]
    else:
        idx_specs = [pl.BlockSpec((1, D_QI, tq), qmap), pl.BlockSpec((1, H_IDX, tq), qmap)]
    grid_spec = pltpu.PrefetchScalarGridSpec(
        num_scalar_prefetch=4,
        grid=(bsz, len(ti)),
        in_specs=[pl.BlockSpec((1, D_A, tq), qmap),
                  *idx_specs,
                  pl.BlockSpec((1, sk, D_IDX), lambda b, s, *_: (b, 0, 0), pipeline_mode=pl.Buffered(1)),
                  pl.BlockSpec((1, tk, D_A), lambda b, s, ti, tk_, tn, tl: (b, tk_[s], 0)),
                  pl.BlockSpec((1, D_A, tk), lambda b, s, ti, tk_, tn, tl: (b, 0, tk_[s])),
                  pl.BlockSpec((2, N_A, SUB, SUB), lambda b, s, *_: (0, 0, 0, 0))],
        out_specs=pl.BlockSpec((1, D_A, tq), qmap),
        scratch_shapes=[pltpu.VMEM((32, wrows, tq), I32),
                        pltpu.VMEM((wrows, tq), I32),
                        pltpu.VMEM((wrows, tq), I32),
                        pltpu.VMEM((wrows, tq), I32),
                        pltpu.VMEM((FAR_SUBS * SUB, tq), F32),
                        pltpu.VMEM((N_A, 8, tq), F32),
                        pltpu.VMEM((N_A, 8, tq), F32),
                        pltpu.VMEM((D_A, tq), F32),
                        pltpu.VMEM((N_A, FAR_SUBS * SUB, tq), F32),
                        pltpu.VMEM((N_A, 8, tq), F32)])
    return pl.pallas_call(
        body,
        grid_spec=grid_spec,
        out_shape=jax.ShapeDtypeStruct((bsz, D_A, sq), BF16),
        compiler_params=_cparams(("parallel", "arbitrary")),
    )(*tabs, qt, qit, wit, ki, k, vt, bias)


def _pack_cache_body(k4_ref, v4_ref, kn_ref, vnt_ref, ko_ref, vto_ref, *, ncache):
    c = pl.program_id(1)

    @pl.when(c < ncache)
    def _():
        for h in range(N_A):
            hs = slice(h * HD_A, (h + 1) * HD_A)
            ko_ref[0, :, hs] = k4_ref[0, pl.ds(h, SUB, stride=N_A), :].astype(BF16)
            vto_ref[0, hs, :] = v4_ref[0, pl.ds(h, SUB, stride=N_A), :].T.astype(BF16)

    @pl.when(c == ncache)
    def _():
        ko_ref[0] = kn_ref[0]
        vto_ref[0] = vnt_ref[0]


def _pack_cache(cache_k, cache_v, k_new, vt_new):
    bsz, past = cache_k.shape[0], cache_k.shape[1]
    ncache = past // SUB
    cmap = lambda b, c: (b, jnp.minimum(c, ncache - 1), 0)
    rows = lambda a: a.reshape(bsz, past * N_A, HD_A)
    return pl.pallas_call(
        functools.partial(_pack_cache_body, ncache=ncache),
        grid=(bsz, ncache + 1),
        in_specs=[pl.BlockSpec((1, SUB * N_A, HD_A), cmap),
                  pl.BlockSpec((1, SUB * N_A, HD_A), cmap),
                  pl.BlockSpec((1, SUB, D_A), lambda b, c: (b, 0, 0)),
                  pl.BlockSpec((1, D_A, SUB), lambda b, c: (b, 0, 0))],
        out_specs=[pl.BlockSpec((1, SUB, D_A), lambda b, c: (b, c, 0)),
                   pl.BlockSpec((1, D_A, SUB), lambda b, c: (b, 0, c))],
        out_shape=[jax.ShapeDtypeStruct((bsz, past + SUB, D_A), BF16),
                   jax.ShapeDtypeStruct((bsz, D_A, past + SUB), BF16)],
        compiler_params=_cparams(("parallel", "arbitrary")),
    )(rows(cache_k), rows(cache_v), k_new, vt_new)


def _mlstm_body(q_ref, k_ref, kt_ref, v_ref, og_ref, sm_ref, gr_ref, gh_ref, c0_ref, n0_ref, m0_ref,
                hb_ref, co_ref, no_ref, mo_ref, c_s, n_s, m_s, *, lc, bb):
    c = pl.program_id(1)

    @pl.when(c == 0)
    def _():
        for s in range(bb):
            for h in range(N_B):
                c_s[s * N_B + h] = c0_ref[s, h]
                n_s[s * N_B + h] = jnp.broadcast_to(n0_ref[s, h:h + 1, :], (8, HD_B))
                m_s[s * N_B + h] = jnp.broadcast_to(m0_ref[s, h:h + 1, :], (8, LANES))

    row = lax.broadcasted_iota(I32, (lc, lc), 0)
    col = lax.broadcasted_iota(I32, (lc, lc), 1)
    causal = col <= row
    hp = lax.Precision.HIGHEST
    lower = causal.astype(F32)
    upper = (row <= col).astype(F32)

    for s in range(bb):
        sm = sm_ref[s]
        gr = gr_ref[s]
        b_cols = jnp.dot(lower, sm, precision=hp, preferred_element_type=F32)
        b_rows = jnp.dot(gr, upper, precision=hp, preferred_element_type=F32)
        for h in range(N_B):
            sh = s * N_B + h
            hs = slice(h * HD_B, (h + 1) * HD_B)
            m_prev = m_s[sh][0:1, 0:1]
            bc = b_cols[:, SM_LF + h:SM_LF + h + 1]
            ig_c = sm[:, SM_IG + h:SM_IG + h + 1]
            br = b_rows[N_B + h:N_B + h + 1, :]
            ig_r = gr[h:h + 1, :]
            log_d = jnp.where(causal, bc - br + ig_r, NEG)
            log_inter = bc + m_prev
            m_t = jnp.maximum(log_inter, jnp.max(log_d, axis=1, keepdims=True))
            d = jnp.exp(log_d - m_t)
            inter = jnp.exp(log_inter - m_t)
            qh = q_ref[s, :, hs]
            kth = kt_ref[s, hs, :]
            vh = v_ref[s, :, hs]
            sc = jnp.dot(qh, kth, preferred_element_type=F32) * d
            ch = c_s[sh]
            nrow = n_s[sh][0:1, :]
            num = (jnp.dot(sc.astype(BF16), vh, preferred_element_type=F32)
                   + inter * jnp.dot(qh, ch.astype(BF16), preferred_element_type=F32))
            den = (jnp.sum(sc, axis=1, keepdims=True)
                   + inter * jnp.sum(qh.astype(F32) * nrow, axis=1, keepdims=True))
            hh = num / jnp.maximum(jnp.abs(den), jnp.exp(-m_t))
            hb = _rms(hh, gh_ref[...]) * og_ref[s, :, hs].astype(F32)
            hb_ref[s, :, hs] = hb.astype(BF16)

            m_new = m_t[lc - 1:lc, :]
            b_last = bc[lc - 1:lc, :]
            decay = jnp.exp(b_last + m_prev - m_new)
            w_r = jnp.exp(b_last - br + ig_r - m_new)
            w_c = jnp.exp(b_last - bc + ig_c - m_new)
            kw = (kth.astype(F32) * w_r).astype(BF16)
            c_s[sh] = decay * ch + jnp.dot(kw, vh, preferred_element_type=F32)
            n_new = decay * nrow + jnp.sum(k_ref[s, :, hs].astype(F32) * w_c, axis=0, keepdims=True)
            n_s[sh] = jnp.broadcast_to(n_new, (8, HD_B))
            m_s[sh] = jnp.broadcast_to(m_new, (8, LANES))

    @pl.when(c == pl.num_programs(1) - 1)
    def _():
        for s in range(bb):
            for h in range(N_B):
                co_ref[s, h] = c_s[s * N_B + h]
                no_ref[s, h:h + 1, :] = n_s[s * N_B + h][0:1, :]
                mo_ref[s, h:h + 1, :] = m_s[s * N_B + h][0:1, :]


def _mlstm(q, k, kt, v, og, sm, gr, g_h, c0, n0, m0, *, lc):
    bsz, s = q.shape[0], q.shape[1]
    bb = 1
    tokb = lambda w: pl.BlockSpec((bb, lc, w), lambda b, c: (b, c, 0))
    st3 = lambda w: pl.BlockSpec((bb, N_B, w), lambda b, c: (b, 0, 0))
    st4 = pl.BlockSpec((bb, N_B, HD_B, HD_B), lambda b, c: (b, 0, 0, 0))
    return pl.pallas_call(
        functools.partial(_mlstm_body, lc=lc, bb=bb),
        grid=(bsz // bb, s // lc),
        in_specs=[tokb(D_B), tokb(D_B),
                  pl.BlockSpec((bb, D_B, lc), lambda b, c: (b, 0, c)),
                  tokb(D_B), tokb(D_B), tokb(LANES),
                  pl.BlockSpec((bb, 2 * N_B, lc), lambda b, c: (b, 0, c)),
                  pl.BlockSpec((1, HD_B), lambda b, c: (0, 0)),
                  st4, st3(HD_B), st3(LANES)],
        out_specs=[tokb(D_B), st4, st3(HD_B), st3(LANES)],
        out_shape=[jax.ShapeDtypeStruct((bsz, s, D_B), BF16),
                   jax.ShapeDtypeStruct((bsz, N_B, HD_B, HD_B), F32),
                   jax.ShapeDtypeStruct((bsz, N_B, HD_B), F32),
                   jax.ShapeDtypeStruct((bsz, N_B, LANES), F32)],
        scratch_shapes=[pltpu.VMEM((bb * N_B, HD_B, HD_B), F32),
                        pltpu.VMEM((bb * N_B, 8, HD_B), F32),
                        pltpu.VMEM((bb * N_B, 8, LANES), F32)],
        compiler_params=_cparams(("parallel", "arbitrary")),
    )(q, k, kt, v, og, sm, gr, g_h, c0, n0, m0)


def _mixer_front(x, ffn1, inp):
    x1 = _ffn(x, *ffn1)
    return (x1,) + tuple(_inproj(x1, *inp))


def kernel(x_prompt, x_sample, cache_k, cache_v, cache_k_idx, state_C, state_n, state_m, g_ffn1, w1_ffn1, w3_ffn1, w2_ffn1, g_mix, w_in, g_q, g_k, g_kidx, rel_bias, b_i, b_f, g_h, w_out, g_ffn2, w1_ffn2, w3_ffn2, w2_ffn2):
    bsz, seq = x_prompt.shape[0], x_prompt.shape[1]
    dbs, dseq = x_sample.shape[0], x_sample.shape[1]
    past = cache_k.shape[1]

    ffn1 = _prep_ffn(g_ffn1, w1_ffn1, w3_ffn1, w2_ffn1)
    ffn2 = _prep_ffn(g_ffn2, w1_ffn2, w3_ffn2, w2_ffn2)
    inp = _prep_inproj(g_mix, w_in, g_q, g_k, g_kidx, b_i, b_f)
    wa = w_out[:D_A].astype(BF16)
    wh = w_out[D_A:].astype(BF16)
    gh = g_h.reshape(1, HD_B)
    bias = _bias_tiles(rel_bias)

    n_p = bsz * seq
    x1, q, kf, k16, vf, v16, qi, qb, kb, vb, og, sm = _mixer_front(x_prompt.reshape(n_p, D_MODEL), ffn1, inp)
    r3 = lambda a, b_, s_: a.reshape(b_, s_, a.shape[-1])
    sm3 = r3(sm, bsz, seq)
    tr = lambda a: jnp.swapaxes(a, 1, 2)
    tq_p = min(SUB, seq)
    attn = tr(_dsa(tr(r3(q, bsz, seq)), tr(r3(qi, bsz, seq)), tr(sm3[:, :, SM_WI:SM_IG]),
                   sm3[:, :, :D_IDX].astype(BF16), r3(k16, bsz, seq), tr(r3(v16, bsz, seq)), bias,
                   tq=tq_p, tk=min(DSA_KEY_TILE, seq), q_off=0, l_valid=seq, topk=min(TOPK_MAX, seq // 4)))
    gr = jnp.swapaxes(sm3[:, :, SM_IG:SM_END], 1, 2)
    kb3 = r3(kb, bsz, seq)
    hb, c_p, n_p_, m_p = _mlstm(r3(qb, bsz, seq), kb3, jnp.swapaxes(kb3, 1, 2), r3(vb, bsz, seq),
                                r3(og, bsz, seq), sm3, gr, gh,
                                jnp.zeros((bsz, N_B, HD_B, HD_B), F32), jnp.zeros((bsz, N_B, HD_B), F32),
                                jnp.zeros((bsz, N_B, LANES), F32), lc=min(MLSTM_CHUNK, seq))
    x2 = _outproj(x1, attn.reshape(n_p, D_A), hb.reshape(n_p, D_B), wa, wh)
    y_prompt = _ffn(x2, *ffn2).reshape(bsz, seq, D_MODEL)

    n_s = dbs * dseq
    x1s, qs, kfs, k16s, vfs, v16s, qis, qbs, kbs, vbs, ogs, sms = _mixer_front(
        x_sample.reshape(n_s, D_MODEL), ffn1, inp)
    sms3 = r3(sms, dbs, dseq)
    l_all = past + dseq
    tk_s = 3 * SUB
    sk = -(-l_all // tk_s) * tk_s
    padk = lambda a: jnp.pad(a, ((0, 0), (0, sk - l_all), (0, 0)))
    assert past % SUB == 0 and sk - past == SUB
    padn = lambda a: jnp.pad(a, ((0, 0), (0, SUB - dseq), (0, 0)))
    k_all, vt_all = _pack_cache(cache_k, cache_v, padn(r3(k16s, dbs, dseq)), tr(padn(r3(v16s, dbs, dseq))))
    ki_all = padk(jnp.concatenate([cache_k_idx.astype(BF16), sms3[:, :, :D_IDX].astype(BF16)], axis=1))
    assert dseq <= LANES
    padq = lambda a: jnp.pad(tr(a), ((0, 0), (0, 0), (0, LANES - dseq)))
    qi_s, wi_s = r3(qis, dbs, dseq), sms3[:, :, SM_WI:SM_IG]
    packed = H_IDX * dseq == 2 * LANES
    if packed:
        qit_s = qi_s.reshape(dbs, dseq, H_IDX, D_IDX).transpose(0, 3, 2, 1).reshape(dbs, D_IDX, 2 * LANES)
        wit_s = tr(wi_s).reshape(dbs, 1, 2 * LANES)
    else:
        qit_s, wit_s = padq(qi_s), padq(wi_s)
    attn_s = tr(_dsa(padq(r3(qs, dbs, dseq)), qit_s, wit_s, ki_all, k_all, vt_all, bias,
                     tq=LANES, tk=tk_s, q_off=past, l_valid=l_all, topk=min(TOPK_MAX, l_all // 4),
                     packed=packed)[:, :, :dseq])
    lc_s = LANES
    padt = lambda a: jnp.pad(a, ((0, 0), (0, lc_s - dseq), (0, 0)))
    lane = jnp.arange(LANES)
    sm_pad = jnp.where((lane >= SM_IG) & (lane < SM_LF), NEG, 0.0).astype(F32)
    sms_p = jnp.concatenate([sms3, jnp.broadcast_to(sm_pad, (dbs, lc_s - dseq, LANES))], axis=1)
    kbs3 = padt(r3(kbs, dbs, dseq))
    hbs, c_s, n_s_, m_s = _mlstm(padt(r3(qbs, dbs, dseq)), kbs3, jnp.swapaxes(kbs3, 1, 2), padt(r3(vbs, dbs, dseq)),
                                 padt(r3(ogs, dbs, dseq)), sms_p, jnp.swapaxes(sms_p[:, :, SM_IG:SM_END], 1, 2), gh,
                                 state_C.astype(F32), state_n.astype(F32),
                                 jnp.broadcast_to(state_m.astype(F32)[:, :, None], (dbs, N_B, LANES)), lc=lc_s)
    x2s = _outproj(x1s, attn_s.reshape(n_s, D_A), hbs[:, :dseq].reshape(n_s, D_B), wa, wh)
    y_sample = _ffn(x2s, *ffn2).reshape(dbs, dseq, D_MODEL)

    sd = state_C.dtype
    return (y_prompt, y_sample,
            kf.reshape(bsz, seq, N_A, HD_A), vf.reshape(bsz, seq, N_A, HD_A), sm3[:, :, :D_IDX],
            c_p.astype(sd), n_p_.astype(sd), m_p[:, :, 0].astype(sd),
            kfs.reshape(dbs, dseq, N_A, HD_A), vfs.reshape(dbs, dseq, N_A, HD_A), sms3[:, :, :D_IDX],
            c_s.astype(sd), n_s_.astype(sd), m_s[:, :, 0].astype(sd))
```

```python
import functools
import math

import numpy as np
import jax
import jax.numpy as jnp
from jax import lax
from jax.experimental import pallas as pl
from jax.experimental.pallas import tpu as pltpu

F32 = jnp.float32
BF16 = jnp.bfloat16
I32 = jnp.int32

D_MODEL = 2048
CHUNK = 64
N_A, HD_A = 8, 128
H_IDX, D_IDX = 16, 64
TOPK_MAX = 256
N_B, HD_B = 4, 256
D_FF = 5504
NUM_BUCKETS, MAX_DISTANCE = 32, 128
EPS = 1e-6
NEG = -1e30
INT_MIN = -(2 ** 31)
LOG2E = math.log2(math.e)

LANES = 128
D_A = N_A * HD_A
D_B = N_B * HD_B
D_QI = H_IDX * D_IDX
FF_TILE = 512
D_FF_PAD = -(-D_FF // FF_TILE) * FF_TILE
FFN_TOK_TILE = 512
PROJ_TOK_TILE = 512
TOK_TILE = 512
SUB = 256
PACK_FRAMES = 512
MLSTM_CHUNK = 256
VMEM_LIMIT = 56 * 1024 * 1024

SM_WI, SM_IG, SM_LF = D_IDX, D_IDX + H_IDX, D_IDX + H_IDX + N_B
SM_END = SM_LF + N_B


def _cparams(sem):
    return pltpu.CompilerParams(dimension_semantics=sem, vmem_limit_bytes=VMEM_LIMIT)


def _rms(x, g):
    ms = jnp.mean(x * x, axis=-1, keepdims=True)
    return x * lax.rsqrt(ms + EPS) * g


def _ffn_body(x_ref, g_ref, w1_ref, w3_ref, w2_ref, o_ref, hn_ref):
    j = pl.program_id(1)

    @pl.when(j == 0)
    def _():
        hn_ref[...] = _rms(x_ref[...], g_ref[...]).astype(BF16)
        o_ref[...] = jnp.zeros(o_ref.shape, F32)

    h = hn_ref[...]
    a = jnp.dot(h, w1_ref[...], preferred_element_type=F32)
    b = jnp.dot(h, w3_ref[...], preferred_element_type=F32)
    u = (a * jax.nn.sigmoid(a) * b).astype(BF16)
    o_ref[...] += jnp.dot(u, w2_ref[...], preferred_element_type=F32)

    @pl.when(j == pl.num_programs(1) - 1)
    def _():
        o_ref[...] = x_ref[...] + 0.5 * o_ref[...]


def _ffn(x, g, w1, w3, w2):
    n = x.shape[0]
    tm = min(FFN_TOK_TILE, n)
    grid = (n // tm, D_FF_PAD // FF_TILE)
    return pl.pallas_call(
        _ffn_body,
        grid=grid,
        in_specs=[
            pl.BlockSpec((tm, D_MODEL), lambda i, j: (i, 0)),
            pl.BlockSpec((1, D_MODEL), lambda i, j: (0, 0)),
            pl.BlockSpec((D_MODEL, FF_TILE), lambda i, j: (0, j)),
            pl.BlockSpec((D_MODEL, FF_TILE), lambda i, j: (0, j)),
            pl.BlockSpec((FF_TILE, D_MODEL), lambda i, j: (j, 0)),
        ],
        out_specs=pl.BlockSpec((tm, D_MODEL), lambda i, j: (i, 0)),
        out_shape=jax.ShapeDtypeStruct((n, D_MODEL), F32),
        scratch_shapes=[pltpu.VMEM((tm, D_MODEL), BF16)],
        compiler_params=_cparams(("parallel", "arbitrary")),
    )(x, g, w1, w3, w2)


def _cast_pad_body(x_ref, o_ref, *, blocks_valid, cols_valid):
    x = jnp.where(pl.program_id(0) < blocks_valid, x_ref[...], 0.0)
    o_ref[:, :cols_valid] = x.astype(BF16)
    if cols_valid < o_ref.shape[1]:
        o_ref[:, cols_valid:] = jnp.zeros((o_ref.shape[0], o_ref.shape[1] - cols_valid), BF16)


def _cast_pad(w, rows, cols, block_rows):
    r, c = w.shape
    assert rows % block_rows == 0 and r % block_rows == 0 and c % LANES == 0 and cols >= c and rows >= r
    nvalid = r // block_rows
    body = functools.partial(_cast_pad_body, blocks_valid=nvalid, cols_valid=c)
    return pl.pallas_call(
        body,
        grid=(rows // block_rows,),
        in_specs=[pl.BlockSpec((block_rows, c), lambda i: (jnp.minimum(i, nvalid - 1), 0))],
        out_specs=pl.BlockSpec((block_rows, cols), lambda i: (i, 0)),
        out_shape=jax.ShapeDtypeStruct((rows, cols), BF16),
        compiler_params=_cparams(("parallel",)),
    )(w)


def _prep_ffn(g, w1, w3, w2):
    return (g.reshape(1, D_MODEL),
            _cast_pad(w1, D_MODEL, D_FF_PAD, 256),
            _cast_pad(w3, D_MODEL, D_FF_PAD, 256),
            _cast_pad(w2, D_FF_PAD, D_MODEL, LANES))


def _head_norm(p, g, heads, hd):
    outs = []
    for h in range(heads):
        outs.append(_rms(p[:, h * hd:(h + 1) * hd], g))
    return outs


def _inproj_a_body(x_ref, g_ref, w_ref, wsh_ref, wsl_ref, gq_ref, gk_ref, sg_ref, sb_ref,
                   q_ref, kf_ref, k16_ref, vf_ref, v16_ref, qi_ref, sm_ref):
    h32 = _rms(x_ref[...], g_ref[...])
    h = h32.astype(BF16)
    col = lambda c: jnp.dot(h, w_ref[:, c * D_A:(c + 1) * D_A], preferred_element_type=F32)

    h_lo = (h32 - h.astype(F32)).astype(BF16)
    ps = (jnp.dot(h, wsh_ref[...], preferred_element_type=F32)
          + jnp.dot(h_lo, wsh_ref[...], preferred_element_type=F32)
          + jnp.dot(h, wsl_ref[...], preferred_element_type=F32))
    lane = lax.broadcasted_iota(I32, ps.shape, 1)
    ms = jnp.sum(jnp.where(lane < SM_WI, ps * ps, 0.0), axis=-1, keepdims=True) * (1.0 / D_IDX)
    kin = ps * lax.rsqrt(ms + EPS) * sg_ref[...]
    z = ps + sb_ref[...]
    ls = jnp.minimum(z, 0.0) - jnp.log1p(jnp.exp(-jnp.abs(z)))
    sm_ref[...] = jnp.where(lane < SM_WI, kin,
                            jnp.where(lane < SM_IG, ps,
                                      jnp.where(lane < SM_LF, z,
                                                jnp.where(lane < SM_END, ls, 0.0))))

    for hd, qh in enumerate(_head_norm(col(0), gq_ref[...], N_A, HD_A)):
        q_ref[:, hd * HD_A:(hd + 1) * HD_A] = (qh * (HD_A ** -0.5 * LOG2E)).astype(BF16)
    for hd, kh in enumerate(_head_norm(col(1), gk_ref[...], N_A, HD_A)):
        kf_ref[:, hd * HD_A:(hd + 1) * HD_A] = kh
        k16_ref[:, hd * HD_A:(hd + 1) * HD_A] = kh.astype(BF16)
    p = col(2)
    vf_ref[...] = p
    v16_ref[...] = p.astype(BF16)
    qi_ref[...] = col(3).astype(BF16)


def _inproj_b_body(x_ref, g_ref, w_ref, qb_ref, kb_ref, vb_ref, og_ref):
    h = _rms(x_ref[...], g_ref[...]).astype(BF16)
    col = lambda c: jnp.dot(h, w_ref[:, c * D_B:(c + 1) * D_B], preferred_element_type=F32)
    qb_ref[...] = col(0).astype(BF16)
    kb_ref[...] = (col(1) * HD_B ** -0.5).astype(BF16)
    vb_ref[...] = col(2).astype(BF16)
    og_ref[...] = jax.nn.sigmoid(col(3)).astype(BF16)


def _inproj(x, g, wa, wb, wsh, wsl, gq, gk, sg, sb):
    n = x.shape[0]
    tm = min(PROJ_TOK_TILE, n)
    tok = lambda w: pl.BlockSpec((tm, w), lambda i: (i, 0))
    const = lambda a: pl.BlockSpec(a.shape, lambda i: (0,) * a.ndim, pipeline_mode=pl.Buffered(1))
    wide = lambda dt: jax.ShapeDtypeStruct((n, D_A), dt)
    outs_a = pl.pallas_call(
        _inproj_a_body,
        grid=(n // tm,),
        in_specs=[tok(D_MODEL)] + [const(a) for a in (g, wa, wsh, wsl, gq, gk, sg, sb)],
        out_specs=[tok(D_A)] * 6 + [tok(LANES)],
        out_shape=[wide(BF16), wide(F32), wide(BF16), wide(F32), wide(BF16), wide(BF16),
                   jax.ShapeDtypeStruct((n, LANES), F32)],
        compiler_params=_cparams(("parallel",)),
    )(x, g, wa, wsh, wsl, gq, gk, sg, sb)
    outs_b = pl.pallas_call(
        _inproj_b_body,
        grid=(n // tm,),
        in_specs=[tok(D_MODEL), const(g), const(wb)],
        out_specs=[tok(D_B)] * 4,
        out_shape=[wide(BF16)] * 4,
        compiler_params=_cparams(("parallel",)),
    )(x, g, wb)
    q, kf, k16, vf, v16, qi, sm = outs_a
    return (q, kf, k16, vf, v16, qi) + tuple(outs_b) + (sm,)


def _prep_inproj(g_mix, w_in, g_q, g_k, g_kidx, b_i, b_f):
    o_ki = 4 * D_A
    o_qb = o_ki + D_IDX + H_IDX
    o_ib = o_qb + 4 * D_B
    wa = w_in[:, :o_ki].astype(BF16)
    wb = w_in[:, o_qb:o_ib].astype(BF16)
    ws = jnp.concatenate([w_in[:, o_ki:o_qb], w_in[:, o_ib:],
                          jnp.zeros((D_MODEL, LANES - SM_END), F32)], axis=1)
    wsh = ws.astype(BF16)
    wsl = (ws - wsh.astype(F32)).astype(BF16)
    sg = jnp.concatenate([g_kidx, jnp.ones((LANES - D_IDX,), F32)]).reshape(1, LANES)
    sb = jnp.concatenate([jnp.zeros((SM_IG,), F32), b_i, b_f,
                          jnp.zeros((LANES - SM_END,), F32)]).reshape(1, LANES)
    return (g_mix.reshape(1, D_MODEL), wa, wb, wsh, wsl,
            g_q.reshape(1, HD_A), g_k.reshape(1, HD_A), sg, sb)


def _outproj_body(x_ref, a_ref, h_ref, wa_ref, wh_ref, o_ref):
    o_ref[...] = (x_ref[...]
                  + jnp.dot(a_ref[...], wa_ref[...], preferred_element_type=F32)
                  + jnp.dot(h_ref[...], wh_ref[...], preferred_element_type=F32))


def _outproj(x, a, h, wa, wh):
    n = x.shape[0]
    tm = min(TOK_TILE, n)
    return pl.pallas_call(
        _outproj_body,
        grid=(n // tm,),
        in_specs=[pl.BlockSpec((tm, D_MODEL), lambda i: (i, 0)),
                  pl.BlockSpec((tm, D_A), lambda i: (i, 0)),
                  pl.BlockSpec((tm, D_B), lambda i: (i, 0)),
                  pl.BlockSpec((D_A, D_MODEL), lambda i: (0, 0)),
                  pl.BlockSpec((D_B, D_MODEL), lambda i: (0, 0))],
        out_specs=pl.BlockSpec((tm, D_MODEL), lambda i: (i, 0)),
        out_shape=jax.ShapeDtypeStruct((n, D_MODEL), F32),
        compiler_params=_cparams(("parallel",)),
    )(x, a, h, wa, wh)


def _bucket_thresholds():
    nb = NUM_BUCKETS // 2
    max_exact = nb // 2
    span = nb - max_exact
    ratio = MAX_DISTANCE // max_exact
    out = []
    for m in range(1, span):
        n = max_exact
        while n ** span < max_exact ** span * ratio ** m:
            n += 1
        out.append(n)
    return tuple(out)


def _bias_body(tbl_ref, o_ref):
    o = pl.program_id(0)
    h = pl.program_id(1)
    nb = NUM_BUCKETS // 2
    max_exact = nb // 2
    row = lax.broadcasted_iota(I32, (SUB, SUB), 0)
    col = lax.broadcasted_iota(I32, (SUB, SUB), 1)
    rel = row - o * SUB - col
    n = jnp.abs(rel)
    large = jnp.full((SUB, SUB), max_exact, I32)
    for t in _bucket_thresholds():
        large = large + (n >= t).astype(I32)
    bucket = jnp.where(rel > 0, nb, 0) + jnp.where(n < max_exact, n, large)
    val = jnp.zeros((SUB, SUB), F32)
    for bk in range(NUM_BUCKETS):
        val = jnp.where(bucket == bk, tbl_ref[bk, h], val)
    o_ref[0, 0] = (val - tbl_ref[nb - 1, h]) * LOG2E


def _bias_tiles(rel_bias):
    return pl.pallas_call(
        _bias_body,
        grid=(2, N_A),
        in_specs=[pl.BlockSpec(memory_space=pltpu.SMEM)],
        out_specs=pl.BlockSpec((1, 1, SUB, SUB), lambda o, h: (o, h, 0, 0)),
        out_shape=jax.ShapeDtypeStruct((2, N_A, SUB, SUB), F32),
    )(rel_bias)


WCH = 4
A_UNROLL = 4
FAR_SUBS = 2


def _bit(k):
    return INT_MIN if k == 31 else 1 << k


def _bit_transpose32(a):
    a = list(a)
    j, m = 16, 0x0000FFFF
    while j:
        for k in range(32):
            if not k & j:
                t = (lax.shift_right_logical(a[k], jnp.int32(j)) ^ a[k + j]) & m
                a[k + j] = a[k + j] ^ t
                a[k] = a[k] ^ jnp.left_shift(t, jnp.int32(j))
        j >>= 1
        m ^= m << j
    return a


def _dsa_body(ti_ref, tk_ref, tn_ref, tl_ref,
              qt_ref, qit_ref, wit_ref, ki_ref, k_ref, vt_ref, bias_ref, o_ref,
              plane_ref, adm_ref, cand_ref, sel_ref, mb_ref, m_ref, l_ref, acc_ref, lg_ref, al_ref,
              *, tq, tk, q_off, l_valid, topk, packed):
    s = pl.program_id(1)
    i = ti_ref[s]
    kt = tk_ref[s]
    nsub = tn_ref[s]
    subs = tk // SUB
    q_pos0 = q_off + i * tq
    i_sub = q_off // SUB + (i * tq) // SUB

    def fold(x, op, rows=8):
        parts = [x[r:r + rows] for r in range(0, x.shape[0], rows)]
        while len(parts) > 1:
            parts = [op(parts[a], parts[a + 1]) for a in range(0, len(parts), 2)]
        return parts[0]

    @pl.when(kt == 0)
    def _():
        m_ref[...] = jnp.full(m_ref.shape, NEG, F32)
        l_ref[...] = jnp.zeros(l_ref.shape, F32)
        acc_ref[...] = jnp.zeros(acc_ref.shape, F32)

        qchunk = (q_pos0 + lax.broadcasted_iota(I32, (SUB, tq), 1)) >> 6
        krow = lax.broadcasted_iota(I32, (SUB, tq), 0)

        def score_codes(t):
            off = pl.multiple_of(t * SUB, SUB)
            kit = ki_ref[0, pl.ds(off, SUB), :]
            if packed:
                x = jnp.dot(kit, qit_ref[0], preferred_element_type=F32)
                x = jnp.maximum(x, 0.0) * wit_ref[0]
                sc = x[:, 0:tq] + x[:, tq:2 * tq]
                shift = tq // 2
                while shift >= 2 * tq // H_IDX:
                    sc = sc + pltpu.roll(sc, shift, axis=1)
                    shift //= 2
            else:
                sc = jnp.zeros((SUB, tq), F32)
                for j in range(H_IDX):
                    sj = jnp.dot(kit, qit_ref[0, j * D_IDX:(j + 1) * D_IDX, :], preferred_element_type=F32)
                    sc = sc + jnp.maximum(sj, 0.0) * wit_ref[0, j:j + 1, :]
            bits = lax.bitcast_convert_type(sc, I32)
            return bits ^ ((bits >> 31) | INT_MIN)

        def slice_codes(t, code, all_admissible):
            planes = _bit_transpose32([code[8 * k:8 * k + 8] for k in range(32)])
            row = pl.multiple_of(t * 8, 8)
            for k in range(32):
                plane_ref[k, pl.ds(row, 8), :] = planes[k]
            if all_admissible:
                adm_ref[pl.ds(row, 8), :] = jnp.full((8, tq), -1, I32)
            else:
                kpos = t * SUB + krow
                adm = ((kpos >> 6) <= qchunk) & (kpos < l_valid)
                aw = jnp.zeros((8, tq), I32)
                for k in range(32):
                    aw = aw | jnp.where(adm[8 * k:8 * k + 8], _bit(k), 0)
                adm_ref[pl.ds(row, 8), :] = aw

        n_open = jnp.minimum(l_valid, ((q_pos0 >> 6) + 1) * CHUNK) // SUB

        def score_tiles(gi, c):
            for u in range(A_UNROLL):
                slice_codes(gi * A_UNROLL + u, score_codes(gi * A_UNROLL + u), True)
            return c

        def score_tile(t, c):
            slice_codes(t, score_codes(t), False)
            return c

        lax.fori_loop(0, n_open // A_UNROLL, score_tiles, 0)
        lax.fori_loop(n_open // A_UNROLL * A_UNROLL, nsub, score_tile, 0)

        nchunk = (nsub + WCH - 1) // WCH

        def clear_tile(t, c):
            row = pl.multiple_of(t * 8, 8)
            plane_ref[:, pl.ds(row, 8), :] = jnp.zeros((32, 8, tq), I32)
            adm_ref[pl.ds(row, 8), :] = jnp.zeros((8, tq), I32)
            return c

        lax.fori_loop(nsub, nchunk * WCH, clear_tile, 0)

        def chunk_rows(cn):
            return pl.ds(pl.multiple_of(cn * (8 * WCH), 8 * WCH), 8 * WCH)

        def init_chunk(cn, c):
            cand_ref[chunk_rows(cn), :] = adm_ref[chunk_rows(cn), :]
            sel_ref[chunk_rows(cn), :] = jnp.zeros((8 * WCH, tq), I32)
            return c

        lax.fori_loop(0, nchunk, init_chunk, 0)

        def lane_total(cnt):
            return jnp.sum(fold(cnt.astype(F32), jnp.add), axis=0, keepdims=True).astype(I32)

        def count_cand(plane_of):
            def count_chunk(cn, cnt):
                ones = cand_ref[chunk_rows(cn), :] & plane_of(cn)
                return cnt + lax.population_count(ones)

            return lane_total(lax.fori_loop(0, nchunk, count_chunk, jnp.zeros((8 * WCH, tq), I32)))

        def radix_step(plane_of, n_sel, n_one, next_plane_of):
            take = (n_sel + n_one) >= topk

            def update_chunk(cn, cnt):
                cw = cand_ref[chunk_rows(cn), :]
                pw = plane_of(cn)
                ones = cw & pw
                kept = jnp.where(take, ones, cw & ~pw)
                cand_ref[chunk_rows(cn), :] = kept
                sel_ref[chunk_rows(cn), :] = sel_ref[chunk_rows(cn), :] | jnp.where(take, 0, ones)
                return cnt + lax.population_count(kept & next_plane_of(cn))

            cnt = lax.fori_loop(0, nchunk, update_chunk, jnp.zeros((8 * WCH, tq), I32))
            return jnp.where(take, n_sel, n_sel + n_one), lane_total(cnt)

        def code_plane(b):
            return lambda cn: plane_ref[b, chunk_rows(cn), :]

        def code_step(it, st):
            return radix_step(code_plane(31 - it), st[0], st[1], code_plane(jnp.maximum(30 - it, 0)))

        n_sel, _ = lax.fori_loop(0, 32, code_step, (jnp.zeros((1, tq), I32), count_cand(code_plane(31))))

        n_tie = count_cand(lambda cn: jnp.int32(-1))
        crowded = jnp.max(jnp.where((n_sel + n_tie > topk) & (n_tie > 1), 1, 0))

        @pl.when(crowded > 0)
        def _():
            wrow = lax.broadcasted_iota(I32, (8 * WCH, tq), 0)
            index_planes = []
            for tb in reversed(range(max(1, (plane_ref.shape[1] // 8 - 1).bit_length()))):
                index_planes.append(lambda cn, tb=tb: (((cn * WCH + (wrow >> 3)) >> tb) & 1) - 1)
            for low in (0x0000FFFF, 0x00FF00FF, 0x0F0F0F0F, 0x33333333, 0x55555555):
                index_planes.append(lambda cn, low=low: jnp.int32(low))
            for rb in (2, 1, 0):
                index_planes.append(lambda cn, rb=rb: ((wrow >> rb) & 1) - 1)
            index_planes.append(index_planes[-1])
            n, n_one = n_sel, count_cand(index_planes[0])
            for this_plane, next_plane in zip(index_planes[:-1], index_planes[1:]):
                n, n_one = radix_step(this_plane, n, n_one, next_plane)

        def final_chunk(cn, c):
            sel_ref[chunk_rows(cn), :] = sel_ref[chunk_rows(cn), :] | cand_ref[chunk_rows(cn), :]
            return c

        lax.fori_loop(0, nchunk, final_chunk, 0)

    n_here = jnp.clip(nsub - kt * subs, 0, subs)

    def key_block(u, nsubs, near):
        nk = nsubs * SUB
        g = kt * subs + u
        dsub = g - i_sub
        uoff = pl.multiple_of(u * SUB, SUB)
        for a in range(nsubs):
            sw = sel_ref[pl.ds(pl.multiple_of((g + a) * 8, 8), 8), :]
            for k in range(32):
                r = a * SUB + 8 * k
                mb_ref[r:r + 8, :] = jnp.where((sw & _bit(k)) != 0, 0.0, 2.0 * NEG)
        def logits_sweep(h):
            hs = slice(h * HD_A, (h + 1) * HD_A)
            kh = k_ref[0, pl.ds(uoff, nk), hs]
            lg = jnp.dot(kh, qt_ref[0, hs, :], preferred_element_type=F32)
            if near:
                lg = lg + bias_ref[-dsub, h, :, 0:tq]
            lg = lg + mb_ref[0:nk, :]
            lg_ref[h, 0:nk, :] = lg
            m_old = m_ref[h]
            m_new = jnp.maximum(m_old, jnp.max(fold(lg, jnp.maximum), axis=0, keepdims=True))
            al_ref[h] = jnp.exp2(m_old - m_new)
            m_ref[h] = m_new

        def values_sweep(h):
            hs = slice(h * HD_A, (h + 1) * HD_A)
            vth = vt_ref[0, hs, pl.ds(uoff, nk)]
            alpha = al_ref[h]
            p = jnp.exp2((lg_ref[h, 0:nk, :] - m_ref[h][0:1, :]).astype(BF16))
            pv = jnp.dot(jnp.concatenate([vth, jnp.ones((16, nk), BF16)], axis=0), p,
                         preferred_element_type=F32)
            l_ref[h] = alpha * l_ref[h] + pv[HD_A:HD_A + 8, :]
            acc_ref[hs, :] = alpha[0:1, :] * acc_ref[hs, :] + pv[0:HD_A, :]

        for h in range(N_A):
            logits_sweep(h)
        for h in range(N_A):
            values_sweep(h)

    n_far = jnp.clip(i_sub - 1 - kt * subs, 0, n_here)

    def near_one(u, c):
        key_block(u, 1, True)
        return c

    done = 0
    width = FAR_SUBS
    while width > subs:
        width //= 2
    while width >= 1:
        def far_block(bi, c, width=width, done=done):
            key_block(done + bi * width, width, False)
            return c

        count = (n_far - done) // width
        lax.fori_loop(0, count, far_block, 0)
        done = done + count * width
        width //= 2
    lax.fori_loop(n_far, n_here, near_one, 0)

    @pl.when(tl_ref[s] == 1)
    def _():
        for h in range(N_A):
            hs = slice(h * HD_A, (h + 1) * HD_A)
            o_ref[0, hs, :] = (acc_ref[hs, :] / l_ref[h][0:1, :]).astype(BF16)


def _dsa(qt, qit, wit, ki, k, vt, bias, *, tq, tk, q_off, l_valid, topk, packed=False):
    bsz, sq = qt.shape[0], qt.shape[2]
    sk = k.shape[1]
    assert sq % tq == 0 and sk % tk == 0 and tk % SUB == 0 and q_off % SUB == 0
    assert (tq % SUB == 0 or sq == tq) and tq % LANES == 0 and (not packed or sq == tq)
    subs = tk // SUB
    wrows = -(-(sk // SUB) // WCH) * WCH * 8
    ti, tkk, tn, tl = [], [], [], []
    for i in range(sq // tq):
        q_last = q_off + (i + 1) * tq - 1
        lim = min(l_valid, (q_last // CHUNK + 1) * CHUNK)
        nsub = -(-lim // SUB)
        nkt = -(-nsub // subs)
        for t in range(nkt):
            ti.append(i), tkk.append(t), tn.append(nsub), tl.append(int(t == nkt - 1))
    tabs = [jnp.asarray(np.asarray(a, np.int32)) for a in (ti, tkk, tn, tl)]
    body = functools.partial(_dsa_body, tq=tq, tk=tk, q_off=q_off, l_valid=l_valid, topk=topk, packed=packed)
    qmap = lambda b, s, ti, tk_, tn, tl: (b, 0, ti[s])
    if packed:
        idx_specs = [pl.BlockSpec((1, D_IDX, 2 * tq), qmap), pl.BlockSpec((1, 1, 2 * tq), qmap)]
    else:
        idx_specs = [pl.BlockSpec((1, D_QI, tq), qmap), pl.BlockSpec((1, H_IDX, tq), qmap)]
    grid_spec = pltpu.PrefetchScalarGridSpec(
        num_scalar_prefetch=4,
        grid=(bsz, len(ti)),
        in_specs=[pl.BlockSpec((1, D_A, tq), qmap),
                  *idx_specs,
                  pl.BlockSpec((1, sk, D_IDX), lambda b, s, *_: (b, 0, 0)),
                  pl.BlockSpec((1, tk, D_A), lambda b, s, ti, tk_, tn, tl: (b, tk_[s], 0)),
                  pl.BlockSpec((1, D_A, tk), lambda b, s, ti, tk_, tn, tl: (b, 0, tk_[s])),
                  pl.BlockSpec((2, N_A, SUB, SUB), lambda b, s, *_: (0, 0, 0, 0))],
        out_specs=pl.BlockSpec((1, D_A, tq), qmap),
        scratch_shapes=[pltpu.VMEM((32, wrows, tq), I32),
                        pltpu.VMEM((wrows, tq), I32),
                        pltpu.VMEM((wrows, tq), I32),
                        pltpu.VMEM((wrows, tq), I32),
                        pltpu.VMEM((FAR_SUBS * SUB, tq), F32),
                        pltpu.VMEM((N_A, 8, tq), F32),
                        pltpu.VMEM((N_A, 8, tq), F32),
                        pltpu.VMEM((D_A, tq), F32),
                        pltpu.VMEM((N_A, FAR_SUBS * SUB, tq), F32),
                        pltpu.VMEM((N_A, 8, tq), F32)])
    return pl.pallas_call(
        body,
        grid_spec=grid_spec,
        out_shape=jax.ShapeDtypeStruct((bsz, D_A, sq), BF16),
        compiler_params=_cparams(("parallel", "arbitrary")),
    )(*tabs, qt, qit, wit, ki, k, vt, bias)


def _pack_cache_body(k4_ref, v4_ref, kn_ref, vnt_ref, ko_ref, vto_ref, *, ncache):
    c = pl.program_id(1)

    @pl.when(c < ncache)
    def _():
        for h in range(N_A):
            hs = slice(h * HD_A, (h + 1) * HD_A)
            ko_ref[0, :, hs] = k4_ref[0, pl.ds(h, PACK_FRAMES, stride=N_A), :].astype(BF16)
            vto_ref[0, hs, :] = v4_ref[0, pl.ds(h, PACK_FRAMES, stride=N_A), :].T.astype(BF16)

    @pl.when(c == ncache)
    def _():
        ko_ref[0] = kn_ref[0]
        vto_ref[0] = vnt_ref[0]


def _pack_cache(cache_k, cache_v, k_new, vt_new):
    bsz, past = cache_k.shape[0], cache_k.shape[1]
    blk = PACK_FRAMES
    ncache = past // blk
    cmap = lambda b, c: (b, jnp.minimum(c, ncache - 1), 0)
    rows = lambda a: a.reshape(bsz, past * N_A, HD_A)
    return pl.pallas_call(
        functools.partial(_pack_cache_body, ncache=ncache),
        grid=(bsz, ncache + 1),
        in_specs=[pl.BlockSpec((1, blk * N_A, HD_A), cmap),
                  pl.BlockSpec((1, blk * N_A, HD_A), cmap),
                  pl.BlockSpec((1, blk, D_A), lambda b, c: (b, 0, 0)),
                  pl.BlockSpec((1, D_A, blk), lambda b, c: (b, 0, 0))],
        out_specs=[pl.BlockSpec((1, blk, D_A), lambda b, c: (b, c, 0)),
                   pl.BlockSpec((1, D_A, blk), lambda b, c: (b, 0, c))],
        out_shape=[jax.ShapeDtypeStruct((bsz, past + blk, D_A), BF16),
                   jax.ShapeDtypeStruct((bsz, D_A, past + blk), BF16)],
        compiler_params=_cparams(("parallel", "arbitrary")),
    )(rows(cache_k), rows(cache_v), k_new, vt_new)


def _mlstm_body(q_ref, k_ref, kt_ref, v_ref, og_ref, sm_ref, gr_ref, gh_ref, c0_ref, n0_ref, m0_ref,
                hb_ref, co_ref, no_ref, mo_ref, c_s, n_s, m_s, *, lc, bb):
    c = pl.program_id(1)

    @pl.when(c == 0)
    def _():
        for s in range(bb):
            for h in range(N_B):
                c_s[s * N_B + h] = c0_ref[s, h]
                n_s[s * N_B + h] = jnp.broadcast_to(n0_ref[s, h:h + 1, :], (8, HD_B))
                m_s[s * N_B + h] = jnp.broadcast_to(m0_ref[s, h:h + 1, :], (8, LANES))

    row = lax.broadcasted_iota(I32, (lc, lc), 0)
    col = lax.broadcasted_iota(I32, (lc, lc), 1)
    causal = col <= row
    hp = lax.Precision.HIGHEST
    lower = causal.astype(F32)
    upper = (row <= col).astype(F32)

    for s in range(bb):
        sm = sm_ref[s]
        gr = gr_ref[s]
        b_cols = jnp.dot(lower, sm, precision=hp, preferred_element_type=F32)
        b_rows = jnp.dot(gr, upper, precision=hp, preferred_element_type=F32)
        for h in range(N_B):
            sh = s * N_B + h
            hs = slice(h * HD_B, (h + 1) * HD_B)
            m_prev = m_s[sh][0:1, 0:1]
            bc = b_cols[:, SM_LF + h:SM_LF + h + 1]
            ig_c = sm[:, SM_IG + h:SM_IG + h + 1]
            br = b_rows[N_B + h:N_B + h + 1, :]
            ig_r = gr[h:h + 1, :]
            log_d = jnp.where(causal, bc - br + ig_r, NEG)
            log_inter = bc + m_prev
            m_t = jnp.maximum(log_inter, jnp.max(log_d, axis=1, keepdims=True))
            d = jnp.exp(log_d - m_t)
            inter = jnp.exp(log_inter - m_t)
            qh = q_ref[s, :, hs]
            kth = kt_ref[s, hs, :]
            vh = v_ref[s, :, hs]
            sc = jnp.dot(qh, kth, preferred_element_type=F32) * d
            ch = c_s[sh]
            nrow = n_s[sh][0:1, :]
            num = (jnp.dot(sc.astype(BF16), vh, preferred_element_type=F32)
                   + inter * jnp.dot(qh, ch.astype(BF16), preferred_element_type=F32))
            den = (jnp.sum(sc, axis=1, keepdims=True)
                   + inter * jnp.sum(qh.astype(F32) * nrow, axis=1, keepdims=True))
            hh = num / jnp.maximum(jnp.abs(den), jnp.exp(-m_t))
            hb = _rms(hh, gh_ref[...]) * og_ref[s, :, hs].astype(F32)
            hb_ref[s, :, hs] = hb.astype(BF16)

            m_new = m_t[lc - 1:lc, :]
            b_last = bc[lc - 1:lc, :]
            decay = jnp.exp(b_last + m_prev - m_new)
            w_r = jnp.exp(b_last - br + ig_r - m_new)
            w_c = jnp.exp(b_last - bc + ig_c - m_new)
            kw = (kth.astype(F32) * w_r).astype(BF16)
            c_s[sh] = decay * ch + jnp.dot(kw, vh, preferred_element_type=F32)
            n_new = decay * nrow + jnp.sum(k_ref[s, :, hs].astype(F32) * w_c, axis=0, keepdims=True)
            n_s[sh] = jnp.broadcast_to(n_new, (8, HD_B))
            m_s[sh] = jnp.broadcast_to(m_new, (8, LANES))

    @pl.when(c == pl.num_programs(1) - 1)
    def _():
        for s in range(bb):
            for h in range(N_B):
                co_ref[s, h] = c_s[s * N_B + h]
                no_ref[s, h:h + 1, :] = n_s[s * N_B + h][0:1, :]
                mo_ref[s, h:h + 1, :] = m_s[s * N_B + h][0:1, :]


def _mlstm(q, k, kt, v, og, sm, gr, g_h, c0, n0, m0, *, lc):
    bsz, s = q.shape[0], q.shape[1]
    bb = 1
    tokb = lambda w: pl.BlockSpec((bb, lc, w), lambda b, c: (b, c, 0))
    st3 = lambda w: pl.BlockSpec((bb, N_B, w), lambda b, c: (b, 0, 0))
    st4 = pl.BlockSpec((bb, N_B, HD_B, HD_B), lambda b, c: (b, 0, 0, 0))
    return pl.pallas_call(
        functools.partial(_mlstm_body, lc=lc, bb=bb),
        grid=(bsz // bb, s // lc),
        in_specs=[tokb(D_B), tokb(D_B),
                  pl.BlockSpec((bb, D_B, lc), lambda b, c: (b, 0, c)),
                  tokb(D_B), tokb(D_B), tokb(LANES),
                  pl.BlockSpec((bb, 2 * N_B, lc), lambda b, c: (b, 0, c)),
                  pl.BlockSpec((1, HD_B), lambda b, c: (0, 0)),
                  st4, st3(HD_B), st3(LANES)],
        out_specs=[tokb(D_B), st4, st3(HD_B), st3(LANES)],
        out_shape=[jax.ShapeDtypeStruct((bsz, s, D_B), BF16),
                   jax.ShapeDtypeStruct((bsz, N_B, HD_B, HD_B), F32),
                   jax.ShapeDtypeStruct((bsz, N_B, HD_B), F32),
                   jax.ShapeDtypeStruct((bsz, N_B, LANES), F32)],
        scratch_shapes=[pltpu.VMEM((bb * N_B, HD_B, HD_B), F32),
                        pltpu.VMEM((bb * N_B, 8, HD_B), F32),
                        pltpu.VMEM((bb * N_B, 8, LANES), F32)],
        compiler_params=_cparams(("parallel", "arbitrary")),
    )(q, k, kt, v, og, sm, gr, g_h, c0, n0, m0)


def _mixer_front(x, ffn1, inp):
    x1 = _ffn(x, *ffn1)
    return (x1,) + tuple(_inproj(x1, *inp))


def kernel(x_prompt, x_sample, cache_k, cache_v, cache_k_idx, state_C, state_n, state_m, g_ffn1, w1_ffn1, w3_ffn1, w2_ffn1, g_mix, w_in, g_q, g_k, g_kidx, rel_bias, b_i, b_f, g_h, w_out, g_ffn2, w1_ffn2, w3_ffn2, w2_ffn2):
    bsz, seq = x_prompt.shape[0], x_prompt.shape[1]
    dbs, dseq = x_sample.shape[0], x_sample.shape[1]
    past = cache_k.shape[1]

    ffn1 = _prep_ffn(g_ffn1, w1_ffn1, w3_ffn1, w2_ffn1)
    ffn2 = _prep_ffn(g_ffn2, w1_ffn2, w3_ffn2, w2_ffn2)
    inp = _prep_inproj(g_mix, w_in, g_q, g_k, g_kidx, b_i, b_f)
    wa = w_out[:D_A].astype(BF16)
    wh = w_out[D_A:].astype(BF16)
    gh = g_h.reshape(1, HD_B)
    bias = _bias_tiles(rel_bias)

    n_p = bsz * seq
    x1, q, kf, k16, vf, v16, qi, qb, kb, vb, og, sm = _mixer_front(x_prompt.reshape(n_p, D_MODEL), ffn1, inp)
    r3 = lambda a, b_, s_: a.reshape(b_, s_, a.shape[-1])
    sm3 = r3(sm, bsz, seq)
    tr = lambda a: jnp.swapaxes(a, 1, 2)
    tq_p = min(SUB, seq)
    attn = tr(_dsa(tr(r3(q, bsz, seq)), tr(r3(qi, bsz, seq)), tr(sm3[:, :, SM_WI:SM_IG]),
                   sm3[:, :, :D_IDX].astype(BF16), r3(k16, bsz, seq), tr(r3(v16, bsz, seq)), bias,
                   tq=tq_p, tk=min(1024, seq), q_off=0, l_valid=seq, topk=min(TOPK_MAX, seq // 4)))
    gr = jnp.swapaxes(sm3[:, :, SM_IG:SM_END], 1, 2)
    kb3 = r3(kb, bsz, seq)
    hb, c_p, n_p_, m_p = _mlstm(r3(qb, bsz, seq), kb3, jnp.swapaxes(kb3, 1, 2), r3(vb, bsz, seq),
                                r3(og, bsz, seq), sm3, gr, gh,
                                jnp.zeros((bsz, N_B, HD_B, HD_B), F32), jnp.zeros((bsz, N_B, HD_B), F32),
                                jnp.zeros((bsz, N_B, LANES), F32), lc=min(MLSTM_CHUNK, seq))
    x2 = _outproj(x1, attn.reshape(n_p, D_A), hb.reshape(n_p, D_B), wa, wh)
    y_prompt = _ffn(x2, *ffn2).reshape(bsz, seq, D_MODEL)

    n_s = dbs * dseq
    x1s, qs, kfs, k16s, vfs, v16s, qis, qbs, kbs, vbs, ogs, sms = _mixer_front(
        x_sample.reshape(n_s, D_MODEL), ffn1, inp)
    sms3 = r3(sms, dbs, dseq)
    l_all = past + dseq
    sk = past + PACK_FRAMES
    tk_s = sk // 2
    padk = lambda a: jnp.pad(a, ((0, 0), (0, sk - l_all), (0, 0)))
    assert past % PACK_FRAMES == 0 and dseq <= PACK_FRAMES and tk_s % SUB == 0
    padn = lambda a: jnp.pad(a, ((0, 0), (0, PACK_FRAMES - dseq), (0, 0)))
    k_all, vt_all = _pack_cache(cache_k, cache_v, padn(r3(k16s, dbs, dseq)), tr(padn(r3(v16s, dbs, dseq))))
    ki_all = padk(jnp.concatenate([cache_k_idx.astype(BF16), sms3[:, :, :D_IDX].astype(BF16)], axis=1))
    assert dseq <= LANES
    padq = lambda a: jnp.pad(tr(a), ((0, 0), (0, 0), (0, LANES - dseq)))
    qi_s, wi_s = r3(qis, dbs, dseq), sms3[:, :, SM_WI:SM_IG]
    packed = H_IDX * dseq == 2 * LANES
    if packed:
        qit_s = qi_s.reshape(dbs, dseq, H_IDX, D_IDX).transpose(0, 3, 2, 1).reshape(dbs, D_IDX, 2 * LANES)
        wit_s = tr(wi_s).reshape(dbs, 1, 2 * LANES)
    else:
        qit_s, wit_s = padq(qi_s), padq(wi_s)
    attn_s = tr(_dsa(padq(r3(qs, dbs, dseq)), qit_s, wit_s, ki_all, k_all, vt_all, bias,
                     tq=LANES, tk=tk_s, q_off=past, l_valid=l_all, topk=min(TOPK_MAX, l_all // 4),
                     packed=packed)[:, :, :dseq])
    lc_s = LANES
    padt = lambda a: jnp.pad(a, ((0, 0), (0, lc_s - dseq), (0, 0)))
    lane = jnp.arange(LANES)
    sm_pad = jnp.where((lane >= SM_IG) & (lane < SM_LF), NEG, 0.0).astype(F32)
    sms_p = jnp.concatenate([sms3, jnp.broadcast_to(sm_pad, (dbs, lc_s - dseq, LANES))], axis=1)
    kbs3 = padt(r3(kbs, dbs, dseq))
    hbs, c_s, n_s_, m_s = _mlstm(padt(r3(qbs, dbs, dseq)), kbs3, jnp.swapaxes(kbs3, 1, 2), padt(r3(vbs, dbs, dseq)),
                                 padt(r3(ogs, dbs, dseq)), sms_p, jnp.swapaxes(sms_p[:, :, SM_IG:SM_END], 1, 2), gh,
                                 state_C.astype(F32), state_n.astype(F32),
                                 jnp.broadcast_to(state_m.astype(F32)[:, :, None], (dbs, N_B, LANES)), lc=lc_s)
    x2s = _outproj(x1s, attn_s.reshape(n_s, D_A), hbs[:, :dseq].reshape(n_s, D_B), wa, wh)
    y_sample = _ffn(x2s, *ffn2).reshape(dbs, dseq, D_MODEL)

    sd = state_C.dtype
    return (y_prompt, y_sample,
            kf.reshape(bsz, seq, N_A, HD_A), vf.reshape(bsz, seq, N_A, HD_A), sm3[:, :, :D_IDX],
            c_p.astype(sd), n_p_.astype(sd), m_p[:, :, 0].astype(sd),
            kfs.reshape(dbs, dseq, N_A, HD_A), vfs.reshape(dbs, dseq, N_A, HD_A), sms3[:, :, :D_IDX],
            c_s.astype(sd), n_s_.astype(sd), m_s[:, :, 0].astype(sd))
```
